```python
import math
import jax, jax.numpy as jnp
from jax import lax
import numpy as np

D_MODEL = 1024
BATCH = 8
SEQ = 2048
DEPTH = 1
DEC_BATCH = 128
DEC_SEQ = 4
PAST_LEN = 2048
PAGE_SIZE = 128

N_HEADS_SB = 8
HEAD_DIM = 64
SB_WIDTH = N_HEADS_SB * HEAD_DIM
SB_SCALE = HEAD_DIM ** -0.5
SB_BIAS_INIT = -8.0
Q_BLOCK = 128
N_HEADS_HG = 8
HG_KEY_DIM = 64
HG_VAL_DIM = 64
HG_QK = N_HEADS_HG * HG_KEY_DIM
HG_V = N_HEADS_HG * HG_VAL_DIM
HG_CHUNK = 64
MIX_WIDTH = SB_WIDTH + HG_V
N_IN = 3 * SB_WIDTH + 2 * HG_QK + 2 * HG_V
SPLITS = (SB_WIDTH, 2 * SB_WIDTH, 3 * SB_WIDTH, 3 * SB_WIDTH + HG_QK,
          3 * SB_WIDTH + 2 * HG_QK, 3 * SB_WIDTH + 2 * HG_QK + HG_V)
D_FF = 4 * D_MODEL
N_ADA = 6
EPS = 1e-6
F32 = jnp.float32

kernel_name = "hymba_stickbreak_hgrn2_step"


def rms_norm(x, g):
    xf = x.astype(F32)
    y = xf * lax.rsqrt(jnp.mean(xf * xf, axis=-1, keepdims=True) + EPS)
    return (y * g.astype(F32)).astype(x.dtype)


def stick_breaking(q, k, v, q_pos, bias):
    z = jnp.einsum('bthd,bshd->bhts', q.astype(F32), k.astype(F32)) * SB_SCALE \
        + bias.astype(F32)[None, :, None, None]
    valid = jnp.arange(k.shape[1])[None, :] < q_pos[:, None]
    log_1m = jnp.where(valid, jax.nn.log_sigmoid(-z), 0.0)
    rev = lax.cumsum(log_1m, axis=3, reverse=True)
    tail = jnp.concatenate([rev[..., 1:], jnp.zeros_like(rev[..., :1])], axis=-1)
    w = jnp.where(valid, jnp.exp(jax.nn.log_sigmoid(z) + tail), 0.0)
    return jnp.einsum('bhts,bshd->bthd', w, v.astype(F32))


def hgrn2_chunked(q, k, v, log_f, s0):
    B, T, H, DK = q.shape
    C = math.gcd(T, HG_CHUNK)
    n = T // C
    rs = lambda a: a.reshape(B, n, C, H, a.shape[-1])
    q, k, v, g = rs(q), rs(k), rs(v), rs(log_f)
    b = jnp.cumsum(g, axis=2)
    b_last = b[:, :, -1:]
    q_dec = q * jnp.exp(b)
    k_dec = k * jnp.exp(-b)
    causal = jnp.tril(jnp.ones((C, C), dtype=bool))
    att = jnp.where(causal, jnp.einsum('bnthk,bnshk->bnhts', q_dec, k_dec), 0.0)
    o_intra = jnp.einsum('bnhts,bnshv->bnthv', att, v)
    d_state = jnp.einsum('bnshk,bnshv->bnhkv', k * jnp.exp(b_last - b), v)
    decay = jnp.exp(b_last[:, :, 0])

    def step(S, inp):
        dS, dec = inp
        return dec[..., None] * S + dS, S

    s_final, s_start = lax.scan(step, s0, (jnp.swapaxes(d_state, 0, 1), jnp.swapaxes(decay, 0, 1)))
    s_start = jnp.swapaxes(s_start, 0, 1)
    o_inter = jnp.einsum('bnthk,bnhkv->bnthv', q_dec, s_start)
    return (o_intra + o_inter).reshape(B, T, H, v.shape[-1]), s_final


def decoder_layer(x, c, w_ada, b_ada, n1, n2, w_in, qg, kg, sb_b, lb, hg_out_g, w_out, w_up, w_down,
                  k_past, v_past, s0):
    B, T, _ = x.shape
    ada = (jax.nn.silu(c) @ w_ada + b_ada)[:, None, :]
    sh1, sc1, g1, sh2, sc2, g2 = jnp.split(ada, N_ADA, axis=-1)
    h = rms_norm(x, n1) * (1 + sc1) + sh1
    proj = h @ w_in
    q_a, k_a, v_a, q_b, f_b, i_b, g_b = jnp.split(proj, SPLITS, axis=-1)
    q_a = rms_norm(q_a.reshape(B, T, N_HEADS_SB, HEAD_DIM), qg)
    k_a = rms_norm(k_a.reshape(B, T, N_HEADS_SB, HEAD_DIM), kg)
    v_a = v_a.reshape(B, T, N_HEADS_SB, HEAD_DIM)
    if k_past is None:
        nb = T // Q_BLOCK
        qb = jnp.swapaxes(q_a.reshape(B, nb, Q_BLOCK, N_HEADS_SB, HEAD_DIM), 0, 1)
        pos = jnp.arange(T).reshape(nb, Q_BLOCK)
        o_a = lax.map(lambda a: stick_breaking(a[0], k_a, v_a, a[1], sb_b), (qb, pos))
        o_a = jnp.swapaxes(o_a, 0, 1).reshape(B, T, SB_WIDTH)
    else:
        P = k_past.shape[1]
        k_all = jnp.concatenate([k_past.astype(k_a.dtype), k_a], axis=1)
        v_all = jnp.concatenate([v_past.astype(v_a.dtype), v_a], axis=1)
        o_a = stick_breaking(q_a, k_all, v_all, P + jnp.arange(T), sb_b).reshape(B, T, SB_WIDTH)
    f = lb + (1.0 - lb) * jax.nn.sigmoid(f_b.astype(F32))
    hq = jax.nn.silu(q_b.astype(F32)).reshape(B, T, N_HEADS_HG, HG_KEY_DIM)
    hk = (1.0 - f).reshape(B, T, N_HEADS_HG, HG_KEY_DIM)
    hlog_f = jnp.log(f).reshape(B, T, N_HEADS_HG, HG_KEY_DIM)
    hv = i_b.astype(F32).reshape(B, T, N_HEADS_HG, HG_VAL_DIM)
    o_b, s_new = hgrn2_chunked(hq, hk, hv, hlog_f, s0.astype(F32))
    o_b = rms_norm(o_b, hg_out_g) * jax.nn.silu(g_b.astype(F32)).reshape(B, T, N_HEADS_HG, HG_VAL_DIM)
    mix = jnp.concatenate([o_a.astype(x.dtype), o_b.reshape(B, T, HG_V).astype(x.dtype)], axis=-1) @ w_out
    x = x + g1 * mix
    h2 = rms_norm(x, n2) * (1 + sc2) + sh2
    x = x + g2 * (jnp.square(jax.nn.relu(h2 @ w_up)) @ w_down)
    return x, k_a, v_a, s_new.astype(x.dtype)


def setup_inputs(seed: int = 0) -> dict:
    key = jax.random.key(seed)
    ks = jax.random.split(key, 24)
    n_pages = PAST_LEN // PAGE_SIZE
    n_used = DEC_BATCH * n_pages
    n_phys = n_used + (n_used + 3) // 4
    nrm = lambda k, s, sc=1.0: jax.random.normal(k, s, F32) * sc
    perm = jax.random.permutation(ks[0], n_phys)
    page_table = perm[:n_used].reshape(DEC_BATCH, n_pages).astype(jnp.int32)
    return {
        "x_prompt": nrm(ks[1], (BATCH, SEQ, D_MODEL)),
        "x_sample": nrm(ks[2], (DEC_BATCH, DEC_SEQ, D_MODEL)),
        "cache_k": nrm(ks[3], (DEPTH, n_phys, PAGE_SIZE, N_HEADS_SB, HEAD_DIM)),
        "cache_v": nrm(ks[4], (DEPTH, n_phys, PAGE_SIZE, N_HEADS_SB, HEAD_DIM)),
        "state_hgrn": nrm(ks[5], (DEPTH, DEC_BATCH, N_HEADS_HG, HG_KEY_DIM, HG_VAL_DIM), 0.5),
        "page_table": page_table,
        "c_prompt": nrm(ks[6], (BATCH, D_MODEL)),
        "c_sample": nrm(ks[7], (DEC_BATCH, D_MODEL)),
        "w_ada": nrm(ks[8], (DEPTH, D_MODEL, N_ADA * D_MODEL), D_MODEL ** -0.5),
        "b_ada": nrm(ks[9], (DEPTH, N_ADA * D_MODEL), 0.02),
        "norm1_g": 1.0 + nrm(ks[10], (DEPTH, D_MODEL), 0.02),
        "norm2_g": 1.0 + nrm(ks[11], (DEPTH, D_MODEL), 0.02),
        "w_in": nrm(ks[12], (DEPTH, D_MODEL, N_IN), D_MODEL ** -0.5),
        "q_norm_g": 1.0 + nrm(ks[13], (DEPTH, HEAD_DIM), 0.02),
        "k_norm_g": 1.0 + nrm(ks[14], (DEPTH, HEAD_DIM), 0.02),
        "sb_bias": SB_BIAS_INIT + nrm(ks[20], (DEPTH, N_HEADS_SB), 0.1),
        "hg_lb_logits": nrm(ks[15], (DEPTH + 1, HG_QK), 0.1),
        "hg_out_g": 1.0 + nrm(ks[16], (DEPTH, HG_VAL_DIM), 0.02),
        "w_out": nrm(ks[17], (DEPTH, MIX_WIDTH, D_MODEL), MIX_WIDTH ** -0.5),
        "w_up": nrm(ks[18], (DEPTH, D_MODEL, D_FF), D_MODEL ** -0.5),
        "w_down": nrm(ks[19], (DEPTH, D_FF, D_MODEL), D_FF ** -0.5),
    }


def reference(x_prompt, x_sample, cache_k, cache_v, state_hgrn, page_table, c_prompt, c_sample,
              w_ada, b_ada, norm1_g, norm2_g, w_in, q_norm_g, k_norm_g, sb_bias, hg_lb_logits, hg_out_g,
              w_out, w_up, w_down):
    lb_sched = jnp.cumsum(jax.nn.softmax(hg_lb_logits.astype(F32), axis=0), axis=0)
    db = x_sample.shape[0]
    s0_prompt = jnp.zeros((x_prompt.shape[0], N_HEADS_HG, HG_KEY_DIM, HG_VAL_DIM), F32)
    y_p, y_s = x_prompt, x_sample
    kp, vp, ks_, vs_, sp, ss = [], [], [], [], [], []
    for l in range(DEPTH):
        lw = (w_ada[l], b_ada[l], norm1_g[l], norm2_g[l], w_in[l], q_norm_g[l], k_norm_g[l],
              sb_bias[l], lb_sched[l], hg_out_g[l], w_out[l], w_up[l], w_down[l])
        y_p, k_new, v_new, s_new = decoder_layer(y_p, c_prompt, *lw, None, None, s0_prompt)
        kp.append(k_new); vp.append(v_new); sp.append(s_new)
        k_past = cache_k[l][page_table].reshape(db, -1, N_HEADS_SB, HEAD_DIM)
        v_past = cache_v[l][page_table].reshape(db, -1, N_HEADS_SB, HEAD_DIM)
        y_s, k_new, v_new, s_new = decoder_layer(y_s, c_sample, *lw, k_past, v_past, state_hgrn[l])
        ks_.append(k_new); vs_.append(v_new); ss.append(s_new)
    return (y_p, y_s, jnp.stack(kp), jnp.stack(vp), jnp.stack(ks_), jnp.stack(vs_), jnp.stack(sp), jnp.stack(ss))
```

```python
import functools

import jax
import jax.numpy as jnp
from jax import lax
from jax.experimental import pallas as pl
from jax.experimental.pallas import tpu as pltpu

F32 = jnp.float32
BF16 = jnp.bfloat16

D_MODEL = 1024
N_HEADS = 8
HEAD_DIM = 64
GROUP_W = N_HEADS * HEAD_DIM
N_GROUPS = 7
D_FF = 4 * D_MODEL
N_ADA = 6
EPS = 1e-6
SB_SCALE = HEAD_DIM ** -0.5
HG_CHUNK = 64
PAGE = 128
LANES = 128
SUBLANES = 8
VMEM_LIMIT = 48 * 1024 * 1024


def _dot(a, b):
    return jnp.dot(a, b, preferred_element_type=F32)


def _dot_nt(a, b):
    return lax.dot_general(a, b, (((1,), (1,)), ((), ())), preferred_element_type=F32)


def _dot_tn(a, b):
    return lax.dot_general(a, b, (((0,), (0,)), ((), ())), preferred_element_type=F32)


def _split_bf16(x, n):
    parts = []
    r = x
    for i in range(n):
        p = r.astype(BF16)
        parts.append(p)
        if i + 1 < n:
            r = r - p.astype(F32)
    return parts


def _silu(x):
    return x * jax.nn.sigmoid(x)


def _head_norm(y, gain, bd):
    hi, lo = _split_bf16(y * y, 2)
    half = 2 * LANES
    ss = jnp.concatenate(
        [_dot(hi[:, c:c + half], bd) + _dot(lo[:, c:c + half], bd) for c in range(0, GROUP_W, half)],
        axis=1)
    return y * lax.rsqrt(ss * (1.0 / HEAD_DIM) + EPS) * gain


def _log1m_beta(z):
    return -(jnp.maximum(z, 0.0) + jnp.log(1.0 + jnp.exp(-jnp.abs(z))))


def _tail_sums(l1m, tri):
    hi, lo = _split_bf16(l1m, 2)
    cs2 = _dot(hi, tri) + _dot(lo, tri)
    return cs2[:, :LANES], cs2[:, LANES:]


def _ada_kernel(c_ref, w_ref, b_ref, o_ref):
    s = _silu(c_ref[...]).astype(BF16)
    o_ref[...] = _dot(s, w_ref[...].astype(BF16)) + b_ref[...]


def _ada(c_all, w_ada, b_ada):
    m = c_all.shape[0]
    n = w_ada.shape[1]
    tn = 1024
    return pl.pallas_call(
        _ada_kernel,
        grid=(n // tn,),
        in_specs=[pl.BlockSpec((m, D_MODEL), lambda j: (0, 0)),
                  pl.BlockSpec((D_MODEL, tn), lambda j: (0, j)),
                  pl.BlockSpec((1, tn), lambda j: (0, j))],
        out_specs=pl.BlockSpec((m, tn), lambda j: (0, j)),
        out_shape=jax.ShapeDtypeStruct((m, n), F32),
        compiler_params=pltpu.CompilerParams(dimension_semantics=("parallel",),
                                             vmem_limit_bytes=VMEM_LIMIT),
        name="ada",
    )(c_all, w_ada, b_ada.reshape(1, n))


def _proj_kernel(x_ref, sc_ref, sh_ref, n1_ref, w_ref, qg_ref, kg_ref, lbl_ref, bd_ref,
                 q_ref, k_ref, v_ref, hq_ref, lf_ref, i_ref, g_ref):
    x = x_ref[...]
    h = x * lax.rsqrt(jnp.mean(x * x, axis=-1, keepdims=True) + EPS) * n1_ref[...]
    hb = (h * (1.0 + sc_ref[...]) + sh_ref[...]).astype(BF16)
    bd = bd_ref[...]

    def group(g):
        return _dot(hb, w_ref[:, g * GROUP_W:(g + 1) * GROUP_W])

    q_ref[...] = (_head_norm(group(0), qg_ref[...], bd) * SB_SCALE).astype(BF16)
    k_ref[...] = _head_norm(group(1), kg_ref[...], bd)
    v_ref[...] = group(2)
    hq_ref[...] = _silu(group(3)).astype(BF16)
    lbl = lbl_ref[...]
    e = jnp.exp(lbl - jnp.max(lbl, axis=0, keepdims=True))
    lb = e[0:1, :] / jnp.sum(e, axis=0, keepdims=True)
    f = lb + (1.0 - lb) * jax.nn.sigmoid(group(4))
    lf_ref[...] = jnp.log(f)
    i_ref[...] = group(5).astype(BF16)
    g_ref[...] = _silu(group(6)).astype(BF16)


def _proj(x3, sc, sh, n1, w_in_bf, qg_t, kg_t, lb_logits, bd, tm):
    bx, r, _ = x3.shape
    rm = sc.shape[1]
    nt = r // tm
    mod_spec = pl.BlockSpec((None, 1 if rm == 1 else tm, D_MODEL),
                            (lambda b, i: (b, 0, 0)) if rm == 1 else (lambda b, i: (b, i, 0)))
    const = lambda shape: pl.BlockSpec(shape, lambda b, i: (0,) * len(shape))
    out_spec = pl.BlockSpec((tm, GROUP_W), lambda b, i: (b * nt + i, 0))
    n = bx * r
    out_dtypes = (BF16, F32, F32, BF16, F32, BF16, BF16)
    return pl.pallas_call(
        _proj_kernel,
        grid=(bx, nt),
        in_specs=[pl.BlockSpec((None, tm, D_MODEL), lambda b, i: (b, i, 0)),
                  mod_spec, mod_spec,
                  const((1, D_MODEL)),
                  const((D_MODEL, N_GROUPS * GROUP_W)),
                  const((1, GROUP_W)), const((1, GROUP_W)),
                  const(lb_logits.shape),
                  const((2 * LANES, 2 * LANES))],
        out_specs=[out_spec] * N_GROUPS,
        out_shape=[jax.ShapeDtypeStruct((n, GROUP_W), dt) for dt in out_dtypes],
        compiler_params=pltpu.CompilerParams(dimension_semantics=("parallel", "parallel"),
                                             vmem_limit_bytes=VMEM_LIMIT),
        name="proj",
    )(x3, sc, sh, n1, w_in_bf, qg_t, kg_t, lb_logits, bd)


def _sb_prompt_kernel(bias_ref, q_ref, k_ref, v_ref, tri_ref, o_ref):
    p = pl.program_id(1)
    i = pl.program_id(2)
    tri = tri_ref[...]
    q = q_ref[...]
    lane = lax.broadcasted_iota(jnp.int32, (LANES, LANES), 1)
    row = lax.broadcasted_iota(jnp.int32, (LANES, LANES), 0)
    zero = jnp.zeros_like(q)
    qs = (jnp.where(lane < HEAD_DIM, q, zero), jnp.where(lane >= HEAD_DIM, q, zero))
    bs = (bias_ref[2 * p], bias_ref[2 * p + 1])
    strictly_before = lane < row

    def block(j, carry, masked):
        rs, accs = carry
        start = pl.multiple_of(j * LANES, LANES)
        kb = k_ref[pl.ds(start, LANES), :].astype(BF16)
        vb = v_ref[pl.ds(start, LANES), :].astype(BF16)
        new_rs, new_accs = [], []
        for hh in range(2):
            z = _dot_nt(qs[hh], kb) + bs[hh]
            l1m = _log1m_beta(z)
            if masked:
                l1m = jnp.where(strictly_before, l1m, 0.0)
            cs, tot = _tail_sums(l1m, tri)
            w = jnp.exp(z + cs + rs[hh])
            if masked:
                w = jnp.where(strictly_before, w, 0.0)
            new_rs.append(rs[hh] + tot)
            new_accs.append(accs[hh] + _dot(w.astype(BF16), vb))
        return tuple(new_rs), tuple(new_accs)

    zeros = jnp.zeros((LANES, LANES), F32)
    carry = block(i, ((zeros, zeros), (zeros, zeros)), True)
    carry = lax.fori_loop(0, i, lambda jj, c: block(i - 1 - jj, c, False), carry)
    _, accs = carry
    o_ref[...] = jnp.where(lane < HEAD_DIM, accs[0], accs[1]).astype(BF16)


def _sb_prompt(q, k, v, sb_bias, tri, b, t):
    nq = t // LANES
    q3 = q.reshape(b, t, GROUP_W)
    k3 = k.reshape(b, t, GROUP_W)
    v3 = v.reshape(b, t, GROUP_W)
    out = pl.pallas_call(
        _sb_prompt_kernel,
        grid=(b, N_HEADS // 2, nq),
        in_specs=[pl.BlockSpec(memory_space=pltpu.SMEM),
                  pl.BlockSpec((None, LANES, LANES), lambda bb, p, i: (bb, i, p)),
                  pl.BlockSpec((None, t, LANES), lambda bb, p, i: (bb, 0, p)),
                  pl.BlockSpec((None, t, LANES), lambda bb, p, i: (bb, 0, p)),
                  pl.BlockSpec((LANES, 2 * LANES), lambda bb, p, i: (0, 0))],
        out_specs=pl.BlockSpec((None, LANES, LANES), lambda bb, p, i: (bb, i, p)),
        out_shape=jax.ShapeDtypeStruct((b, t, GROUP_W), BF16),
        compiler_params=pltpu.CompilerParams(
            dimension_semantics=("parallel", "parallel", "arbitrary"),
            vmem_limit_bytes=VMEM_LIMIT),
        name="sb_prompt",
    )(sb_bias, q3, k3, v3, tri)
    return out.reshape(b * t, GROUP_W)


N_QROWS = N_HEADS * SUBLANES


def _sb_decode_kernel(pt_ref, q_ref, kn_ref, vn_ref, kc_ref, vc_ref, bias_ref, tri_ref,
                      o_ref, qm_ref, r_ref, acc_ref):
    del pt_ref
    j = pl.program_id(1)
    nj = pl.num_programs(1)
    tri = tri_ref[...]
    row = lax.broadcasted_iota(jnp.int32, (N_QROWS, GROUP_W), 0)
    lane = lax.broadcasted_iota(jnp.int32, (N_QROWS, GROUP_W), 1)
    own_head = (row // SUBLANES) == (lane // HEAD_DIM)

    def step(kb, vb, valid):
        z = _dot_nt(qm_ref[...], kb) + bias_ref[...]
        l1m = _log1m_beta(z)
        if valid is not None:
            l1m = jnp.where(valid, l1m, 0.0)
        cs, tot = _tail_sums(l1m, tri)
        w = jnp.exp(z + cs + r_ref[...])
        if valid is not None:
            w = jnp.where(valid, w, 0.0)
        r_ref[...] += tot
        acc_ref[...] += _dot(w.astype(BF16), vb)

    @pl.when(j == 0)
    def _():
        qrep = jnp.concatenate([q_ref[...]] * N_HEADS, axis=0)
        qm_ref[...] = jnp.where(own_head, qrep, 0.0).astype(BF16)
        r_ref[...] = jnp.zeros_like(r_ref)
        acc_ref[...] = jnp.zeros_like(acc_ref)
        pad = jnp.zeros((PAGE - SUBLANES, GROUP_W), F32)
        kb = jnp.concatenate([kn_ref[...], pad], axis=0).astype(BF16)
        vb = jnp.concatenate([vn_ref[...], pad], axis=0).astype(BF16)
        r2 = lax.broadcasted_iota(jnp.int32, (N_QROWS, PAGE), 0)
        c2 = lax.broadcasted_iota(jnp.int32, (N_QROWS, PAGE), 1)
        step(kb, vb, c2 < (r2 % SUBLANES))

    @pl.when(j > 0)
    def _():
        step(kc_ref[...].astype(BF16), vc_ref[...].astype(BF16), None)

    @pl.when(j == nj - 1)
    def _():
        acc = jnp.where(own_head, acc_ref[...], 0.0)
        o = acc[0:SUBLANES]
        for h in range(1, N_HEADS):
            o = o + acc[h * SUBLANES:(h + 1) * SUBLANES]
        o_ref[...] = o


def _sb_decode(q8, kn8, vn8, kc, vc, page_table, bias_bc, tri):
    db = q8.shape[0]
    n_pages = page_table.shape[1]
    pt_flat = page_table.reshape(-1)

    def page_map(s, j, pt):
        return (pt[s * n_pages + (n_pages - jnp.maximum(j, 1))], 0, 0)

    seq_spec = pl.BlockSpec((None, SUBLANES, GROUP_W), lambda s, j, pt: (s, 0, 0))
    grid_spec = pltpu.PrefetchScalarGridSpec(
        num_scalar_prefetch=1,
        grid=(db, n_pages + 1),
        in_specs=[seq_spec, seq_spec, seq_spec,
                  pl.BlockSpec((None, PAGE, GROUP_W), page_map),
                  pl.BlockSpec((None, PAGE, GROUP_W), page_map),
                  pl.BlockSpec((N_QROWS, LANES), lambda s, j, pt: (0, 0)),
                  pl.BlockSpec((LANES, 2 * LANES), lambda s, j, pt: (0, 0))],
        out_specs=seq_spec,
        scratch_shapes=[pltpu.VMEM((N_QROWS, GROUP_W), BF16),
                        pltpu.VMEM((N_QROWS, LANES), F32),
                        pltpu.VMEM((N_QROWS, GROUP_W), F32)])
    return pl.pallas_call(
        _sb_decode_kernel,
        grid_spec=grid_spec,
        out_shape=jax.ShapeDtypeStruct((db, SUBLANES, GROUP_W), F32),
        compiler_params=pltpu.CompilerParams(dimension_semantics=("parallel", "arbitrary"),
                                             vmem_limit_bytes=VMEM_LIMIT),
        name="sb_decode",
    )(pt_flat, q8, kn8, vn8, kc, vc, bias_bc, tri)


def _hgrn_kernel(hq_ref, lf_ref, iv_ref, sg_ref, s0_ref, og_ref, bd_ref, ltri_ref, ones_ref,
                 o_ref, sout_ref, s_scr, *, chunk, n_chunks):
    i = pl.program_id(1)

    @pl.when(i == 0)
    def _():
        s_scr[...] = s0_ref[...]

    ltri = ltri_ref[...]
    ones_sq = ones_ref[...]
    rr = lax.broadcasted_iota(jnp.int32, (chunk, chunk), 0)
    cc = lax.broadcasted_iota(jnp.int32, (chunk, chunk), 1)
    causal = cc <= rr
    er = lax.broadcasted_iota(jnp.int32, (HEAD_DIM, HEAD_DIM), 0)
    ec = lax.broadcasted_iota(jnp.int32, (HEAD_DIM, HEAD_DIM), 1)
    eye = er == ec

    def chunk_body(c, _):
        rows = pl.ds(pl.multiple_of(c * chunk, chunk), chunk)
        lf = lf_ref[rows, :]
        b = sum(_dot(ltri, part) for part in _split_bf16(lf, 3))
        b_last = b[chunk - 1:chunk, :]
        hk = 1.0 - jnp.exp(lf)
        qd = (hq_ref[rows, :].astype(F32) * jnp.exp(b)).astype(BF16)
        kd = (hk * jnp.exp(-b)).astype(BF16)
        kk = (hk * jnp.exp(b_last - b)).astype(BF16)
        v = iv_ref[rows, :].astype(BF16)
        outs = []
        for h in range(N_HEADS):
            sl = slice(h * HEAD_DIM, (h + 1) * HEAD_DIM)
            qd_h, v_h = qd[:, sl], v[:, sl]
            att = jnp.where(causal, _dot_nt(qd_h, kd[:, sl]), 0.0)
            s = s_scr[h]
            outs.append(_dot(att.astype(BF16), v_h) + _dot(qd_h, s.astype(BF16)))
            diag = jnp.where(eye, jnp.broadcast_to(b_last[:, sl], (HEAD_DIM, HEAD_DIM)), 0.0)
            b_col = sum(_dot(part, ones_sq) for part in _split_bf16(diag, 3))
            s_scr[h] = jnp.exp(b_col) * s + _dot_tn(kk[:, sl], v_h)
        o = jnp.concatenate(outs, axis=1)
        o_ref[rows, :] = (_head_norm(o, og_ref[...], bd_ref[...]) * sg_ref[rows, :].astype(F32)).astype(o_ref.dtype)
        return 0

    lax.fori_loop(0, n_chunks, chunk_body, 0)

    @pl.when(i == pl.num_programs(1) - 1)
    def _():
        sout_ref[...] = s_scr[...]


def _hgrn(hq, lf, iv, sg, s0, og_t, bd, bx, r, chunk, tc):
    shp = (bx, r, GROUP_W)
    nt = r // tc
    tok_spec = pl.BlockSpec((None, tc, GROUP_W), lambda b, i: (b, i, 0))
    state_spec = pl.BlockSpec((None, N_HEADS, HEAD_DIM, HEAD_DIM), lambda b, i: (b, 0, 0, 0))
    const = lambda shape: pl.BlockSpec(shape, lambda b, i: (0,) * len(shape))
    ltri = (jnp.arange(chunk)[None, :] <= jnp.arange(chunk)[:, None]).astype(BF16)
    ones_sq = jnp.ones((HEAD_DIM, HEAD_DIM), BF16)
    o, s_new = pl.pallas_call(
        functools.partial(_hgrn_kernel, chunk=chunk, n_chunks=tc // chunk),
        grid=(bx, nt),
        in_specs=[tok_spec, tok_spec, tok_spec, tok_spec, state_spec,
                  const((1, GROUP_W)), const((2 * LANES, 2 * LANES)),
                  const((chunk, chunk)), const((HEAD_DIM, HEAD_DIM))],
        out_specs=[tok_spec, state_spec],
        out_shape=[jax.ShapeDtypeStruct(shp, hq.dtype),
                   jax.ShapeDtypeStruct((bx, N_HEADS, HEAD_DIM, HEAD_DIM), F32)],
        scratch_shapes=[pltpu.VMEM((N_HEADS, HEAD_DIM, HEAD_DIM), F32)],
        compiler_params=pltpu.CompilerParams(dimension_semantics=("parallel", "arbitrary"),
                                             vmem_limit_bytes=VMEM_LIMIT),
        name="hgrn",
    )(hq.reshape(shp), lf.reshape(shp), iv.reshape(shp), sg.reshape(shp), s0, og_t, bd, ltri, ones_sq)
    return o, s_new


def _out_kernel(x_ref, oa_ref, ob_ref, w_ref, g1_ref, sc_ref, sh_ref, n2_ref, x1_ref, h2_ref):
    mix = (_dot(oa_ref[...].astype(BF16), w_ref[0:GROUP_W, :])
           + _dot(ob_ref[...].astype(BF16), w_ref[GROUP_W:2 * GROUP_W, :]))
    x1 = x_ref[...] + g1_ref[...] * mix
    x1_ref[...] = x1
    h2 = x1 * lax.rsqrt(jnp.mean(x1 * x1, axis=-1, keepdims=True) + EPS) * n2_ref[...]
    h2_ref[...] = (h2 * (1.0 + sc_ref[...]) + sh_ref[...]).astype(BF16)


def _out_proj(x3, oa, ob, w_out_bf, g1, sc, sh, n2, tm):
    bx, r, _ = x3.shape
    rm = g1.shape[1]
    nt = r // tm
    mod_spec = pl.BlockSpec((None, 1 if rm == 1 else tm, D_MODEL),
                            (lambda b, i: (b, 0, 0)) if rm == 1 else (lambda b, i: (b, i, 0)))
    row_spec = pl.BlockSpec((None, tm, D_MODEL), lambda b, i: (b, i, 0))
    half_spec = pl.BlockSpec((None, tm, GROUP_W), lambda b, i: (b, i, 0))
    const = lambda shape: pl.BlockSpec(shape, lambda b, i: (0,) * len(shape))
    return pl.pallas_call(
        _out_kernel,
        grid=(bx, nt),
        in_specs=[row_spec, half_spec, half_spec, const((2 * GROUP_W, D_MODEL)),
                  mod_spec, mod_spec, mod_spec, const((1, D_MODEL))],
        out_specs=[row_spec, row_spec],
        out_shape=[jax.ShapeDtypeStruct(x3.shape, F32), jax.ShapeDtypeStruct(x3.shape, BF16)],
        compiler_params=pltpu.CompilerParams(dimension_semantics=("parallel", "parallel"),
                                             vmem_limit_bytes=VMEM_LIMIT),
        name="out_proj",
    )(x3, oa.reshape(bx, r, GROUP_W), ob.reshape(bx, r, GROUP_W), w_out_bf, g1, sc, sh, n2)


def _mlp_kernel(h_ref, x1_ref, g2_ref, wu_ref, wd_ref, o_ref, acc_ref):
    f = pl.program_id(2)
    u = _dot(h_ref[...], wu_ref[...])
    a = jnp.square(jnp.maximum(u, 0.0)).astype(BF16)
    part = _dot(a, wd_ref[...])

    @pl.when(f == 0)
    def _():
        acc_ref[...] = part

    @pl.when(f > 0)
    def _():
        acc_ref[...] += part

    @pl.when(f == pl.num_programs(2) - 1)
    def _():
        o_ref[...] = x1_ref[...] + g2_ref[...] * acc_ref[...]


def _mlp(h2, x1, g2, w_up_bf, w_down_bf, tm, tf):
    bx, r, _ = x1.shape
    rm = g2.shape[1]
    nt = r // tm
    mod_spec = pl.BlockSpec((None, 1 if rm == 1 else tm, D_MODEL),
                            (lambda b, i, f: (b, 0, 0)) if rm == 1 else (lambda b, i, f: (b, i, 0)))
    row_spec = pl.BlockSpec((None, tm, D_MODEL), lambda b, i, f: (b, i, 0))
    return pl.pallas_call(
        _mlp_kernel,
        grid=(bx, nt, D_FF // tf),
        in_specs=[row_spec, row_spec, mod_spec,
                  pl.BlockSpec((D_MODEL, tf), lambda b, i, f: (0, f)),
                  pl.BlockSpec((tf, D_MODEL), lambda b, i, f: (f, 0))],
        out_specs=row_spec,
        out_shape=jax.ShapeDtypeStruct(x1.shape, F32),
        scratch_shapes=[pltpu.VMEM((tm, D_MODEL), F32)],
        compiler_params=pltpu.CompilerParams(
            dimension_semantics=("parallel", "parallel", "arbitrary"),
            vmem_limit_bytes=VMEM_LIMIT),
        name="mlp",
    )(h2, x1, g2, w_up_bf, w_down_bf)


def _pad_rows(a, rows):
    return jnp.pad(a, ((0, 0), (0, rows - a.shape[1]), (0, 0)))


def kernel(x_prompt, x_sample, cache_k, cache_v, state_hgrn, page_table, c_prompt, c_sample,
           w_ada, b_ada, norm1_g, norm2_g, w_in, q_norm_g, k_norm_g, sb_bias, hg_lb_logits, hg_out_g,
           w_out, w_up, w_down):
    assert w_ada.shape[0] == 1 and hg_lb_logits.shape[0] == 2, "single-layer step"
    b, t, _ = x_prompt.shape
    db, dt, _ = x_sample.shape
    n_phys = cache_k.shape[1]

    w_in_bf = w_in[0].astype(BF16)
    w_out_bf = w_out[0].astype(BF16)
    w_up_bf = w_up[0].astype(BF16)
    w_down_bf = w_down[0].astype(BF16)
    n1 = norm1_g[0].reshape(1, D_MODEL)
    n2 = norm2_g[0].reshape(1, D_MODEL)
    qg_t = jnp.tile(q_norm_g[0], N_HEADS).reshape(1, GROUP_W)
    kg_t = jnp.tile(k_norm_g[0], N_HEADS).reshape(1, GROUP_W)
    og_t = jnp.tile(hg_out_g[0], N_HEADS).reshape(1, GROUP_W)
    bias = sb_bias[0].astype(F32)
    bias_bc = jnp.broadcast_to(jnp.repeat(bias, SUBLANES)[:, None], (N_QROWS, LANES))
    idx = jnp.arange(2 * LANES)
    bd = (idx[:, None] // HEAD_DIM == idx[None, :] // HEAD_DIM).astype(BF16)
    kk = jnp.arange(LANES)
    tri = jnp.concatenate([(kk[:, None] >= kk[None, :]).astype(BF16), jnp.ones((LANES, LANES), BF16)], axis=1)

    ada = _ada(jnp.concatenate([c_prompt, c_sample], axis=0), w_ada[0], b_ada[0])
    ada_p = ada[:b].reshape(b, 1, N_ADA, D_MODEL)
    ada_s = jnp.repeat(ada[b:], dt, axis=0).reshape(1, db * dt, N_ADA, D_MODEL)
    mods_p = [ada_p[:, :, m] for m in range(N_ADA)]
    mods_s = [ada_s[:, :, m] for m in range(N_ADA)]

    q, k_p, v_p, hq, lf, iv, sg = _proj(x_prompt, mods_p[1], mods_p[0], n1, w_in_bf, qg_t, kg_t,
                                         hg_lb_logits, bd, tm=512)
    oa = _sb_prompt(q, k_p, v_p, bias, tri, b, t)
    s0 = jnp.zeros((b, N_HEADS, HEAD_DIM, HEAD_DIM), F32)
    ob, s_p = _hgrn(hq, lf, iv, sg, s0, og_t, bd, b, t, HG_CHUNK, tc=512)
    x1, h2 = _out_proj(x_prompt, oa, ob, w_out_bf, mods_p[2], mods_p[4], mods_p[3], n2, tm=512)
    y_p = _mlp(h2, x1, mods_p[5], w_up_bf, w_down_bf, tm=1024, tf=512)

    ns = db * dt
    xs3 = x_sample.reshape(1, ns, D_MODEL)
    q, k_s, v_s, hq, lf, iv, sg = _proj(xs3, mods_s[1], mods_s[0], n1, w_in_bf, qg_t, kg_t,
                                         hg_lb_logits, bd, tm=ns)
    seq = lambda a: _pad_rows(a.reshape(db, dt, GROUP_W).astype(F32), SUBLANES)
    kc = cache_k[0].reshape(n_phys, PAGE, GROUP_W)
    vc = cache_v[0].reshape(n_phys, PAGE, GROUP_W)
    oa8 = _sb_decode(seq(q), seq(k_s), seq(v_s), kc, vc, page_table, bias_bc, tri)
    ob8, s_s = _hgrn(seq(hq), seq(lf), seq(iv), seq(sg), state_hgrn[0], og_t, bd,
                     db, SUBLANES, SUBLANES, tc=SUBLANES)
    oa = oa8[:, :dt].reshape(ns, GROUP_W)
    ob = ob8[:, :dt].reshape(ns, GROUP_W)
    x1, h2 = _out_proj(xs3, oa, ob, w_out_bf, mods_s[2], mods_s[4], mods_s[3], n2, tm=ns)
    y_s = _mlp(h2, x1, mods_s[5], w_up_bf, w_down_bf, tm=ns, tf=512)

    heads = lambda a, bb, tt: a.reshape(1, bb, tt, N_HEADS, HEAD_DIM)
    return (y_p, y_s.reshape(db, dt, D_MODEL),
            heads(k_p, b, t), heads(v_p, b, t), heads(k_s, db, dt), heads(v_s, db, dt),
            s_p[None], s_s[None])
```

```python
import functools
import math

import jax
import jax.numpy as jnp
from jax import lax
from jax.experimental import pallas as pl
from jax.experimental.pallas import tpu as pltpu

F32 = jnp.float32
BF16 = jnp.bfloat16

D_MODEL = 1024
N_HEADS = 8
HEAD_DIM = 64
GROUP_W = N_HEADS * HEAD_DIM
N_GROUPS = 7
D_FF = 4 * D_MODEL
N_ADA = 6
EPS = 1e-6
SB_SCALE = HEAD_DIM ** -0.5
LOG2E = math.log2(math.e)
HG_CHUNK = 64
PAGE = 128
LANES = 128
SUBLANES = 8
VMEM_LIMIT = 48 * 1024 * 1024
TQ = 512
PAGES_PER_STEP = 4


def _dot(a, b):
    return jnp.dot(a, b, preferred_element_type=F32)


def _dot_nt(a, b):
    return lax.dot_general(a, b, (((1,), (1,)), ((), ())), preferred_element_type=F32)


def _dot_tn(a, b):
    return lax.dot_general(a, b, (((0,), (0,)), ((), ())), preferred_element_type=F32)


def _split_bf16(x, n):
    parts = []
    r = x
    for i in range(n):
        p = r.astype(BF16)
        parts.append(p)
        if i + 1 < n:
            r = r - p.astype(F32)
    return parts


def _silu(x):
    return x * jax.nn.sigmoid(x)


def _head_norm(y, gain, bd):
    hi, lo = _split_bf16(y * y, 2)
    half = 2 * LANES
    ss = jnp.concatenate(
        [_dot(hi[:, c:c + half], bd) + _dot(lo[:, c:c + half], bd) for c in range(0, GROUP_W, half)],
        axis=1)
    return y * lax.rsqrt(ss * (1.0 / HEAD_DIM) + EPS) * gain


def _head_norm_cm(yt, gain, bd):
    hi, lo = _split_bf16(yt * yt, 2)
    half = 2 * LANES
    ss = jnp.concatenate(
        [_dot(bd, hi[c:c + half]) + _dot(bd, lo[c:c + half]) for c in range(0, GROUP_W, half)],
        axis=0)
    return yt * lax.rsqrt(ss * (1.0 / HEAD_DIM) + EPS) * gain


def _log2_1m_beta(z2):
    return -(jnp.maximum(z2, 0.0) + jnp.log(1.0 + jnp.exp2(-jnp.abs(z2))) * LOG2E)


def _hi_lo(x):
    hi, lo = _split_bf16(x, 2)
    return jnp.concatenate([hi, lo], axis=1)


def _ada_kernel(c_ref, w_ref, b_ref, o_ref):
    s = _silu(c_ref[...]).astype(BF16)
    o_ref[...] = _dot(s, w_ref[...].astype(BF16)) + b_ref[...]


def _ada(c_all, w_ada, b_ada):
    m = c_all.shape[0]
    n = w_ada.shape[1]
    tn = 1024
    return pl.pallas_call(
        _ada_kernel,
        grid=(n // tn,),
        in_specs=[pl.BlockSpec((m, D_MODEL), lambda j: (0, 0)),
                  pl.BlockSpec((D_MODEL, tn), lambda j: (0, j)),
                  pl.BlockSpec((1, tn), lambda j: (0, j))],
        out_specs=pl.BlockSpec((m, tn), lambda j: (0, j)),
        out_shape=jax.ShapeDtypeStruct((m, n), F32),
        compiler_params=pltpu.CompilerParams(dimension_semantics=("parallel",),
                                             vmem_limit_bytes=VMEM_LIMIT),
        name="ada",
    )(c_all, w_ada, b_ada.reshape(1, n))


def _proj_kernel(x_ref, sc_ref, sh_ref, n1_ref, w_ref, wkv_t_ref, qg_ref, kg_ref, lbl_ref, bd_ref,
                 q_ref, k_ref, v_ref, hq_ref, lf_ref, i_ref, g_ref, *, channel_major_kv):
    x = x_ref[...]
    h = x * lax.rsqrt(jnp.mean(x * x, axis=-1, keepdims=True) + EPS) * n1_ref[...]
    hb = (h * (1.0 + sc_ref[...]) + sh_ref[...]).astype(BF16)
    bd = bd_ref[...]

    def group(g):
        return _dot(hb, w_ref[:, g * GROUP_W:(g + 1) * GROUP_W])

    q_ref[...] = (_head_norm(group(0), qg_ref[...], bd) * (SB_SCALE * LOG2E)).astype(BF16)
    if channel_major_kv:
        k_ref[...] = _head_norm_cm(_dot_nt(wkv_t_ref[0:GROUP_W, :], hb), kg_ref[...], bd)
        v_ref[...] = _dot_nt(wkv_t_ref[GROUP_W:2 * GROUP_W, :], hb)
    else:
        k_ref[...] = _head_norm(group(1), kg_ref[...], bd)
        v_ref[...] = group(2)
    hq_ref[...] = _silu(group(3)).astype(BF16)
    lbl = lbl_ref[...]
    e = jnp.exp(lbl - jnp.max(lbl, axis=0, keepdims=True))
    lb = e[0:1, :] / jnp.sum(e, axis=0, keepdims=True)
    f = lb + (1.0 - lb) * jax.nn.sigmoid(group(4))
    lf_ref[...] = jnp.log(f)
    i_ref[...] = group(5).astype(BF16)
    g_ref[...] = _silu(group(6)).astype(BF16)


def _proj(x3, sc, sh, n1, w_in_bf, wkv_t, qg_t, kg, lb_logits, bd, tm, channel_major_kv):
    bx, r, _ = x3.shape
    rm = sc.shape[1]
    nt = r // tm
    mod_spec = pl.BlockSpec((None, 1 if rm == 1 else tm, D_MODEL),
                            (lambda b, i: (b, 0, 0)) if rm == 1 else (lambda b, i: (b, i, 0)))
    const = lambda shape: pl.BlockSpec(shape, lambda b, i: (0,) * len(shape))
    row_spec = pl.BlockSpec((tm, GROUP_W), lambda b, i: (b * nt + i, 0))
    n = bx * r
    row_shape = lambda dt: jax.ShapeDtypeStruct((n, GROUP_W), dt)
    if channel_major_kv:
        kv_spec = pl.BlockSpec((None, GROUP_W, tm), lambda b, i: (b, 0, i))
        kv_shape = jax.ShapeDtypeStruct((bx, GROUP_W, r), F32)
    else:
        kv_spec, kv_shape = row_spec, row_shape(F32)
    return pl.pallas_call(
        functools.partial(_proj_kernel, channel_major_kv=channel_major_kv),
        grid=(bx, nt),
        in_specs=[pl.BlockSpec((None, tm, D_MODEL), lambda b, i: (b, i, 0)),
                  mod_spec, mod_spec,
                  const((1, D_MODEL)),
                  const((D_MODEL, N_GROUPS * GROUP_W)),
                  const((2 * GROUP_W, D_MODEL)),
                  const((1, GROUP_W)), const(kg.shape),
                  const(lb_logits.shape),
                  const((2 * LANES, 2 * LANES))],
        out_specs=[row_spec, kv_spec, kv_spec, row_spec, row_spec, row_spec, row_spec],
        out_shape=[row_shape(BF16), kv_shape, kv_shape, row_shape(BF16), row_shape(F32),
                   row_shape(BF16), row_shape(BF16)],
        compiler_params=pltpu.CompilerParams(dimension_semantics=("parallel", "parallel"),
                                             vmem_limit_bytes=VMEM_LIMIT),
        name="proj",
    )(x3, sc, sh, n1, w_in_bf, wkv_t, qg_t, kg, lb_logits, bd)


def _sb_prompt_kernel(bias_ref, q_ref, kt_ref, vt_ref, tri_ref, o_ref,
                      r_ref, acc_ref, z0_ref, hl0_ref, z1_ref, hl1_ref):
    p = pl.program_id(1)
    i = pl.program_id(2)
    nsub = TQ // LANES
    lane2 = lax.broadcasted_iota(jnp.int32, (1, 2 * LANES), 1)
    bias_row = jnp.where(lane2 < LANES, bias_ref[2 * p], bias_ref[2 * p + 1])
    zeros_half = jnp.zeros((HEAD_DIM, LANES), F32)

    def stacked(ref, start):
        blk = ref[:, pl.ds(start, LANES)]
        top = jnp.concatenate([blk[:HEAD_DIM], zeros_half], axis=0)
        bot = jnp.concatenate([zeros_half, blk[HEAD_DIM:]], axis=0)
        return jnp.concatenate([top, bot], axis=1).astype(BF16)

    def strictly_before(m):
        row = lax.broadcasted_iota(jnp.int32, (m, LANES), 0)
        col = lax.broadcasted_iota(jnp.int32, (m, LANES), 1)
        return col < row

    def stage_a(rows, start, masked, z_ref, hl_ref):
        m = rows.stop - rows.start
        z = _dot(q_ref[rows, :], stacked(kt_ref, start)) + bias_row
        l = _log2_1m_beta(z)
        z_ref[rows, :] = z
        for hh in range(2):
            lh = l[:, hh * LANES:(hh + 1) * LANES]
            if masked:
                lh = jnp.where(strictly_before(m), lh, 0.0)
            hl_ref[rows, 2 * hh * LANES:2 * (hh + 1) * LANES] = _hi_lo(lh)

    def stage_b(rows, start, masked, z_ref, hl_ref):
        m = rows.stop - rows.start
        tri = tri_ref[...]
        ws = []
        for hh in range(2):
            cs2 = _dot(hl_ref[rows, 2 * hh * LANES:2 * (hh + 1) * LANES], tri)
            w = jnp.exp2(z_ref[rows, hh * LANES:(hh + 1) * LANES] + cs2[:, :LANES] + r_ref[hh, rows, :])
            if masked:
                w = jnp.where(strictly_before(m), w, 0.0)
            r_ref[hh, rows, :] += cs2[:, LANES:]
            ws.append(w.astype(BF16))
        acc_ref[rows, :] += _dot_nt(jnp.concatenate(ws, axis=1), stacked(vt_ref, start))

    r_ref[...] = jnp.zeros_like(r_ref)
    acc_ref[...] = jnp.zeros_like(acc_ref)
    slots = ((z0_ref, hl0_ref), (z1_ref, hl1_ref))

    for c in reversed(range(nsub)):
        rows = slice(c * LANES, TQ)
        start = pl.multiple_of(i * TQ + c * LANES, LANES)
        stage_a(rows, start, True, *slots[c % 2])
        stage_b(rows, start, True, *slots[c % 2])

    @pl.when(i > 0)
    def _():
        n = i * nsub
        rows = slice(0, TQ)
        key = lambda k: pl.multiple_of((n - 1 - k) * LANES, LANES)
        stage_a(rows, key(0), False, *slots[0])

        def body(kk, carry):
            k = 2 * kk
            stage_a(rows, key(k + 1), False, *slots[1])
            stage_b(rows, key(k), False, *slots[0])
            stage_a(rows, key(k + 2), False, *slots[0])
            stage_b(rows, key(k + 1), False, *slots[1])
            return carry

        lax.fori_loop(0, n // 2 - 1, body, 0)
        stage_a(rows, key(n - 1), False, *slots[1])
        stage_b(rows, key(n - 2), False, *slots[0])
        stage_b(rows, key(n - 1), False, *slots[1])

    o_ref[...] = acc_ref[...].astype(BF16)


def _sb_prompt(q, kt, vt, bias2, tri2, b, t):
    nq = t // TQ
    q3 = q.reshape(b, t, GROUP_W)
    pair_cm = pl.BlockSpec((None, LANES, t), lambda bb, p, i: (bb, p, 0))
    tile = pl.BlockSpec((None, TQ, LANES), lambda bb, p, i: (bb, i, p))
    out = pl.pallas_call(
        _sb_prompt_kernel,
        grid=(b, N_HEADS // 2, nq),
        in_specs=[pl.BlockSpec(memory_space=pltpu.SMEM), tile, pair_cm, pair_cm,
                  pl.BlockSpec((2 * LANES, 2 * LANES), lambda bb, p, i: (0, 0))],
        out_specs=tile,
        out_shape=jax.ShapeDtypeStruct((b, t, GROUP_W), BF16),
        scratch_shapes=[pltpu.VMEM((2, TQ, LANES), F32), pltpu.VMEM((TQ, LANES), F32),
                        pltpu.VMEM((TQ, 2 * LANES), F32), pltpu.VMEM((TQ, 4 * LANES), BF16),
                        pltpu.VMEM((TQ, 2 * LANES), F32), pltpu.VMEM((TQ, 4 * LANES), BF16)],
        compiler_params=pltpu.CompilerParams(
            dimension_semantics=("parallel", "parallel", "arbitrary"),
            vmem_limit_bytes=VMEM_LIMIT),
        name="sb_prompt",
    )(bias2, q3, kt, vt, tri2)
    return out.reshape(b * t, GROUP_W)


N_QROWS = N_HEADS * SUBLANES


def _sb_decode_kernel(pt_ref, q_ref, kn_ref, vn_ref, *rest):
    del pt_ref
    npg = PAGES_PER_STEP
    k_refs, v_refs = rest[:npg], rest[npg:2 * npg]
    bias_ref, tri_ref, o_ref, qm_ref, r_ref, acc_ref = rest[2 * npg:]
    g = pl.program_id(1)
    tri = tri_ref[...]
    row = lax.broadcasted_iota(jnp.int32, (N_QROWS, GROUP_W), 0)
    lane = lax.broadcasted_iota(jnp.int32, (N_QROWS, GROUP_W), 1)
    own_head = (row // SUBLANES) == (lane // HEAD_DIM)

    def block(z, r, acc, pv, valid):
        l = _log2_1m_beta(z)
        if valid is not None:
            l = jnp.where(valid, l, 0.0)
        cs2 = _dot(_hi_lo(l), tri)
        w = jnp.exp2(z + cs2[:, :LANES] + r)
        if valid is not None:
            w = jnp.where(valid, w, 0.0)
        return r + cs2[:, LANES:], acc + pv(w.astype(BF16))

    @pl.when(g == 0)
    def _():
        qrep = jnp.concatenate([q_ref[...]] * N_HEADS, axis=0)
        qm = jnp.where(own_head, qrep, 0.0).astype(BF16)
        qm_ref[...] = qm
        pad = jnp.zeros((PAGE - SUBLANES, GROUP_W), F32)
        kb = jnp.concatenate([kn_ref[...], pad], axis=0).astype(BF16)
        vb = jnp.concatenate([vn_ref[...], pad], axis=0).astype(BF16)
        r2 = lax.broadcasted_iota(jnp.int32, (N_QROWS, PAGE), 0)
        c2 = lax.broadcasted_iota(jnp.int32, (N_QROWS, PAGE), 1)
        r, acc = block(_dot_nt(qm, kb) + bias_ref[...],
                       jnp.zeros((N_QROWS, LANES), F32), jnp.zeros((N_QROWS, GROUP_W), F32),
                       lambda w: _dot(w, vb), c2 < (r2 % SUBLANES))
        r_ref[...] = r
        acc_ref[...] = acc

    r, acc = r_ref[...], acc_ref[...]
    qm = qm_ref[...]
    for j in range(npg):
        vt = v_refs[j][...].astype(BF16)
        r, acc = block(_dot(qm, k_refs[j][...].astype(BF16)) + bias_ref[...], r, acc,
                       lambda w, vt=vt: _dot_nt(w, vt), None)
    r_ref[...] = r
    acc_ref[...] = acc

    @pl.when(g == pl.num_programs(1) - 1)
    def _():
        own = jnp.where(own_head, acc, 0.0)
        o = own[0:SUBLANES]
        for h in range(1, N_HEADS):
            o = o + own[h * SUBLANES:(h + 1) * SUBLANES]
        o_ref[...] = o


def _sb_decode(q8, kn8, vn8, kc_cm, vc_cm, page_table, bias_bc, tri2):
    db = q8.shape[0]
    n_pages = page_table.shape[1]
    npg = PAGES_PER_STEP
    pt_flat = page_table.reshape(-1)

    def page_spec(j):
        return pl.BlockSpec((None, GROUP_W, PAGE),
                            lambda s, g, pt: (pt[s * n_pages + n_pages - 1 - (g * npg + j)], 0, 0))

    seq_spec = pl.BlockSpec((None, SUBLANES, GROUP_W), lambda s, g, pt: (s, 0, 0))
    grid_spec = pltpu.PrefetchScalarGridSpec(
        num_scalar_prefetch=1,
        grid=(db, n_pages // npg),
        in_specs=[seq_spec, seq_spec, seq_spec]
                 + [page_spec(j) for j in range(npg)] + [page_spec(j) for j in range(npg)]
                 + [pl.BlockSpec((N_QROWS, LANES), lambda s, g, pt: (0, 0)),
                    pl.BlockSpec((2 * LANES, 2 * LANES), lambda s, g, pt: (0, 0))],
        out_specs=seq_spec,
        scratch_shapes=[pltpu.VMEM((N_QROWS, GROUP_W), BF16),
                        pltpu.VMEM((N_QROWS, LANES), F32),
                        pltpu.VMEM((N_QROWS, GROUP_W), F32)])
    return pl.pallas_call(
        _sb_decode_kernel,
        grid_spec=grid_spec,
        out_shape=jax.ShapeDtypeStruct((db, SUBLANES, GROUP_W), F32),
        compiler_params=pltpu.CompilerParams(dimension_semantics=("parallel", "arbitrary"),
                                             vmem_limit_bytes=VMEM_LIMIT),
        name="sb_decode",
    )(pt_flat, q8, kn8, vn8, *([kc_cm] * npg), *([vc_cm] * npg), bias_bc, tri2)


def _hgrn_kernel(hq_ref, lf_ref, iv_ref, sg_ref, s0_ref, og_ref, bd_ref, ltri_ref, ones_ref,
                 o_ref, sout_ref, s_scr, *, chunk, n_chunks, nb):
    i = pl.program_id(1)

    @pl.when(i == 0)
    def _():
        s_scr[...] = s0_ref[...]

    ltri = ltri_ref[...]
    ones_sq = ones_ref[...]
    rr = lax.broadcasted_iota(jnp.int32, (chunk, chunk), 0)
    cc = lax.broadcasted_iota(jnp.int32, (chunk, chunk), 1)
    causal = cc <= rr
    er = lax.broadcasted_iota(jnp.int32, (HEAD_DIM, HEAD_DIM), 0)
    ec = lax.broadcasted_iota(jnp.int32, (HEAD_DIM, HEAD_DIM), 1)
    eye = er == ec

    for bi in range(nb):
        states = [s_scr[bi, h] for h in range(N_HEADS)]
        for c in range(n_chunks):
            rows = slice(c * chunk, (c + 1) * chunk)
            lf = lf_ref[bi, rows, :]
            b = sum(_dot(ltri, part) for part in _split_bf16(lf, 3))
            b_last = b[chunk - 1:chunk, :]
            hk = 1.0 - jnp.exp(lf)
            qd = (hq_ref[bi, rows, :].astype(F32) * jnp.exp(b)).astype(BF16)
            kd = (hk * jnp.exp(-b)).astype(BF16)
            kk = (hk * jnp.exp(b_last - b)).astype(BF16)
            v = iv_ref[bi, rows, :].astype(BF16)
            outs = []
            for h in range(N_HEADS):
                sl = slice(h * HEAD_DIM, (h + 1) * HEAD_DIM)
                qd_h, v_h = qd[:, sl], v[:, sl]
                att = jnp.where(causal, _dot_nt(qd_h, kd[:, sl]), 0.0)
                s = states[h]
                outs.append(_dot(att.astype(BF16), v_h) + _dot(qd_h, s.astype(BF16)))
                diag = jnp.where(eye, jnp.broadcast_to(b_last[:, sl], (HEAD_DIM, HEAD_DIM)), 0.0)
                b_col = sum(_dot(part, ones_sq) for part in _split_bf16(diag, 3))
                states[h] = jnp.exp(b_col) * s + _dot_tn(kk[:, sl], v_h)
            o = jnp.concatenate(outs, axis=1)
            gated = _head_norm(o, og_ref[...], bd_ref[...]) * sg_ref[bi, rows, :].astype(F32)
            o_ref[bi, rows, :] = gated.astype(o_ref.dtype)
        for h in range(N_HEADS):
            s_scr[bi, h] = states[h]

    @pl.when(i == pl.num_programs(1) - 1)
    def _():
        sout_ref[...] = s_scr[...]


def _hgrn(hq, lf, iv, sg, s0, og_t, bd, bx, r, chunk, tc, nb):
    shp = (bx, r, GROUP_W)
    tok_spec = pl.BlockSpec((nb, tc, GROUP_W), lambda b, i: (b, i, 0))
    state_spec = pl.BlockSpec((nb, N_HEADS, HEAD_DIM, HEAD_DIM), lambda b, i: (b, 0, 0, 0))
    const = lambda shape: pl.BlockSpec(shape, lambda b, i: (0,) * len(shape))
    ltri = (jnp.arange(chunk)[None, :] <= jnp.arange(chunk)[:, None]).astype(BF16)
    ones_sq = jnp.ones((HEAD_DIM, HEAD_DIM), BF16)
    o, s_new = pl.pallas_call(
        functools.partial(_hgrn_kernel, chunk=chunk, n_chunks=tc // chunk, nb=nb),
        grid=(bx // nb, r // tc),
        in_specs=[tok_spec, tok_spec, tok_spec, tok_spec, state_spec,
                  const((1, GROUP_W)), const((2 * LANES, 2 * LANES)),
                  const((chunk, chunk)), const((HEAD_DIM, HEAD_DIM))],
        out_specs=[tok_spec, state_spec],
        out_shape=[jax.ShapeDtypeStruct(shp, hq.dtype),
                   jax.ShapeDtypeStruct((bx, N_HEADS, HEAD_DIM, HEAD_DIM), F32)],
        scratch_shapes=[pltpu.VMEM((nb, N_HEADS, HEAD_DIM, HEAD_DIM), F32)],
        compiler_params=pltpu.CompilerParams(dimension_semantics=("parallel", "arbitrary"),
                                             vmem_limit_bytes=VMEM_LIMIT),
        name="hgrn",
    )(hq.reshape(shp), lf.reshape(shp), iv.reshape(shp), sg.reshape(shp), s0, og_t, bd, ltri, ones_sq)
    return o, s_new


def _out_kernel(x_ref, oa_ref, ob_ref, w_ref, g1_ref, sc_ref, sh_ref, n2_ref, x1_ref, h2_ref):
    mix = (_dot(oa_ref[...].astype(BF16), w_ref[0:GROUP_W, :])
           + _dot(ob_ref[...].astype(BF16), w_ref[GROUP_W:2 * GROUP_W, :]))
    x1 = x_ref[...] + g1_ref[...] * mix
    x1_ref[...] = x1
    h2 = x1 * lax.rsqrt(jnp.mean(x1 * x1, axis=-1, keepdims=True) + EPS) * n2_ref[...]
    h2_ref[...] = (h2 * (1.0 + sc_ref[...]) + sh_ref[...]).astype(BF16)


def _out_proj(x3, oa, ob, w_out_bf, g1, sc, sh, n2, tm):
    bx, r, _ = x3.shape
    rm = g1.shape[1]
    nt = r // tm
    mod_spec = pl.BlockSpec((None, 1 if rm == 1 else tm, D_MODEL),
                            (lambda b, i: (b, 0, 0)) if rm == 1 else (lambda b, i: (b, i, 0)))
    row_spec = pl.BlockSpec((None, tm, D_MODEL), lambda b, i: (b, i, 0))
    half_spec = pl.BlockSpec((None, tm, GROUP_W), lambda b, i: (b, i, 0))
    const = lambda shape: pl.BlockSpec(shape, lambda b, i: (0,) * len(shape))
    return pl.pallas_call(
        _out_kernel,
        grid=(bx, nt),
        in_specs=[row_spec, half_spec, half_spec, const((2 * GROUP_W, D_MODEL)),
                  mod_spec, mod_spec, mod_spec, const((1, D_MODEL))],
        out_specs=[row_spec, row_spec],
        out_shape=[jax.ShapeDtypeStruct(x3.shape, F32), jax.ShapeDtypeStruct(x3.shape, BF16)],
        compiler_params=pltpu.CompilerParams(dimension_semantics=("parallel", "parallel"),
                                             vmem_limit_bytes=VMEM_LIMIT),
        name="out_proj",
    )(x3, oa.reshape(bx, r, GROUP_W), ob.reshape(bx, r, GROUP_W), w_out_bf, g1, sc, sh, n2)


def _mlp_kernel(h_ref, x1_ref, g2_ref, wu_ref, wd_ref, o_ref, acc_ref):
    f = pl.program_id(2)
    u = _dot(h_ref[...], wu_ref[...])
    a = jnp.square(jnp.maximum(u, 0.0)).astype(BF16)
    part = _dot(a, wd_ref[...])

    @pl.when(f == 0)
    def _():
        acc_ref[...] = part

    @pl.when(f > 0)
    def _():
        acc_ref[...] += part

    @pl.when(f == pl.num_programs(2) - 1)
    def _():
        o_ref[...] = x1_ref[...] + g2_ref[...] * acc_ref[...]


def _mlp(h2, x1, g2, w_up_bf, w_down_bf, tm, tf):
    bx, r, _ = x1.shape
    rm = g2.shape[1]
    nt = r // tm
    mod_spec = pl.BlockSpec((None, 1 if rm == 1 else tm, D_MODEL),
                            (lambda b, i, f: (b, 0, 0)) if rm == 1 else (lambda b, i, f: (b, i, 0)))
    row_spec = pl.BlockSpec((None, tm, D_MODEL), lambda b, i, f: (b, i, 0))
    return pl.pallas_call(
        _mlp_kernel,
        grid=(bx, nt, D_FF // tf),
        in_specs=[row_spec, row_spec, mod_spec,
                  pl.BlockSpec((D_MODEL, tf), lambda b, i, f: (0, f)),
                  pl.BlockSpec((tf, D_MODEL), lambda b, i, f: (f, 0))],
        out_specs=row_spec,
        out_shape=jax.ShapeDtypeStruct(x1.shape, F32),
        scratch_shapes=[pltpu.VMEM((tm, D_MODEL), F32)],
        compiler_params=pltpu.CompilerParams(
            dimension_semantics=("parallel", "parallel", "arbitrary"),
            vmem_limit_bytes=VMEM_LIMIT),
        name="mlp",
    )(h2, x1, g2, w_up_bf, w_down_bf)


def _pad_rows(a, rows):
    return jnp.pad(a, ((0, 0), (0, rows - a.shape[1]), (0, 0)))


def kernel(x_prompt, x_sample, cache_k, cache_v, state_hgrn, page_table, c_prompt, c_sample,
           w_ada, b_ada, norm1_g, norm2_g, w_in, q_norm_g, k_norm_g, sb_bias, hg_lb_logits, hg_out_g,
           w_out, w_up, w_down):
    assert w_ada.shape[0] == 1 and hg_lb_logits.shape[0] == 2, "single-layer step"
    b, t, _ = x_prompt.shape
    db, dt, _ = x_sample.shape
    n_phys = cache_k.shape[1]
    tm_p = 512

    w_in_bf = w_in[0].astype(BF16)
    wkv_t = w_in[0][:, GROUP_W:3 * GROUP_W].T.astype(BF16)
    w_out_bf = w_out[0].astype(BF16)
    w_up_bf = w_up[0].astype(BF16)
    w_down_bf = w_down[0].astype(BF16)
    n1 = norm1_g[0].reshape(1, D_MODEL)
    n2 = norm2_g[0].reshape(1, D_MODEL)
    qg_t = jnp.tile(q_norm_g[0], N_HEADS).reshape(1, GROUP_W)
    kg_t = jnp.tile(k_norm_g[0], N_HEADS).reshape(1, GROUP_W)
    kg_cm = jnp.broadcast_to(kg_t.reshape(GROUP_W, 1), (GROUP_W, tm_p))
    og_t = jnp.tile(hg_out_g[0], N_HEADS).reshape(1, GROUP_W)
    bias2 = sb_bias[0].astype(F32) * LOG2E
    bias_bc = jnp.broadcast_to(jnp.repeat(bias2, SUBLANES)[:, None], (N_QROWS, LANES))
    idx = jnp.arange(2 * LANES)
    bd = (idx[:, None] // HEAD_DIM == idx[None, :] // HEAD_DIM).astype(BF16)
    kk = jnp.arange(LANES)
    tri = jnp.concatenate([(kk[:, None] >= kk[None, :]).astype(BF16), jnp.ones((LANES, LANES), BF16)], axis=1)
    tri2 = jnp.concatenate([tri, tri], axis=0)

    ada = _ada(jnp.concatenate([c_prompt, c_sample], axis=0), w_ada[0], b_ada[0])
    ada_p = ada[:b].reshape(b, 1, N_ADA, D_MODEL)
    ada_s = jnp.repeat(ada[b:], dt, axis=0).reshape(1, db * dt, N_ADA, D_MODEL)
    mods_p = [ada_p[:, :, m] for m in range(N_ADA)]
    mods_s = [ada_s[:, :, m] for m in range(N_ADA)]

    q, kt_p, vt_p, hq, lf, iv, sg = _proj(x_prompt, mods_p[1], mods_p[0], n1, w_in_bf, wkv_t, qg_t, kg_cm,
                                           hg_lb_logits, bd, tm=tm_p, channel_major_kv=True)
    oa = _sb_prompt(q, kt_p, vt_p, bias2, tri2, b, t)
    s0 = jnp.zeros((b, N_HEADS, HEAD_DIM, HEAD_DIM), F32)
    ob, s_p = _hgrn(hq, lf, iv, sg, s0, og_t, bd, b, t, HG_CHUNK, tc=256, nb=1)
    x1, h2 = _out_proj(x_prompt, oa, ob, w_out_bf, mods_p[2], mods_p[4], mods_p[3], n2, tm=512)
    y_p = _mlp(h2, x1, mods_p[5], w_up_bf, w_down_bf, tm=1024, tf=512)

    ns = db * dt
    xs3 = x_sample.reshape(1, ns, D_MODEL)
    q, k_s, v_s, hq, lf, iv, sg = _proj(xs3, mods_s[1], mods_s[0], n1, w_in_bf, wkv_t, qg_t, kg_t,
                                         hg_lb_logits, bd, tm=ns, channel_major_kv=False)
    seq = lambda a: _pad_rows(a.reshape(db, dt, GROUP_W).astype(F32), SUBLANES)
    cm = lambda c: jnp.transpose(c[0], (0, 2, 3, 1)).reshape(n_phys, GROUP_W, PAGE)
    oa8 = _sb_decode(seq(q), seq(k_s), seq(v_s), cm(cache_k), cm(cache_v), page_table, bias_bc, tri2)
    ob8, s_s = _hgrn(seq(hq), seq(lf), seq(iv), seq(sg), state_hgrn[0], og_t, bd,
                     db, SUBLANES, SUBLANES, tc=SUBLANES, nb=4)
    oa = oa8[:, :dt].reshape(ns, GROUP_W)
    ob = ob8[:, :dt].reshape(ns, GROUP_W)
    x1, h2 = _out_proj(xs3, oa, ob, w_out_bf, mods_s[2], mods_s[4], mods_s[3], n2, tm=ns)
    y_s = _mlp(h2, x1, mods_s[5], w_up_bf, w_down_bf, tm=ns, tf=512)

    heads = lambda a, bb, tt: a.reshape(1, bb, tt, N_HEADS, HEAD_DIM)
    heads_cm = lambda a: jnp.transpose(a.reshape(b, N_HEADS, HEAD_DIM, t), (0, 3, 1, 2))[None]
    return (y_p, y_s.reshape(db, dt, D_MODEL),
            heads_cm(kt_p), heads_cm(vt_p), heads(k_s, db, dt), heads(v_s, db, dt),
            s_p[None], s_s[None])
```

```python
import functools
import math

import jax
import jax.numpy as jnp
from jax import lax
from jax.experimental import pallas as pl
from jax.experimental.pallas import tpu as pltpu

F32 = jnp.float32
BF16 = jnp.bfloat16

D_MODEL = 1024
N_HEADS = 8
HEAD_DIM = 64
GROUP_W = N_HEADS * HEAD_DIM
N_GROUPS = 7
D_FF = 4 * D_MODEL
N_ADA = 6
ADA_SH1, ADA_SC1, ADA_G1, ADA_SH2, ADA_SC2, ADA_G2 = range(N_ADA)
EPS = 1e-6
SB_SCALE = HEAD_DIM ** -0.5
LOG2E = math.log2(math.e)
HG_CHUNK = 64
PAGE = 128
LANES = 128
SUBLANES = 8
VMEM_LIMIT = 48 * 1024 * 1024
TQ = 1024
PAGE_GROUP = 4
HG_GROUP = 4


def _dot(a, b):
    return jnp.dot(a, b, preferred_element_type=F32)


def _dot_nt(a, b):
    return lax.dot_general(a, b, (((1,), (1,)), ((), ())), preferred_element_type=F32)


def _dot_tn(a, b):
    return lax.dot_general(a, b, (((0,), (0,)), ((), ())), preferred_element_type=F32)


def _split_bf16(x, n):
    parts = []
    r = x
    for i in range(n):
        p = r.astype(BF16)
        parts.append(p)
        if i + 1 < n:
            r = r - p.astype(F32)
    return parts


def _silu(x):
    return x * jax.nn.sigmoid(x)


def _head_norm(y, gain, bd):
    hi, lo = _split_bf16(y * y, 2)
    half = 2 * LANES
    ss = jnp.concatenate(
        [_dot(hi[:, c:c + half], bd) + _dot(lo[:, c:c + half], bd) for c in range(0, GROUP_W, half)],
        axis=1)
    return y * lax.rsqrt(ss * (1.0 / HEAD_DIM) + EPS) * gain


def _head_norm_cm(yt, gain, bd):
    hi, lo = _split_bf16(yt * yt, 2)
    half = 2 * LANES
    ss = jnp.concatenate(
        [_dot(bd, hi[c:c + half]) + _dot(bd, lo[c:c + half]) for c in range(0, GROUP_W, half)],
        axis=0)
    return yt * lax.rsqrt(ss * (1.0 / HEAD_DIM) + EPS) * gain


def _softplus2(z2):
    return jnp.maximum(z2, 0.0) + jnp.log(1.0 + jnp.exp2(-jnp.abs(z2))) * LOG2E


def _hi_lo(x):
    hi, lo = _split_bf16(x, 2)
    return jnp.concatenate([hi, lo], axis=1)


def _ada_kernel(c_ref, w_ref, b_ref, o_ref):
    s = _silu(c_ref[...]).astype(BF16)
    o_ref[...] = _dot(s, w_ref[...].astype(BF16)) + b_ref[...]


def _ada(c_all, w_ada, b_ada):
    m = c_all.shape[0]
    n = w_ada.shape[1]
    tn = 1024
    return pl.pallas_call(
        _ada_kernel,
        grid=(n // tn,),
        in_specs=[pl.BlockSpec((m, D_MODEL), lambda j: (0, 0)),
                  pl.BlockSpec((D_MODEL, tn), lambda j: (0, j)),
                  pl.BlockSpec((1, tn), lambda j: (0, j))],
        out_specs=pl.BlockSpec((m, tn), lambda j: (0, j)),
        out_shape=jax.ShapeDtypeStruct((m, n), F32),
        compiler_params=pltpu.CompilerParams(dimension_semantics=("parallel",),
                                             vmem_limit_bytes=VMEM_LIMIT),
        name="ada",
    )(c_all, w_ada, b_ada.reshape(1, n))


def _proj_kernel(x_ref, sc_ref, sh_ref, n1_ref, w_ref, wkv_t_ref, qg_ref, kg_ref, lbl_ref, bd_ref,
                 q_ref, k_ref, v_ref, hq_ref, lf_ref, i_ref, g_ref, *, channel_major_kv):
    x = x_ref[...]
    h = x * lax.rsqrt(jnp.mean(x * x, axis=-1, keepdims=True) + EPS) * n1_ref[...]
    hb = (h * (1.0 + sc_ref[...]) + sh_ref[...]).astype(BF16)
    bd = bd_ref[...]

    def group(g):
        return _dot(hb, w_ref[:, g * GROUP_W:(g + 1) * GROUP_W])

    q_ref[...] = (_head_norm(group(0), qg_ref[...], bd) * (SB_SCALE * LOG2E)).astype(BF16)
    if channel_major_kv:
        k_ref[...] = _head_norm_cm(_dot_nt(wkv_t_ref[0:GROUP_W, :], hb), kg_ref[...], bd)
        v_ref[...] = _dot_nt(wkv_t_ref[GROUP_W:2 * GROUP_W, :], hb)
    else:
        k_ref[...] = _head_norm(group(1), kg_ref[...], bd)
        v_ref[...] = group(2)
    hq_ref[...] = _silu(group(3)).astype(BF16)
    lbl = lbl_ref[...]
    e = jnp.exp(lbl - jnp.max(lbl, axis=0, keepdims=True))
    lb = e[0:1, :] / jnp.sum(e, axis=0, keepdims=True)
    f = lb + (1.0 - lb) * jax.nn.sigmoid(group(4))
    lf_ref[...] = jnp.log(f)
    i_ref[...] = group(5).astype(BF16)
    g_ref[...] = _silu(group(6)).astype(BF16)


def _mod_spec(mods, m, tm):
    if mods.shape[1] == 1:
        return pl.BlockSpec((None, 1, D_MODEL), lambda b, i, *_: (b, 0, m))
    return pl.BlockSpec((None, tm, D_MODEL), lambda b, i, *_: (b, i, m))


def _proj(x3, mods, n1, w_in_bf, wkv_t, qg_t, kg, lb_logits, bd, tm, channel_major_kv):
    bx, r, _ = x3.shape
    nt = r // tm
    const = lambda shape: pl.BlockSpec(shape, lambda b, i: (0,) * len(shape))
    row_spec = pl.BlockSpec((tm, GROUP_W), lambda b, i: (b * nt + i, 0))
    n = bx * r
    row_shape = lambda dt: jax.ShapeDtypeStruct((n, GROUP_W), dt)
    if channel_major_kv:
        kv_spec = pl.BlockSpec((None, GROUP_W, tm), lambda b, i: (b, 0, i))
        kv_shape = jax.ShapeDtypeStruct((bx, GROUP_W, r), F32)
    else:
        kv_spec, kv_shape = row_spec, row_shape(F32)
    return pl.pallas_call(
        functools.partial(_proj_kernel, channel_major_kv=channel_major_kv),
        grid=(bx, nt),
        in_specs=[pl.BlockSpec((None, tm, D_MODEL), lambda b, i: (b, i, 0)),
                  _mod_spec(mods, ADA_SC1, tm), _mod_spec(mods, ADA_SH1, tm),
                  const((1, D_MODEL)),
                  const((D_MODEL, N_GROUPS * GROUP_W)),
                  const((2 * GROUP_W, D_MODEL)),
                  const((1, GROUP_W)), const(kg.shape),
                  const(lb_logits.shape),
                  const((2 * LANES, 2 * LANES))],
        out_specs=[row_spec, kv_spec, kv_spec, row_spec, row_spec, row_spec, row_spec],
        out_shape=[row_shape(BF16), kv_shape, kv_shape, row_shape(BF16), row_shape(F32),
                   row_shape(BF16), row_shape(BF16)],
        compiler_params=pltpu.CompilerParams(dimension_semantics=("parallel", "parallel"),
                                             vmem_limit_bytes=VMEM_LIMIT),
        name="proj",
    )(x3, mods, mods, n1, w_in_bf, wkv_t, qg_t, kg, lb_logits, bd)


def _sb_prompt_kernel(bias_ref, q_ref, qe_ref, kt_ref, vt_ref, tri_ref, o_ref,
                      qx_ref, r_ref, acc_ref, z0_ref, hl0_ref, z1_ref, hl1_ref):
    p = pl.program_id(1)
    i = pl.program_id(2)
    nsub = TQ // LANES
    lane2 = lax.broadcasted_iota(jnp.int32, (1, 2 * LANES), 1)
    bias_row = jnp.where(lane2 < LANES, bias_ref[2 * p], bias_ref[2 * p + 1])
    bias_parts = _split_bf16(bias_row, 3)
    brow = lax.broadcasted_iota(jnp.int32, (LANES, 2 * LANES), 0)
    bias_blk = jnp.zeros((LANES, 2 * LANES), F32)
    for n, part in enumerate(bias_parts):
        bias_blk = jnp.where(brow == n, part.astype(F32), bias_blk)
    bias_blk = bias_blk.astype(BF16)
    zeros_half = jnp.zeros((HEAD_DIM, LANES), F32)
    qx_ref[:, :LANES] = q_ref[...]
    qx_ref[:, LANES:] = qe_ref[...]

    def stacked(ref, start):
        blk = ref[:, pl.ds(start, LANES)]
        top = jnp.concatenate([blk[:HEAD_DIM], zeros_half], axis=0)
        bot = jnp.concatenate([zeros_half, blk[HEAD_DIM:]], axis=0)
        return jnp.concatenate([top, bot], axis=1).astype(BF16)

    def strictly_before(m):
        row = lax.broadcasted_iota(jnp.int32, (m, LANES), 0)
        col = lax.broadcasted_iota(jnp.int32, (m, LANES), 1)
        return col < row

    def stage_a(rows, start, masked, z_ref, hl_ref):
        m = rows.stop - rows.start
        z = _dot(qx_ref[rows, :], jnp.concatenate([stacked(kt_ref, start), bias_blk], axis=0))
        l = _softplus2(z)
        z_ref[rows, :] = z
        for hh in range(2):
            lh = l[:, hh * LANES:(hh + 1) * LANES]
            if masked:
                lh = jnp.where(strictly_before(m), lh, 0.0)
            hl_ref[rows, 2 * hh * LANES:2 * (hh + 1) * LANES] = _hi_lo(lh)

    def stage_b(rows, start, masked, z_ref, hl_ref):
        m = rows.stop - rows.start
        tri = tri_ref[...]
        ws = []
        for hh in range(2):
            cs2 = _dot(hl_ref[rows, 2 * hh * LANES:2 * (hh + 1) * LANES], tri)
            w = jnp.exp2(z_ref[rows, hh * LANES:(hh + 1) * LANES] - cs2[:, :LANES] - r_ref[hh, rows, :])
            if masked:
                w = jnp.where(strictly_before(m), w, 0.0)
            r_ref[hh, rows, :] += cs2[:, LANES:]
            ws.append(w.astype(BF16))
        acc_ref[rows, :] += _dot_nt(jnp.concatenate(ws, axis=1), stacked(vt_ref, start))

    r_ref[...] = jnp.zeros_like(r_ref)
    acc_ref[...] = jnp.zeros_like(acc_ref)
    slots = ((z0_ref, hl0_ref), (z1_ref, hl1_ref))

    for c in reversed(range(nsub)):
        rows = slice(c * LANES, TQ)
        start = pl.multiple_of(i * TQ + c * LANES, LANES)
        stage_a(rows, start, True, *slots[c % 2])
        stage_b(rows, start, True, *slots[c % 2])

    @pl.when(i > 0)
    def _():
        n = i * nsub
        rows = slice(0, TQ)
        key = lambda k: pl.multiple_of((n - 1 - k) * LANES, LANES)
        stage_a(rows, key(0), False, *slots[0])

        def body(kk, carry):
            k = 2 * kk
            stage_a(rows, key(k + 1), False, *slots[1])
            stage_b(rows, key(k), False, *slots[0])
            stage_a(rows, key(k + 2), False, *slots[0])
            stage_b(rows, key(k + 1), False, *slots[1])
            return carry

        lax.fori_loop(0, n // 2 - 1, body, 0)
        stage_a(rows, key(n - 1), False, *slots[1])
        stage_b(rows, key(n - 2), False, *slots[0])
        stage_b(rows, key(n - 1), False, *slots[1])

    o_ref[...] = acc_ref[...].astype(BF16)


def _sb_prompt(q, kt, vt, bias2, tri2, b, t):
    nq = t // TQ
    q3 = q.reshape(b, t, GROUP_W)
    pair_cm = pl.BlockSpec((None, LANES, t), lambda bb, p, i: (bb, p, 0))
    tile = pl.BlockSpec((None, TQ, LANES), lambda bb, p, i: (bb, i, p))
    n_bias_cols = 3
    qe = jnp.broadcast_to((jnp.arange(LANES) < n_bias_cols).astype(BF16)[None, :], (TQ, LANES))
    out = pl.pallas_call(
        _sb_prompt_kernel,
        grid=(b, N_HEADS // 2, nq),
        in_specs=[pl.BlockSpec(memory_space=pltpu.SMEM), tile,
                  pl.BlockSpec((TQ, LANES), lambda bb, p, i: (0, 0)), pair_cm, pair_cm,
                  pl.BlockSpec((2 * LANES, 2 * LANES), lambda bb, p, i: (0, 0))],
        out_specs=tile,
        out_shape=jax.ShapeDtypeStruct((b, t, GROUP_W), BF16),
        scratch_shapes=[pltpu.VMEM((TQ, 2 * LANES), BF16),
                        pltpu.VMEM((2, TQ, LANES), F32), pltpu.VMEM((TQ, LANES), F32),
                        pltpu.VMEM((TQ, 2 * LANES), F32), pltpu.VMEM((TQ, 4 * LANES), BF16),
                        pltpu.VMEM((TQ, 2 * LANES), F32), pltpu.VMEM((TQ, 4 * LANES), BF16)],
        compiler_params=pltpu.CompilerParams(
            dimension_semantics=("parallel", "parallel", "arbitrary"),
            vmem_limit_bytes=VMEM_LIMIT),
        name="sb_prompt",
    )(bias2, q3, qe, kt, vt, tri2)
    return out.reshape(b * t, GROUP_W)


N_QROWS = N_HEADS * SUBLANES


def _sb_decode_kernel(pt_ref, q_ref, kn_ref, vn_ref, *rest, n_pages):
    del pt_ref
    k_refs, v_refs = rest[:n_pages], rest[n_pages:2 * n_pages]
    bias_ref, tri_ref, o_ref = rest[2 * n_pages:]
    gsz = PAGE_GROUP
    tri = tri_ref[...]
    bias = bias_ref[...]
    row = lax.broadcasted_iota(jnp.int32, (N_QROWS, GROUP_W), 0)
    lane = lax.broadcasted_iota(jnp.int32, (N_QROWS, GROUP_W), 1)
    own_head = (row // SUBLANES) == (lane // HEAD_DIM)
    qm = jnp.where(own_head, jnp.concatenate([q_ref[...]] * N_HEADS, axis=0), 0.0).astype(BF16)

    def log_terms(z, nblk, valid):
        l = _softplus2(z)
        if valid is not None:
            l = jnp.where(valid, l, 0.0)
        return jnp.concatenate([_hi_lo(l[:, j * LANES:(j + 1) * LANES]) for j in range(nblk)], axis=0)

    def weights(z, cs2, r, nblk, valid):
        ws = []
        for j in range(nblk):
            blk = cs2[j * N_QROWS:(j + 1) * N_QROWS]
            w = jnp.exp2(z[:, j * LANES:(j + 1) * LANES] - blk[:, :LANES] - r)
            if valid is not None:
                w = jnp.where(valid, w, 0.0)
            r = r + blk[:, LANES:]
            ws.append(w.astype(BF16))
        return jnp.concatenate(ws, axis=1), r

    def group_scores(g):
        kt = jnp.concatenate([k_refs[g * gsz + j][...].astype(BF16) for j in range(gsz)], axis=1)
        z = _dot(qm, kt) + jnp.concatenate([bias] * gsz, axis=1)
        return z, log_terms(z, gsz, None)

    def group_pv(g, w):
        vt = jnp.concatenate([v_refs[g * gsz + j][...].astype(BF16) for j in range(gsz)], axis=1)
        return _dot_nt(w, vt)

    pad = jnp.zeros((PAGE - SUBLANES, GROUP_W), F32)
    kb = jnp.concatenate([kn_ref[...], pad], axis=0).astype(BF16)
    vb = jnp.concatenate([vn_ref[...], pad], axis=0).astype(BF16)
    r2 = lax.broadcasted_iota(jnp.int32, (N_QROWS, PAGE), 0)
    c2 = lax.broadcasted_iota(jnp.int32, (N_QROWS, PAGE), 1)
    own_valid = c2 < (r2 % SUBLANES)
    z_own = _dot_nt(qm, kb) + bias
    hl_own = log_terms(z_own, 1, own_valid)

    n_groups = n_pages // gsz
    z, hl = group_scores(0)
    w_own, r = weights(z_own, _dot(hl_own, tri), jnp.zeros((N_QROWS, LANES), F32), 1, own_valid)
    acc = _dot(w_own, vb)
    cs2 = _dot(hl, tri)
    for g in range(n_groups):
        if g + 1 < n_groups:
            z_next, hl_next = group_scores(g + 1)
        w, r = weights(z, cs2, r, gsz, None)
        acc = acc + group_pv(g, w)
        if g + 1 < n_groups:
            z, cs2 = z_next, _dot(hl_next, tri)

    own = jnp.where(own_head, acc, 0.0)
    o = own[0:SUBLANES]
    for h in range(1, N_HEADS):
        o = o + own[h * SUBLANES:(h + 1) * SUBLANES]
    o_ref[...] = o


def _sb_decode(q8, kn8, vn8, kc_cm, vc_cm, page_table, bias_bc, tri2):
    db = q8.shape[0]
    n_pages = page_table.shape[1]
    pt_flat = page_table.reshape(-1)

    def page_spec(j):
        return pl.BlockSpec((None, GROUP_W, PAGE),
                            lambda s, pt: (pt[s * n_pages + n_pages - 1 - j], 0, 0))

    seq_spec = pl.BlockSpec((None, SUBLANES, GROUP_W), lambda s, pt: (s, 0, 0))
    grid_spec = pltpu.PrefetchScalarGridSpec(
        num_scalar_prefetch=1,
        grid=(db,),
        in_specs=[seq_spec, seq_spec, seq_spec]
                 + [page_spec(j) for j in range(n_pages)] + [page_spec(j) for j in range(n_pages)]
                 + [pl.BlockSpec((N_QROWS, LANES), lambda s, pt: (0, 0)),
                    pl.BlockSpec((2 * LANES, 2 * LANES), lambda s, pt: (0, 0))],
        out_specs=seq_spec)
    return pl.pallas_call(
        functools.partial(_sb_decode_kernel, n_pages=n_pages),
        grid_spec=grid_spec,
        out_shape=jax.ShapeDtypeStruct((db, SUBLANES, GROUP_W), F32),
        compiler_params=pltpu.CompilerParams(dimension_semantics=("parallel",),
                                             vmem_limit_bytes=VMEM_LIMIT),
        name="sb_decode",
    )(pt_flat, q8, kn8, vn8, *([kc_cm] * n_pages), *([vc_cm] * n_pages), bias_bc, tri2)


def _hgrn_kernel(hq_ref, lf_ref, iv_ref, sg_ref, s0_ref, og_ref, bd_ref, ltri_ref, ones_ref,
                 o_ref, sout_ref, s_scr, *, chunk, n_chunks, nb):
    i = pl.program_id(1)

    @pl.when(i == 0)
    def _():
        s_scr[...] = s0_ref[...]

    ltri = ltri_ref[...]
    ones_sq = ones_ref[...]
    rr = lax.broadcasted_iota(jnp.int32, (chunk, chunk), 0)
    cc = lax.broadcasted_iota(jnp.int32, (chunk, chunk), 1)
    causal = cc <= rr
    er = lax.broadcasted_iota(jnp.int32, (HEAD_DIM, HEAD_DIM), 0)
    ec = lax.broadcasted_iota(jnp.int32, (HEAD_DIM, HEAD_DIM), 1)
    eye = er == ec

    for bi in range(nb):
        states = [s_scr[bi, h] for h in range(N_HEADS)]
        for c in range(n_chunks):
            rows = slice(c * chunk, (c + 1) * chunk)
            lf = lf_ref[bi, rows, :]
            b = sum(_dot(ltri, part) for part in _split_bf16(lf, 3))
            b_last = b[chunk - 1:chunk, :]
            hk = 1.0 - jnp.exp(lf)
            qd = (hq_ref[bi, rows, :].astype(F32) * jnp.exp(b)).astype(BF16)
            kd = (hk * jnp.exp(-b)).astype(BF16)
            kk = (hk * jnp.exp(b_last - b)).astype(BF16)
            v = iv_ref[bi, rows, :].astype(BF16)
            outs = []
            for h in range(N_HEADS):
                sl = slice(h * HEAD_DIM, (h + 1) * HEAD_DIM)
                qd_h, v_h = qd[:, sl], v[:, sl]
                att = jnp.where(causal, _dot_nt(qd_h, kd[:, sl]), 0.0)
                s = states[h]
                outs.append(_dot(att.astype(BF16), v_h) + _dot(qd_h, s.astype(BF16)))
                diag = jnp.where(eye, jnp.broadcast_to(b_last[:, sl], (HEAD_DIM, HEAD_DIM)), 0.0)
                b_col = sum(_dot(part, ones_sq) for part in _split_bf16(diag, 3))
                states[h] = jnp.exp(b_col) * s + _dot_tn(kk[:, sl], v_h)
            o = jnp.concatenate(outs, axis=1)
            gated = _head_norm(o, og_ref[...], bd_ref[...]) * sg_ref[bi, rows, :].astype(F32)
            o_ref[bi, rows, :] = gated.astype(o_ref.dtype)
        for h in range(N_HEADS):
            s_scr[bi, h] = states[h]

    @pl.when(i == pl.num_programs(1) - 1)
    def _():
        sout_ref[...] = s_scr[...]


def _hgrn(hq, lf, iv, sg, s0, og_t, bd, bx, r, chunk, tc, nb):
    shp = (bx, r, GROUP_W)
    tok_spec = pl.BlockSpec((nb, tc, GROUP_W), lambda b, i: (b, i, 0))
    state_spec = pl.BlockSpec((nb, N_HEADS, HEAD_DIM, HEAD_DIM), lambda b, i: (b, 0, 0, 0))
    const = lambda shape: pl.BlockSpec(shape, lambda b, i: (0,) * len(shape))
    ltri = (jnp.arange(chunk)[None, :] <= jnp.arange(chunk)[:, None]).astype(BF16)
    ones_sq = jnp.ones((HEAD_DIM, HEAD_DIM), BF16)
    o, s_new = pl.pallas_call(
        functools.partial(_hgrn_kernel, chunk=chunk, n_chunks=tc // chunk, nb=nb),
        grid=(bx // nb, r // tc),
        in_specs=[tok_spec, tok_spec, tok_spec, tok_spec, state_spec,
                  const((1, GROUP_W)), const((2 * LANES, 2 * LANES)),
                  const((chunk, chunk)), const((HEAD_DIM, HEAD_DIM))],
        out_specs=[tok_spec, state_spec],
        out_shape=[jax.ShapeDtypeStruct(shp, hq.dtype),
                   jax.ShapeDtypeStruct((bx, N_HEADS, HEAD_DIM, HEAD_DIM), F32)],
        scratch_shapes=[pltpu.VMEM((nb, N_HEADS, HEAD_DIM, HEAD_DIM), F32)],
        compiler_params=pltpu.CompilerParams(dimension_semantics=("parallel", "arbitrary"),
                                             vmem_limit_bytes=VMEM_LIMIT),
        name="hgrn",
    )(hq.reshape(shp), lf.reshape(shp), iv.reshape(shp), sg.reshape(shp), s0, og_t, bd, ltri, ones_sq)
    return o, s_new


def _hgrn_prompt_kernel(hq_ref, lf_ref, iv_ref, sg_ref, s0_ref, og_ref, bd_ref, ltri_ref,
                        o_ref, sout_ref, s_scr, *, n_chunks):
    i = pl.program_id(1)

    @pl.when(i == 0)
    def _():
        s_scr[...] = s0_ref[...]

    c_len = HG_CHUNK
    gw = HG_GROUP * HEAD_DIM
    n_groups = N_HEADS // HG_GROUP
    ltri = ltri_ref[...]
    r4 = lax.broadcasted_iota(jnp.int32, (gw, gw), 0)
    c4 = lax.broadcasted_iota(jnp.int32, (gw, gw), 1)
    same_head = (r4 // HEAD_DIM) == (c4 // HEAD_DIM)
    rt = lax.broadcasted_iota(jnp.int32, (c_len, gw), 0)
    ct = lax.broadcasted_iota(jnp.int32, (c_len, gw), 1)
    causal = (ct % c_len) <= rt
    ones_tok = jnp.ones((c_len, gw), BF16)

    def block_diag(a):
        return jnp.where(same_head, jnp.concatenate([a] * HG_GROUP, axis=0), 0.0).astype(BF16)

    states = [s_scr[g] for g in range(n_groups)]
    for c in range(n_chunks):
        rows = slice(c * c_len, (c + 1) * c_len)
        lf = lf_ref[rows, :]
        parts = _split_bf16(lf, 3)
        b = sum(_dot(ltri, part) for part in parts)
        b_col = sum(_dot_tn(part, ones_tok) for part in parts)
        b_last = b[c_len - 1:c_len, :]
        hk = 1.0 - jnp.exp(lf)
        qd = (hq_ref[rows, :].astype(F32) * jnp.exp(b)).astype(BF16)
        kd = hk * jnp.exp(-b)
        kk = (hk * jnp.exp(b_last - b)).astype(BF16)
        v = iv_ref[rows, :]
        outs = []
        for g in range(n_groups):
            sl = slice(g * gw, (g + 1) * gw)
            s = states[g]
            vg = v[:, sl]
            att = jnp.where(causal, _dot_nt(qd[:, sl], block_diag(kd[:, sl])), 0.0).astype(BF16)
            outs.append(_dot(jnp.concatenate([att, qd[:, sl]], axis=1),
                             jnp.concatenate([block_diag(vg.astype(F32)), s.astype(BF16)], axis=0)))
            ds = jnp.where(same_head, _dot_tn(kk[:, sl], vg), 0.0)
            states[g] = jnp.exp(b_col[sl, :]) * s + ds
        o = jnp.concatenate(outs, axis=1)
        gated = _head_norm(o, og_ref[...], bd_ref[...]) * sg_ref[rows, :].astype(F32)
        o_ref[rows, :] = gated.astype(o_ref.dtype)
    for g in range(n_groups):
        s_scr[g] = states[g]

    @pl.when(i == pl.num_programs(1) - 1)
    def _():
        sout_ref[...] = s_scr[...]


def _hgrn_prompt(hq, lf, iv, sg, s0_bd, og_t, bd, bx, r, tc):
    shp = (bx, r, GROUP_W)
    gw = HG_GROUP * HEAD_DIM
    tok_spec = pl.BlockSpec((None, tc, GROUP_W), lambda b, i: (b, i, 0))
    state_spec = pl.BlockSpec((None, N_HEADS // HG_GROUP, gw, gw), lambda b, i: (b, 0, 0, 0))
    const = lambda shape: pl.BlockSpec(shape, lambda b, i: (0,) * len(shape))
    ltri = (jnp.arange(HG_CHUNK)[None, :] <= jnp.arange(HG_CHUNK)[:, None]).astype(BF16)
    return pl.pallas_call(
        functools.partial(_hgrn_prompt_kernel, n_chunks=tc // HG_CHUNK),
        grid=(bx, r // tc),
        in_specs=[tok_spec, tok_spec, tok_spec, tok_spec, state_spec,
                  const((1, GROUP_W)), const((2 * LANES, 2 * LANES)), const((HG_CHUNK, HG_CHUNK))],
        out_specs=[tok_spec, state_spec],
        out_shape=[jax.ShapeDtypeStruct(shp, BF16), jax.ShapeDtypeStruct(s0_bd.shape, F32)],
        scratch_shapes=[pltpu.VMEM(s0_bd.shape[1:], F32)],
        compiler_params=pltpu.CompilerParams(dimension_semantics=("parallel", "arbitrary"),
                                             vmem_limit_bytes=VMEM_LIMIT),
        name="hgrn_prompt",
    )(hq.reshape(shp), lf.reshape(shp), iv.reshape(shp), sg.reshape(shp), s0_bd, og_t, bd, ltri)


def _out_kernel(x_ref, oa_ref, ob_ref, w_ref, g1_ref, sc_ref, sh_ref, n2_ref, x1_ref, h2_ref):
    mix = (_dot(oa_ref[...].astype(BF16), w_ref[0:GROUP_W, :])
           + _dot(ob_ref[...].astype(BF16), w_ref[GROUP_W:2 * GROUP_W, :]))
    x1 = x_ref[...] + g1_ref[...] * mix
    x1_ref[...] = x1
    h2 = x1 * lax.rsqrt(jnp.mean(x1 * x1, axis=-1, keepdims=True) + EPS) * n2_ref[...]
    h2_ref[...] = (h2 * (1.0 + sc_ref[...]) + sh_ref[...]).astype(BF16)


def _out_proj(x3, oa, ob, w_out_bf, mods, n2, tm):
    bx, r, _ = x3.shape
    nt = r // tm
    row_spec = pl.BlockSpec((None, tm, D_MODEL), lambda b, i: (b, i, 0))
    half_spec = pl.BlockSpec((None, tm, GROUP_W), lambda b, i: (b, i, 0))
    const = lambda shape: pl.BlockSpec(shape, lambda b, i: (0,) * len(shape))
    return pl.pallas_call(
        _out_kernel,
        grid=(bx, nt),
        in_specs=[row_spec, half_spec, half_spec, const((2 * GROUP_W, D_MODEL)),
                  _mod_spec(mods, ADA_G1, tm), _mod_spec(mods, ADA_SC2, tm), _mod_spec(mods, ADA_SH2, tm),
                  const((1, D_MODEL))],
        out_specs=[row_spec, row_spec],
        out_shape=[jax.ShapeDtypeStruct(x3.shape, F32), jax.ShapeDtypeStruct(x3.shape, BF16)],
        compiler_params=pltpu.CompilerParams(dimension_semantics=("parallel", "parallel"),
                                             vmem_limit_bytes=VMEM_LIMIT),
        name="out_proj",
    )(x3, oa.reshape(bx, r, GROUP_W), ob.reshape(bx, r, GROUP_W), w_out_bf, mods, mods, mods, n2)


def _mlp_kernel(h_ref, x1_ref, g2_ref, wu_ref, wd_ref, o_ref, acc_ref):
    f = pl.program_id(2)
    u = _dot(h_ref[...], wu_ref[...])
    a = jnp.square(jnp.maximum(u, 0.0)).astype(BF16)
    part = _dot(a, wd_ref[...])

    @pl.when(f == 0)
    def _():
        acc_ref[...] = part

    @pl.when(f > 0)
    def _():
        acc_ref[...] += part

    @pl.when(f == pl.num_programs(2) - 1)
    def _():
        o_ref[...] = x1_ref[...] + g2_ref[...] * acc_ref[...]


def _mlp(h2, x1, mods, w_up_bf, w_down_bf, tm, tf):
    bx, r, _ = x1.shape
    nt = r // tm
    row_spec = pl.BlockSpec((None, tm, D_MODEL), lambda b, i, f: (b, i, 0))
    return pl.pallas_call(
        _mlp_kernel,
        grid=(bx, nt, D_FF // tf),
        in_specs=[row_spec, row_spec, _mod_spec(mods, ADA_G2, tm),
                  pl.BlockSpec((D_MODEL, tf), lambda b, i, f: (0, f)),
                  pl.BlockSpec((tf, D_MODEL), lambda b, i, f: (f, 0))],
        out_specs=row_spec,
        out_shape=jax.ShapeDtypeStruct(x1.shape, F32),
        scratch_shapes=[pltpu.VMEM((tm, D_MODEL), F32)],
        compiler_params=pltpu.CompilerParams(
            dimension_semantics=("parallel", "parallel", "arbitrary"),
            vmem_limit_bytes=VMEM_LIMIT),
        name="mlp",
    )(h2, x1, mods, w_up_bf, w_down_bf)


def _pad_rows(a, rows):
    return jnp.pad(a, ((0, 0), (0, rows - a.shape[1]), (0, 0)))


def kernel(x_prompt, x_sample, cache_k, cache_v, state_hgrn, page_table, c_prompt, c_sample,
           w_ada, b_ada, norm1_g, norm2_g, w_in, q_norm_g, k_norm_g, sb_bias, hg_lb_logits, hg_out_g,
           w_out, w_up, w_down):
    assert w_ada.shape[0] == 1 and hg_lb_logits.shape[0] == 2, "single-layer step"
    b, t, _ = x_prompt.shape
    db, dt, _ = x_sample.shape
    n_phys = cache_k.shape[1]
    tm_p = 512

    w_in_bf = w_in[0].astype(BF16)
    wkv_t = w_in[0][:, GROUP_W:3 * GROUP_W].T.astype(BF16)
    w_out_bf = w_out[0].astype(BF16)
    w_up_bf = w_up[0].astype(BF16)
    w_down_bf = w_down[0].astype(BF16)
    n1 = norm1_g[0].reshape(1, D_MODEL)
    n2 = norm2_g[0].reshape(1, D_MODEL)
    qg_t = jnp.tile(q_norm_g[0], N_HEADS).reshape(1, GROUP_W)
    kg_t = jnp.tile(k_norm_g[0], N_HEADS).reshape(1, GROUP_W)
    kg_cm = jnp.broadcast_to(kg_t.reshape(GROUP_W, 1), (GROUP_W, tm_p))
    og_t = jnp.tile(hg_out_g[0], N_HEADS).reshape(1, GROUP_W)
    bias2 = sb_bias[0].astype(F32) * LOG2E
    bias_bc = jnp.broadcast_to(jnp.repeat(bias2, SUBLANES)[:, None], (N_QROWS, LANES))
    idx = jnp.arange(2 * LANES)
    bd = (idx[:, None] // HEAD_DIM == idx[None, :] // HEAD_DIM).astype(BF16)
    kk = jnp.arange(LANES)
    tri = jnp.concatenate([(kk[:, None] >= kk[None, :]).astype(BF16), jnp.ones((LANES, LANES), BF16)], axis=1)
    tri2 = jnp.concatenate([tri, tri], axis=0)

    ada = _ada(jnp.concatenate([c_prompt, c_sample], axis=0), w_ada[0], b_ada[0])
    mods_p = ada[:b].reshape(b, 1, N_ADA * D_MODEL)
    mods_s = jnp.repeat(ada[b:], dt, axis=0).reshape(1, db * dt, N_ADA * D_MODEL)

    q, kt_p, vt_p, hq, lf, iv, sg = _proj(x_prompt, mods_p, n1, w_in_bf, wkv_t, qg_t, kg_cm,
                                           hg_lb_logits, bd, tm=tm_p, channel_major_kv=True)
    oa = _sb_prompt(q, kt_p, vt_p, bias2, tri2, b, t)
    n_hg = N_HEADS // HG_GROUP
    gw = HG_GROUP * HEAD_DIM
    ob, s_bd = _hgrn_prompt(hq, lf, iv, sg, jnp.zeros((b, n_hg, gw, gw), F32), og_t, bd, b, t, tc=256)
    s_bd = s_bd.reshape(b, n_hg, HG_GROUP, HEAD_DIM, HG_GROUP, HEAD_DIM)
    s_p = jnp.stack([s_bd[:, :, j, :, j, :] for j in range(HG_GROUP)], axis=2)
    s_p = s_p.reshape(b, N_HEADS, HEAD_DIM, HEAD_DIM)
    x1, h2 = _out_proj(x_prompt, oa, ob, w_out_bf, mods_p, n2, tm=512)
    y_p = _mlp(h2, x1, mods_p, w_up_bf, w_down_bf, tm=1024, tf=512)

    ns = db * dt
    xs3 = x_sample.reshape(1, ns, D_MODEL)
    q, k_s, v_s, hq, lf, iv, sg = _proj(xs3, mods_s, n1, w_in_bf, wkv_t, qg_t, kg_t,
                                         hg_lb_logits, bd, tm=ns, channel_major_kv=False)
    seq = lambda a: _pad_rows(a.reshape(db, dt, GROUP_W).astype(F32), SUBLANES)
    cm = lambda c: jnp.transpose(c[0], (0, 2, 3, 1)).reshape(n_phys, GROUP_W, PAGE)
    oa8 = _sb_decode(seq(q), seq(k_s), seq(v_s), cm(cache_k), cm(cache_v), page_table, bias_bc, tri2)
    ob8, s_s = _hgrn(seq(hq), seq(lf), seq(iv), seq(sg), state_hgrn[0], og_t, bd,
                     db, SUBLANES, SUBLANES, tc=SUBLANES, nb=4)
    oa = oa8[:, :dt].reshape(ns, GROUP_W)
    ob = ob8[:, :dt].reshape(ns, GROUP_W)
    x1, h2 = _out_proj(xs3, oa, ob, w_out_bf, mods_s, n2, tm=ns)
    y_s = _mlp(h2, x1, mods_s, w_up_bf, w_down_bf, tm=ns, tf=512)

    heads = lambda a, bb, tt: a.reshape(1, bb, tt, N_HEADS, HEAD_DIM)
    heads_cm = lambda a: jnp.transpose(a.reshape(b, N_HEADS, HEAD_DIM, t), (0, 3, 1, 2))[None]
    return (y_p, y_s.reshape(db, dt, D_MODEL),
            heads_cm(kt_p), heads_cm(vt_p), heads(k_s, db, dt), heads(v_s, db, dt),
            s_p[None], s_s[None])
```

```python
import functools
import math

import jax
import jax.numpy as jnp
from jax import lax
from jax.experimental import pallas as pl
from jax.experimental.pallas import tpu as pltpu

F32 = jnp.float32
BF16 = jnp.bfloat16

D_MODEL = 1024
N_HEADS = 8
HEAD_DIM = 64
GROUP_W = N_HEADS * HEAD_DIM
N_GROUPS = 7
D_FF = 4 * D_MODEL
N_ADA = 6
ADA_SH1, ADA_SC1, ADA_G1, ADA_SH2, ADA_SC2, ADA_G2 = range(N_ADA)
EPS = 1e-6
SB_SCALE = HEAD_DIM ** -0.5
LOG2E = math.log2(math.e)
HG_CHUNK = 64
PAGE = 128
LANES = 128
SUBLANES = 8
VMEM_LIMIT = 48 * 1024 * 1024
TQ = 1024
PAGE_GROUP = 4
HG_GROUP = 4
MLP_TF = 512


def _dot(a, b):
    return jnp.dot(a, b, preferred_element_type=F32)


def _dot_nt(a, b):
    return lax.dot_general(a, b, (((1,), (1,)), ((), ())), preferred_element_type=F32)


def _dot_tn(a, b):
    return lax.dot_general(a, b, (((0,), (0,)), ((), ())), preferred_element_type=F32)


def _split_bf16(x, n):
    parts = []
    r = x
    for i in range(n):
        p = r.astype(BF16)
        parts.append(p)
        if i + 1 < n:
            r = r - p.astype(F32)
    return parts


def _silu(x):
    return x * jax.nn.sigmoid(x)


def _head_norm(y, gain, bd):
    hi, lo = _split_bf16(y * y, 2)
    half = 2 * LANES
    ss = jnp.concatenate(
        [_dot(hi[:, c:c + half], bd) + _dot(lo[:, c:c + half], bd) for c in range(0, GROUP_W, half)],
        axis=1)
    return y * lax.rsqrt(ss * (1.0 / HEAD_DIM) + EPS) * gain


def _head_norm_cm(yt, gain, bd):
    hi, lo = _split_bf16(yt * yt, 2)
    half = 2 * LANES
    ss = jnp.concatenate(
        [_dot(bd, hi[c:c + half]) + _dot(bd, lo[c:c + half]) for c in range(0, GROUP_W, half)],
        axis=0)
    return yt * lax.rsqrt(ss * (1.0 / HEAD_DIM) + EPS) * gain


def _softplus2(z2):
    return jnp.maximum(z2, 0.0) + jnp.log(1.0 + jnp.exp2(-jnp.abs(z2))) * LOG2E


def _hi_lo(x):
    hi, lo = _split_bf16(x, 2)
    return jnp.concatenate([hi, lo], axis=1)


def _ada_kernel(c_ref, w_ref, b_ref, o_ref):
    s = _silu(c_ref[...]).astype(BF16)
    o_ref[...] = _dot(s, w_ref[...].astype(BF16)) + b_ref[...]


def _ada(c_all, w_ada, b_ada):
    m = c_all.shape[0]
    n = w_ada.shape[1]
    tn = 1024
    return pl.pallas_call(
        _ada_kernel,
        grid=(n // tn,),
        in_specs=[pl.BlockSpec((m, D_MODEL), lambda j: (0, 0)),
                  pl.BlockSpec((D_MODEL, tn), lambda j: (0, j)),
                  pl.BlockSpec((1, tn), lambda j: (0, j))],
        out_specs=pl.BlockSpec((m, tn), lambda j: (0, j)),
        out_shape=jax.ShapeDtypeStruct((m, n), F32),
        compiler_params=pltpu.CompilerParams(dimension_semantics=("parallel",),
                                             vmem_limit_bytes=VMEM_LIMIT),
        name="ada",
    )(c_all, w_ada, b_ada.reshape(1, n))


def _norm1(x_ref, sc_ref, sh_ref, n1_ref):
    x = x_ref[...]
    h = x * lax.rsqrt(jnp.mean(x * x, axis=-1, keepdims=True) + EPS) * n1_ref[...]
    return (h * (1.0 + sc_ref[...]) + sh_ref[...]).astype(BF16)


def _forget_gate_log(lb, y):
    return jnp.log(lb + (1.0 - lb) * jax.nn.sigmoid(y))


def _proj_kernel(x_ref, sc_ref, sh_ref, n1_ref, w_ref, wt_ref, qg_ref, kgc_ref, lbl_ref, bd_ref,
                 q_ref, kt_ref, vt_ref, hq_ref, lf_ref, i_ref, g_ref):
    hb = _norm1(x_ref, sc_ref, sh_ref, n1_ref)
    bd = bd_ref[...]

    def group(g):
        return _dot(hb, w_ref[:, g * GROUP_W:(g + 1) * GROUP_W])

    def group_cm(g):
        return _dot_nt(wt_ref[g * GROUP_W:(g + 1) * GROUP_W, :], hb)

    q_ref[...] = (_head_norm(group(0), qg_ref[...], bd) * (SB_SCALE * LOG2E)).astype(BF16)
    kt_ref[...] = _head_norm_cm(group_cm(1), kgc_ref[...], bd)
    vt_ref[...] = group_cm(2)
    hq_ref[...] = _silu(group(3)).astype(BF16)
    lbl = lbl_ref[...]
    e = jnp.exp(lbl - jnp.max(lbl, axis=0, keepdims=True))
    lb = e[0:1, :] / jnp.sum(e, axis=0, keepdims=True)
    lf_ref[...] = _forget_gate_log(lb, group(4))
    i_ref[...] = group(5).astype(BF16)
    g_ref[...] = _silu(group(6)).astype(BF16)


def _proj_decode_kernel(x_ref, sc_ref, sh_ref, n1_ref, w_ref, wt_ref, qg_ref, kg_ref, kgc_ref, lblt_ref, bd_ref,
                        q_ref, k_ref, v_ref, kt_ref, vt_ref, hq_ref, lf_ref, i_ref, g_ref):
    hb = _norm1(x_ref, sc_ref, sh_ref, n1_ref)
    bd = bd_ref[...]

    def group(g):
        return _dot(hb, w_ref[:, g * GROUP_W:(g + 1) * GROUP_W])

    def group_cm(g):
        return _dot_nt(wt_ref[g * GROUP_W:(g + 1) * GROUP_W, :], hb)

    q_ref[...] = _head_norm(group(0), qg_ref[...], bd) * (SB_SCALE * LOG2E)
    k_ref[...] = _head_norm(group(1), kg_ref[...], bd)
    v_ref[...] = group(2)
    kt = _head_norm_cm(group_cm(1), kgc_ref[...], bd)
    vt = group_cm(2)
    for tok in range(kt_ref.shape[0]):
        kt_ref[tok] = kt[:, tok * LANES:(tok + 1) * LANES]
        vt_ref[tok] = vt[:, tok * LANES:(tok + 1) * LANES]
    hq_ref[...] = _silu(group_cm(3))
    lblt = lblt_ref[...]
    l0, l1 = lblt[:, 0:1], lblt[:, 1:2]
    m = jnp.maximum(l0, l1)
    e0, e1 = jnp.exp(l0 - m), jnp.exp(l1 - m)
    lf_ref[...] = _forget_gate_log(e0 / (e0 + e1), group_cm(4))
    i_ref[...] = group_cm(5)
    g_ref[...] = _silu(group_cm(6))


def _mod_spec(mods, m, tm):
    if mods.shape[1] == 1:
        return pl.BlockSpec((None, 1, D_MODEL), lambda b, i, *_: (b, 0, m))
    return pl.BlockSpec((None, tm, D_MODEL), lambda b, i, *_: (b, i, m))


def _resident(shape):
    return pl.BlockSpec(shape, lambda *_: (0,) * len(shape), pipeline_mode=pl.Buffered(1))


def _proj(x3, mods, n1, w_in_bf, w_in_t, qg_t, kg_cm, lb_logits, bd, tm):
    bx, r, _ = x3.shape
    nt = r // tm
    row_spec = pl.BlockSpec((tm, GROUP_W), lambda b, i: (b * nt + i, 0))
    row_shape = lambda dt: jax.ShapeDtypeStruct((bx * r, GROUP_W), dt)
    cm_spec = pl.BlockSpec((None, GROUP_W, tm), lambda b, i: (b, 0, i))
    cm_shape = jax.ShapeDtypeStruct((bx, GROUP_W, r), F32)
    return pl.pallas_call(
        _proj_kernel,
        grid=(bx, nt),
        in_specs=[pl.BlockSpec((None, tm, D_MODEL), lambda b, i: (b, i, 0)),
                  _mod_spec(mods, ADA_SC1, tm), _mod_spec(mods, ADA_SH1, tm),
                  _resident((1, D_MODEL)), _resident(w_in_bf.shape), _resident(w_in_t.shape),
                  _resident((1, GROUP_W)), _resident(kg_cm.shape), _resident(lb_logits.shape),
                  _resident((2 * LANES, 2 * LANES))],
        out_specs=[row_spec, cm_spec, cm_spec, row_spec, row_spec, row_spec, row_spec],
        out_shape=[row_shape(BF16), cm_shape, cm_shape, row_shape(BF16), row_shape(F32),
                   row_shape(BF16), row_shape(BF16)],
        compiler_params=pltpu.CompilerParams(dimension_semantics=("parallel", "parallel"),
                                             vmem_limit_bytes=VMEM_LIMIT),
        name="proj",
    )(x3, mods, mods, n1, w_in_bf, w_in_t, qg_t, kg_cm, lb_logits, bd)


def _proj_decode(x3, mods, n1, w_in_bf, w_in_t, qg_t, kg_t, kg_cm, lb_logits_t, bd, n_tok):
    _, r, _ = x3.shape
    nseq = r // n_tok
    row_spec = pl.BlockSpec((r, GROUP_W), lambda b, i: (0, 0))
    row_shape = jax.ShapeDtypeStruct((r, GROUP_W), F32)
    tok_spec = pl.BlockSpec((n_tok, GROUP_W, nseq), lambda b, i: (0, 0, 0))
    tok_shape = jax.ShapeDtypeStruct((n_tok, GROUP_W, nseq), F32)
    cm_spec = pl.BlockSpec((GROUP_W, r), lambda b, i: (0, 0))
    cm_shape = jax.ShapeDtypeStruct((GROUP_W, r), F32)
    return pl.pallas_call(
        _proj_decode_kernel,
        grid=(1, 1),
        in_specs=[pl.BlockSpec((None, r, D_MODEL), lambda b, i: (b, i, 0)),
                  _mod_spec(mods, ADA_SC1, r), _mod_spec(mods, ADA_SH1, r),
                  _resident((1, D_MODEL)), _resident(w_in_bf.shape), _resident(w_in_t.shape),
                  _resident((1, GROUP_W)), _resident((1, GROUP_W)), _resident(kg_cm.shape),
                  _resident(lb_logits_t.shape), _resident((2 * LANES, 2 * LANES))],
        out_specs=[row_spec, row_spec, row_spec, tok_spec, tok_spec, cm_spec, cm_spec, cm_spec, cm_spec],
        out_shape=[row_shape, row_shape, row_shape, tok_shape, tok_shape,
                   cm_shape, cm_shape, cm_shape, cm_shape],
        compiler_params=pltpu.CompilerParams(dimension_semantics=("parallel", "parallel"),
                                             vmem_limit_bytes=VMEM_LIMIT),
        name="proj_decode",
    )(x3, mods, mods, n1, w_in_bf, w_in_t, qg_t, kg_t, kg_cm, lb_logits_t, bd)


def _sb_prompt_kernel(bias_ref, q_ref, qe_ref, kt_ref, vt_ref, tri_ref, o_ref,
                      qx_ref, r_ref, acc_ref, z0_ref, hl0_ref, z1_ref, hl1_ref):
    p = pl.program_id(1)
    i = pl.program_id(2)
    nsub = TQ // LANES
    lane2 = lax.broadcasted_iota(jnp.int32, (1, 2 * LANES), 1)
    bias_row = jnp.where(lane2 < LANES, bias_ref[2 * p], bias_ref[2 * p + 1])
    bias_parts = _split_bf16(bias_row, 3)
    brow = lax.broadcasted_iota(jnp.int32, (LANES, 2 * LANES), 0)
    bias_blk = jnp.zeros((LANES, 2 * LANES), F32)
    for n, part in enumerate(bias_parts):
        bias_blk = jnp.where(brow == n, part.astype(F32), bias_blk)
    bias_blk = bias_blk.astype(BF16)
    zeros_half = jnp.zeros((HEAD_DIM, LANES), F32)
    qx_ref[:, :LANES] = q_ref[...]
    qx_ref[:, LANES:] = qe_ref[...]

    def stacked(ref, start):
        blk = ref[:, pl.ds(start, LANES)]
        top = jnp.concatenate([blk[:HEAD_DIM], zeros_half], axis=0)
        bot = jnp.concatenate([zeros_half, blk[HEAD_DIM:]], axis=0)
        return jnp.concatenate([top, bot], axis=1).astype(BF16)

    def strictly_before(m):
        row = lax.broadcasted_iota(jnp.int32, (m, LANES), 0)
        col = lax.broadcasted_iota(jnp.int32, (m, LANES), 1)
        return col < row

    def stage_a(rows, start, masked, z_ref, hl_ref):
        m = rows.stop - rows.start
        z = _dot(qx_ref[rows, :], jnp.concatenate([stacked(kt_ref, start), bias_blk], axis=0))
        l = _softplus2(z)
        z_ref[rows, :] = z
        for hh in range(2):
            lh = l[:, hh * LANES:(hh + 1) * LANES]
            if masked:
                lh = jnp.where(strictly_before(m), lh, 0.0)
            hl_ref[rows, 2 * hh * LANES:2 * (hh + 1) * LANES] = _hi_lo(lh)

    def stage_b(rows, start, masked, z_ref, hl_ref):
        m = rows.stop - rows.start
        tri = tri_ref[...]
        ws = []
        for hh in range(2):
            cs2 = _dot(hl_ref[rows, 2 * hh * LANES:2 * (hh + 1) * LANES], tri)
            w = jnp.exp2(z_ref[rows, hh * LANES:(hh + 1) * LANES] - cs2[:, :LANES] - r_ref[hh, rows, :])
            if masked:
                w = jnp.where(strictly_before(m), w, 0.0)
            r_ref[hh, rows, :] += cs2[:, LANES:]
            ws.append(w.astype(BF16))
        acc_ref[rows, :] += _dot_nt(jnp.concatenate(ws, axis=1), stacked(vt_ref, start))

    r_ref[...] = jnp.zeros_like(r_ref)
    acc_ref[...] = jnp.zeros_like(acc_ref)
    slots = ((z0_ref, hl0_ref), (z1_ref, hl1_ref))

    for c in reversed(range(nsub)):
        rows = slice(c * LANES, TQ)
        start = pl.multiple_of(i * TQ + c * LANES, LANES)
        stage_a(rows, start, True, *slots[c % 2])
        stage_b(rows, start, True, *slots[c % 2])

    @pl.when(i > 0)
    def _():
        n = i * nsub
        rows = slice(0, TQ)
        key = lambda k: pl.multiple_of((n - 1 - k) * LANES, LANES)
        stage_a(rows, key(0), False, *slots[0])

        def body(kk, carry):
            k = 2 * kk
            stage_a(rows, key(k + 1), False, *slots[1])
            stage_b(rows, key(k), False, *slots[0])
            stage_a(rows, key(k + 2), False, *slots[0])
            stage_b(rows, key(k + 1), False, *slots[1])
            return carry

        lax.fori_loop(0, n // 2 - 1, body, 0)
        stage_a(rows, key(n - 1), False, *slots[1])
        stage_b(rows, key(n - 2), False, *slots[0])
        stage_b(rows, key(n - 1), False, *slots[1])

    o_ref[...] = acc_ref[...].astype(BF16)


def _sb_prompt(q, kt, vt, bias2, tri2, b, t):
    nq = t // TQ
    q3 = q.reshape(b, t, GROUP_W)
    pair_cm = pl.BlockSpec((None, LANES, t), lambda bb, p, i: (bb, p, 0))
    tile = pl.BlockSpec((None, TQ, LANES), lambda bb, p, i: (bb, i, p))
    n_bias_cols = 3
    qe = jnp.broadcast_to((jnp.arange(LANES) < n_bias_cols).astype(BF16)[None, :], (TQ, LANES))
    out = pl.pallas_call(
        _sb_prompt_kernel,
        grid=(b, N_HEADS // 2, nq),
        in_specs=[pl.BlockSpec(memory_space=pltpu.SMEM), tile,
                  pl.BlockSpec((TQ, LANES), lambda bb, p, i: (0, 0)), pair_cm, pair_cm,
                  pl.BlockSpec((2 * LANES, 2 * LANES), lambda bb, p, i: (0, 0))],
        out_specs=tile,
        out_shape=jax.ShapeDtypeStruct((b, t, GROUP_W), BF16),
        scratch_shapes=[pltpu.VMEM((TQ, 2 * LANES), BF16),
                        pltpu.VMEM((2, TQ, LANES), F32), pltpu.VMEM((TQ, LANES), F32),
                        pltpu.VMEM((TQ, 2 * LANES), F32), pltpu.VMEM((TQ, 4 * LANES), BF16),
                        pltpu.VMEM((TQ, 2 * LANES), F32), pltpu.VMEM((TQ, 4 * LANES), BF16)],
        compiler_params=pltpu.CompilerParams(
            dimension_semantics=("parallel", "parallel", "arbitrary"),
            vmem_limit_bytes=VMEM_LIMIT),
        name="sb_prompt",
    )(bias2, q3, qe, kt, vt, tri2)
    return out.reshape(b * t, GROUP_W)


N_QROWS = N_HEADS * SUBLANES


def _sb_decode_kernel(pt_ref, q_ref, kn_ref, vn_ref, *rest, n_pages):
    del pt_ref
    k_refs, v_refs = rest[:n_pages], rest[n_pages:2 * n_pages]
    bias_ref, tri_ref, o_ref = rest[2 * n_pages:]
    gsz = PAGE_GROUP
    tri = tri_ref[...]
    bias = bias_ref[...]
    row = lax.broadcasted_iota(jnp.int32, (N_QROWS, GROUP_W), 0)
    lane = lax.broadcasted_iota(jnp.int32, (N_QROWS, GROUP_W), 1)
    own_head = (row // SUBLANES) == (lane // HEAD_DIM)
    qm = jnp.where(own_head, jnp.concatenate([q_ref[...]] * N_HEADS, axis=0), 0.0).astype(BF16)

    def log_terms(z, nblk, valid):
        l = _softplus2(z)
        if valid is not None:
            l = jnp.where(valid, l, 0.0)
        return jnp.concatenate([_hi_lo(l[:, j * LANES:(j + 1) * LANES]) for j in range(nblk)], axis=0)

    def weights(z, cs2, r, nblk, valid):
        ws = []
        for j in range(nblk):
            blk = cs2[j * N_QROWS:(j + 1) * N_QROWS]
            w = jnp.exp2(z[:, j * LANES:(j + 1) * LANES] - blk[:, :LANES] - r)
            if valid is not None:
                w = jnp.where(valid, w, 0.0)
            r = r + blk[:, LANES:]
            ws.append(w.astype(BF16))
        return jnp.concatenate(ws, axis=1), r

    def group_scores(g):
        kt = jnp.concatenate([k_refs[g * gsz + j][...].astype(BF16) for j in range(gsz)], axis=1)
        z = _dot(qm, kt) + jnp.concatenate([bias] * gsz, axis=1)
        return z, log_terms(z, gsz, None)

    def group_pv(g, w):
        vt = jnp.concatenate([v_refs[g * gsz + j][...].astype(BF16) for j in range(gsz)], axis=1)
        return _dot_nt(w, vt)

    pad = jnp.zeros((PAGE - SUBLANES, GROUP_W), F32)
    kb = jnp.concatenate([kn_ref[...], pad], axis=0).astype(BF16)
    vb = jnp.concatenate([vn_ref[...], pad], axis=0).astype(BF16)
    r2 = lax.broadcasted_iota(jnp.int32, (N_QROWS, PAGE), 0)
    c2 = lax.broadcasted_iota(jnp.int32, (N_QROWS, PAGE), 1)
    own_valid = c2 < (r2 % SUBLANES)
    z_own = _dot_nt(qm, kb) + bias
    hl_own = log_terms(z_own, 1, own_valid)

    n_groups = n_pages // gsz
    z, hl = group_scores(0)
    w_own, r = weights(z_own, _dot(hl_own, tri), jnp.zeros((N_QROWS, LANES), F32), 1, own_valid)
    acc = _dot(w_own, vb)
    cs2 = _dot(hl, tri)
    for g in range(n_groups):
        if g + 1 < n_groups:
            z_next, hl_next = group_scores(g + 1)
        w, r = weights(z, cs2, r, gsz, None)
        acc = acc + group_pv(g, w)
        if g + 1 < n_groups:
            z, cs2 = z_next, _dot(hl_next, tri)

    own = jnp.where(own_head, acc, 0.0)
    o = own[0:SUBLANES]
    for h in range(1, N_HEADS):
        o = o + own[h * SUBLANES:(h + 1) * SUBLANES]
    o_ref[...] = o


def _sb_decode(q8, kn8, vn8, kc_cm, vc_cm, page_table, bias_bc, tri2):
    db = q8.shape[0]
    n_pages = page_table.shape[1]
    pt_flat = page_table.reshape(-1)

    def page_spec(j):
        return pl.BlockSpec((None, GROUP_W, PAGE),
                            lambda s, pt: (pt[s * n_pages + n_pages - 1 - j], 0, 0))

    seq_spec = pl.BlockSpec((None, SUBLANES, GROUP_W), lambda s, pt: (s, 0, 0))
    grid_spec = pltpu.PrefetchScalarGridSpec(
        num_scalar_prefetch=1,
        grid=(db,),
        in_specs=[seq_spec, seq_spec, seq_spec]
                 + [page_spec(j) for j in range(n_pages)] + [page_spec(j) for j in range(n_pages)]
                 + [pl.BlockSpec((N_QROWS, LANES), lambda s, pt: (0, 0)),
                    pl.BlockSpec((2 * LANES, 2 * LANES), lambda s, pt: (0, 0))],
        out_specs=seq_spec)
    return pl.pallas_call(
        functools.partial(_sb_decode_kernel, n_pages=n_pages),
        grid_spec=grid_spec,
        out_shape=jax.ShapeDtypeStruct((db, SUBLANES, GROUP_W), F32),
        compiler_params=pltpu.CompilerParams(dimension_semantics=("parallel",),
                                             vmem_limit_bytes=VMEM_LIMIT),
        name="sb_decode",
    )(pt_flat, q8, kn8, vn8, *([kc_cm] * n_pages), *([vc_cm] * n_pages), bias_bc, tri2)


def _hgrn_step_kernel(q_ref, lf_ref, v_ref, sg_ref, s_ref, og_ref, o_ref, sout_ref, f_scr, k_scr, *, n_tok):
    f = jnp.exp(lf_ref[...])
    f_scr[...] = f
    k_scr[...] = 1.0 - f
    tile = (HEAD_DIM, LANES)

    def key_rows(g, outs):
        base = pl.multiple_of(g * SUBLANES, SUBLANES)
        outs = list(outs)
        gates = [[ref[pl.ds(base, SUBLANES), tok * LANES:(tok + 1) * LANES] for ref in (f_scr, k_scr, q_ref)]
                 for tok in range(n_tok)]
        for j in range(SUBLANES):
            s = s_ref[base + j]
            for tok in range(n_tok):
                fb, kb, qb = (jnp.broadcast_to(a[j:j + 1, :], tile) for a in gates[tok])
                s = fb * s + kb * v_ref[:, tok * LANES:(tok + 1) * LANES]
                outs[tok] = outs[tok] + qb * s
            sout_ref[base + j] = s
        return tuple(outs)

    outs = lax.fori_loop(0, HEAD_DIM // SUBLANES, key_rows,
                         tuple(jnp.zeros(tile, F32) for _ in range(n_tok)))
    for tok in range(n_tok):
        cols = slice(tok * LANES, (tok + 1) * LANES)
        o = outs[tok]
        ms = jnp.mean(o * o, axis=0, keepdims=True)
        o_ref[:, cols] = o * lax.rsqrt(ms + EPS) * og_ref[...] * sg_ref[:, cols]


def _hgrn_step(hq_t, lf_t, iv_t, sg_t, state, og_col, n_tok):
    width = n_tok * LANES
    head_rows = pl.BlockSpec((HEAD_DIM, width), lambda h: (h, 0))
    state_spec = pl.BlockSpec((None, HEAD_DIM, HEAD_DIM, LANES), lambda h: (h, 0, 0, 0))
    return pl.pallas_call(
        functools.partial(_hgrn_step_kernel, n_tok=n_tok),
        grid=(N_HEADS,),
        in_specs=[head_rows, head_rows, head_rows, head_rows, state_spec,
                  pl.BlockSpec((HEAD_DIM, LANES), lambda h: (0, 0))],
        out_specs=[head_rows, state_spec],
        out_shape=[jax.ShapeDtypeStruct((GROUP_W, width), F32), jax.ShapeDtypeStruct(state.shape, F32)],
        scratch_shapes=[pltpu.VMEM((HEAD_DIM, width), F32), pltpu.VMEM((HEAD_DIM, width), F32)],
        compiler_params=pltpu.CompilerParams(dimension_semantics=("parallel",),
                                             vmem_limit_bytes=VMEM_LIMIT),
        name="hgrn_step",
    )(hq_t, lf_t, iv_t, sg_t, state, og_col)


def _hgrn_prompt_kernel(hq_ref, lf_ref, iv_ref, sg_ref, s0_ref, og_ref, bd_ref, ltri_ref,
                        o_ref, sout_ref, s_scr, *, n_chunks):
    i = pl.program_id(1)

    @pl.when(i == 0)
    def _():
        s_scr[...] = s0_ref[...]

    c_len = HG_CHUNK
    gw = HG_GROUP * HEAD_DIM
    n_groups = N_HEADS // HG_GROUP
    ltri = ltri_ref[...]
    r4 = lax.broadcasted_iota(jnp.int32, (gw, gw), 0)
    c4 = lax.broadcasted_iota(jnp.int32, (gw, gw), 1)
    same_head = (r4 // HEAD_DIM) == (c4 // HEAD_DIM)
    rt = lax.broadcasted_iota(jnp.int32, (c_len, gw), 0)
    ct = lax.broadcasted_iota(jnp.int32, (c_len, gw), 1)
    causal = (ct % c_len) <= rt
    ones_tok = jnp.ones((c_len, gw), BF16)

    def block_diag(a):
        return jnp.where(same_head, jnp.concatenate([a] * HG_GROUP, axis=0), 0.0).astype(BF16)

    states = [s_scr[g] for g in range(n_groups)]
    for c in range(n_chunks):
        rows = slice(c * c_len, (c + 1) * c_len)
        lf = lf_ref[rows, :]
        parts = _split_bf16(lf, 3)
        b = sum(_dot(ltri, part) for part in parts)
        b_col = sum(_dot_tn(part, ones_tok) for part in parts)
        b_last = b[c_len - 1:c_len, :]
        hk = 1.0 - jnp.exp(lf)
        qd = (hq_ref[rows, :].astype(F32) * jnp.exp(b)).astype(BF16)
        kd = hk * jnp.exp(-b)
        kk = (hk * jnp.exp(b_last - b)).astype(BF16)
        v = iv_ref[rows, :]
        outs = []
        for g in range(n_groups):
            sl = slice(g * gw, (g + 1) * gw)
            s = states[g]
            vg = v[:, sl]
            att = jnp.where(causal, _dot_nt(qd[:, sl], block_diag(kd[:, sl])), 0.0).astype(BF16)
            outs.append(_dot(jnp.concatenate([att, qd[:, sl]], axis=1),
                             jnp.concatenate([block_diag(vg.astype(F32)), s.astype(BF16)], axis=0)))
            ds = jnp.where(same_head, _dot_tn(kk[:, sl], vg), 0.0)
            states[g] = jnp.exp(b_col[sl, :]) * s + ds
        o = jnp.concatenate(outs, axis=1)
        gated = _head_norm(o, og_ref[...], bd_ref[...]) * sg_ref[rows, :].astype(F32)
        o_ref[rows, :] = gated.astype(o_ref.dtype)
    for g in range(n_groups):
        s_scr[g] = states[g]

    @pl.when(i == pl.num_programs(1) - 1)
    def _():
        sout_ref[...] = s_scr[...]


def _hgrn_prompt(hq, lf, iv, sg, s0_bd, og_t, bd, bx, r, tc):
    shp = (bx, r, GROUP_W)
    gw = HG_GROUP * HEAD_DIM
    tok_spec = pl.BlockSpec((None, tc, GROUP_W), lambda b, i: (b, i, 0))
    state_spec = pl.BlockSpec((None, N_HEADS // HG_GROUP, gw, gw), lambda b, i: (b, 0, 0, 0))
    const = lambda shape: pl.BlockSpec(shape, lambda b, i: (0,) * len(shape))
    ltri = (jnp.arange(HG_CHUNK)[None, :] <= jnp.arange(HG_CHUNK)[:, None]).astype(BF16)
    return pl.pallas_call(
        functools.partial(_hgrn_prompt_kernel, n_chunks=tc // HG_CHUNK),
        grid=(bx, r // tc),
        in_specs=[tok_spec, tok_spec, tok_spec, tok_spec, state_spec,
                  const((1, GROUP_W)), const((2 * LANES, 2 * LANES)), const((HG_CHUNK, HG_CHUNK))],
        out_specs=[tok_spec, state_spec],
        out_shape=[jax.ShapeDtypeStruct(shp, BF16), jax.ShapeDtypeStruct(s0_bd.shape, F32)],
        scratch_shapes=[pltpu.VMEM(s0_bd.shape[1:], F32)],
        compiler_params=pltpu.CompilerParams(dimension_semantics=("parallel", "arbitrary"),
                                             vmem_limit_bytes=VMEM_LIMIT),
        name="hgrn_prompt",
    )(hq.reshape(shp), lf.reshape(shp), iv.reshape(shp), sg.reshape(shp), s0_bd, og_t, bd, ltri)


def _tail_kernel(x_ref, oa_ref, ob_ref, g1_ref, sc_ref, sh_ref, g2_ref, n2_ref, wo_ref, wu_ref, wd_ref,
                 o_ref, acc_ref, *, ob_channel_major):
    wo_b = wo_ref[GROUP_W:2 * GROUP_W, :]
    ob = ob_ref[...].astype(BF16)
    mix = _dot(oa_ref[...].astype(BF16), wo_ref[0:GROUP_W, :])
    mix = mix + (_dot_tn(ob, wo_b) if ob_channel_major else _dot(ob, wo_b))
    x1 = x_ref[...] + g1_ref[...] * mix
    o_ref[...] = x1
    h2 = x1 * lax.rsqrt(jnp.mean(x1 * x1, axis=-1, keepdims=True) + EPS) * n2_ref[...]
    h2 = (h2 * (1.0 + sc_ref[...]) + sh_ref[...]).astype(BF16)
    for j in range(D_FF // MLP_TF):
        cols = slice(j * MLP_TF, (j + 1) * MLP_TF)
        a = jnp.square(jnp.maximum(_dot(h2, wu_ref[:, cols]), 0.0)).astype(BF16)
        part = _dot(a, wd_ref[cols, :])
        if j == 0:
            acc_ref[...] = part
        else:
            acc_ref[...] += part
    o_ref[...] += g2_ref[...] * acc_ref[...]


def _tail(x3, oa, ob, mods, n2, w_out_bf, w_up_bf, w_down_bf, tm, ob_channel_major):
    bx, r, _ = x3.shape
    row_spec = pl.BlockSpec((None, tm, D_MODEL), lambda b, i: (b, i, 0))
    half_spec = pl.BlockSpec((None, tm, GROUP_W), lambda b, i: (b, i, 0))
    ob_spec = pl.BlockSpec((None, GROUP_W, tm), lambda b, i: (b, 0, i)) if ob_channel_major else half_spec
    resident = lambda shape: pl.BlockSpec(shape, lambda b, i: (0,) * len(shape), pipeline_mode=pl.Buffered(1))
    return pl.pallas_call(
        functools.partial(_tail_kernel, ob_channel_major=ob_channel_major),
        grid=(bx, r // tm),
        in_specs=[row_spec, half_spec, ob_spec,
                  _mod_spec(mods, ADA_G1, tm), _mod_spec(mods, ADA_SC2, tm), _mod_spec(mods, ADA_SH2, tm),
                  _mod_spec(mods, ADA_G2, tm), resident((1, D_MODEL)),
                  resident((2 * GROUP_W, D_MODEL)), resident((D_MODEL, D_FF)), resident((D_FF, D_MODEL))],
        out_specs=row_spec,
        out_shape=jax.ShapeDtypeStruct(x3.shape, F32),
        scratch_shapes=[pltpu.VMEM((tm, D_MODEL), F32)],
        compiler_params=pltpu.CompilerParams(dimension_semantics=("parallel", "parallel"),
                                             vmem_limit_bytes=VMEM_LIMIT),
        name="tail",
    )(x3, oa, ob, mods, mods, mods, mods, n2, w_out_bf, w_up_bf, w_down_bf)


def _pad_rows(a, rows):
    return jnp.pad(a, ((0, 0), (0, rows - a.shape[1]), (0, 0)))


def kernel(x_prompt, x_sample, cache_k, cache_v, state_hgrn, page_table, c_prompt, c_sample,
           w_ada, b_ada, norm1_g, norm2_g, w_in, q_norm_g, k_norm_g, sb_bias, hg_lb_logits, hg_out_g,
           w_out, w_up, w_down):
    assert w_ada.shape[0] == 1 and hg_lb_logits.shape[0] == 2, "single-layer step"
    b, t, _ = x_prompt.shape
    db, dt, _ = x_sample.shape
    n_phys = cache_k.shape[1]
    tm_p = 512

    w_in_bf = w_in[0].astype(BF16)
    w_in_t = w_in[0].T.astype(BF16)
    w_out_bf = w_out[0].astype(BF16)
    w_up_bf = w_up[0].astype(BF16)
    w_down_bf = w_down[0].astype(BF16)
    n1 = norm1_g[0].reshape(1, D_MODEL)
    n2 = norm2_g[0].reshape(1, D_MODEL)
    qg_t = jnp.tile(q_norm_g[0], N_HEADS).reshape(1, GROUP_W)
    kg_t = jnp.tile(k_norm_g[0], N_HEADS).reshape(1, GROUP_W)
    kg_cm = jnp.broadcast_to(kg_t.reshape(GROUP_W, 1), (GROUP_W, tm_p))
    og_t = jnp.tile(hg_out_g[0], N_HEADS).reshape(1, GROUP_W)
    bias2 = sb_bias[0].astype(F32) * LOG2E
    bias_bc = jnp.broadcast_to(jnp.repeat(bias2, SUBLANES)[:, None], (N_QROWS, LANES))
    idx = jnp.arange(2 * LANES)
    bd = (idx[:, None] // HEAD_DIM == idx[None, :] // HEAD_DIM).astype(BF16)
    kk = jnp.arange(LANES)
    tri = jnp.concatenate([(kk[:, None] >= kk[None, :]).astype(BF16), jnp.ones((LANES, LANES), BF16)], axis=1)
    tri2 = jnp.concatenate([tri, tri], axis=0)

    ada = _ada(jnp.concatenate([c_prompt, c_sample], axis=0), w_ada[0], b_ada[0])
    mods_p = ada[:b].reshape(b, 1, N_ADA * D_MODEL)
    mods_s = jnp.tile(ada[b:], (dt, 1)).reshape(1, dt * db, N_ADA * D_MODEL)

    q, kt_p, vt_p, hq, lf, iv, sg = _proj(x_prompt, mods_p, n1, w_in_bf, w_in_t, qg_t, kg_cm,
                                           hg_lb_logits, bd, tm=tm_p)
    oa = _sb_prompt(q, kt_p, vt_p, bias2, tri2, b, t)
    n_hg = N_HEADS // HG_GROUP
    gw = HG_GROUP * HEAD_DIM
    ob, s_bd = _hgrn_prompt(hq, lf, iv, sg, jnp.zeros((b, n_hg, gw, gw), F32), og_t, bd, b, t, tc=256)
    s_bd = s_bd.reshape(b, n_hg, HG_GROUP, HEAD_DIM, HG_GROUP, HEAD_DIM)
    s_p = jnp.stack([s_bd[:, :, j, :, j, :] for j in range(HG_GROUP)], axis=2)
    s_p = s_p.reshape(b, N_HEADS, HEAD_DIM, HEAD_DIM)
    y_p = _tail(x_prompt, oa.reshape(b, t, GROUP_W), ob, mods_p, n2, w_out_bf, w_up_bf, w_down_bf,
                tm=512, ob_channel_major=False)

    assert db == LANES, "decode batch fills the lane dimension"
    ns = dt * db
    xs3 = jnp.transpose(x_sample, (1, 0, 2)).reshape(1, ns, D_MODEL)
    q, k_s, v_s, kt_s, vt_s, hq_t, lf_t, iv_t, sg_t = _proj_decode(
        xs3, mods_s, n1, w_in_bf, w_in_t, qg_t, kg_t, kg_cm, hg_lb_logits.T, bd, n_tok=dt)
    seq = lambda a: _pad_rows(jnp.transpose(a.reshape(dt, db, GROUP_W), (1, 0, 2)), SUBLANES)
    cm = lambda c: jnp.transpose(c[0], (0, 2, 3, 1)).reshape(n_phys, GROUP_W, PAGE)
    oa8 = _sb_decode(seq(q), seq(k_s), seq(v_s), cm(cache_k), cm(cache_v), page_table, bias_bc, tri2)
    oa = jnp.transpose(oa8[:, :dt], (1, 0, 2)).reshape(1, ns, GROUP_W)
    og_col = jnp.broadcast_to(hg_out_g[0][:, None], (HEAD_DIM, LANES))
    state_cm = jnp.transpose(state_hgrn[0], (1, 2, 3, 0))
    ob_t, s_s = _hgrn_step(hq_t, lf_t, iv_t, sg_t, state_cm, og_col, n_tok=dt)
    y_s = _tail(xs3, oa, ob_t[None], mods_s, n2, w_out_bf, w_up_bf, w_down_bf, tm=ns, ob_channel_major=True)
    y_s = jnp.transpose(y_s.reshape(dt, db, D_MODEL), (1, 0, 2))

    heads_cm = lambda a: jnp.transpose(a.reshape(b, N_HEADS, HEAD_DIM, t), (0, 3, 1, 2))[None]
    heads_tok = lambda a: jnp.transpose(a.reshape(dt, N_HEADS, HEAD_DIM, db), (3, 0, 1, 2))[None]
    return (y_p, y_s, heads_cm(kt_p), heads_cm(vt_p), heads_tok(kt_s), heads_tok(vt_s),
            s_p[None], jnp.transpose(s_s, (3, 0, 1, 2))[None])
```

```python
import functools
import math

import jax
import jax.numpy as jnp
from jax import lax
from jax.experimental import pallas as pl
from jax.experimental.pallas import tpu as pltpu

F32 = jnp.float32
BF16 = jnp.bfloat16

D_MODEL = 1024
N_HEADS = 8
HEAD_DIM = 64
GROUP_W = N_HEADS * HEAD_DIM
N_GROUPS = 7
D_FF = 4 * D_MODEL
N_ADA = 6
ADA_SH1, ADA_SC1, ADA_G1, ADA_SH2, ADA_SC2, ADA_G2 = range(N_ADA)
EPS = 1e-6
SB_SCALE = HEAD_DIM ** -0.5
LOG2E = math.log2(math.e)
HG_CHUNK = 64
PAGE = 128
LANES = 128
SUBLANES = 8
VMEM_LIMIT = 48 * 1024 * 1024
PAGE_GROUP = 4
HG_GROUP = 4
MLP_TF = 512


def _dot(a, b):
    return jnp.dot(a, b, preferred_element_type=F32)


def _dot_nt(a, b):
    return lax.dot_general(a, b, (((1,), (1,)), ((), ())), preferred_element_type=F32)


def _dot_tn(a, b):
    return lax.dot_general(a, b, (((0,), (0,)), ((), ())), preferred_element_type=F32)


def _split_bf16(x, n):
    parts = []
    r = x
    for i in range(n):
        p = r.astype(BF16)
        parts.append(p)
        if i + 1 < n:
            r = r - p.astype(F32)
    return parts


def _silu(x):
    return x * jax.nn.sigmoid(x)


def _head_norm(y, gain, bd):
    hi, lo = _split_bf16(y * y, 2)
    half = 2 * LANES
    ss = jnp.concatenate(
        [_dot(hi[:, c:c + half], bd) + _dot(lo[:, c:c + half], bd) for c in range(0, GROUP_W, half)],
        axis=1)
    return y * lax.rsqrt(ss * (1.0 / HEAD_DIM) + EPS) * gain


def _head_norm_cm(yt, gain, bd):
    hi, lo = _split_bf16(yt * yt, 2)
    half = 2 * LANES
    ss = jnp.concatenate(
        [_dot(bd, hi[c:c + half]) + _dot(bd, lo[c:c + half]) for c in range(0, GROUP_W, half)],
        axis=0)
    return yt * lax.rsqrt(ss * (1.0 / HEAD_DIM) + EPS) * gain


def _softplus2(z2):
    return jnp.maximum(z2, 0.0) + jnp.log(1.0 + jnp.exp2(-jnp.abs(z2))) * LOG2E


def _hi_lo(x):
    hi, lo = _split_bf16(x, 2)
    return jnp.concatenate([hi, lo], axis=1)


def _ada_kernel(c_ref, w_ref, b_ref, o_ref):
    s = _silu(c_ref[...]).astype(BF16)
    o_ref[...] = _dot(s, w_ref[...].astype(BF16)) + b_ref[...]


def _ada(c_all, w_ada, b_ada):
    m = c_all.shape[0]
    n = w_ada.shape[1]
    tn = 1024
    return pl.pallas_call(
        _ada_kernel,
        grid=(n // tn,),
        in_specs=[pl.BlockSpec((m, D_MODEL), lambda j: (0, 0)),
                  pl.BlockSpec((D_MODEL, tn), lambda j: (0, j)),
                  pl.BlockSpec((1, tn), lambda j: (0, j))],
        out_specs=pl.BlockSpec((m, tn), lambda j: (0, j)),
        out_shape=jax.ShapeDtypeStruct((m, n), F32),
        compiler_params=pltpu.CompilerParams(dimension_semantics=("parallel",),
                                             vmem_limit_bytes=VMEM_LIMIT),
        name="ada",
    )(c_all, w_ada, b_ada.reshape(1, n))


def _norm1(x_ref, sc_ref, sh_ref, n1_ref):
    x = x_ref[...]
    h = x * lax.rsqrt(jnp.mean(x * x, axis=-1, keepdims=True) + EPS) * n1_ref[...]
    return (h * (1.0 + sc_ref[...]) + sh_ref[...]).astype(BF16)


def _forget_gate_log(lb, y):
    return jnp.log(lb + (1.0 - lb) * jax.nn.sigmoid(y))


def _proj_kernel(x_ref, sc_ref, sh_ref, n1_ref, w_ref, wt_ref, qg_ref, kgc_ref, lbl_ref, bd_ref,
                 q_ref, kt_ref, vt_ref, hq_ref, lf_ref, i_ref, g_ref):
    hb = _norm1(x_ref, sc_ref, sh_ref, n1_ref)
    bd = bd_ref[...]

    def group(g):
        return _dot(hb, w_ref[:, g * GROUP_W:(g + 1) * GROUP_W])

    def group_cm(g):
        return _dot_nt(wt_ref[g * GROUP_W:(g + 1) * GROUP_W, :], hb)

    q_ref[...] = (_head_norm(group(0), qg_ref[...], bd) * (SB_SCALE * LOG2E)).astype(BF16)
    kt_ref[...] = _head_norm_cm(group_cm(1), kgc_ref[...], bd)
    vt_ref[...] = group_cm(2)
    hq_ref[...] = _silu(group(3)).astype(BF16)
    lbl = lbl_ref[...]
    e = jnp.exp(lbl - jnp.max(lbl, axis=0, keepdims=True))
    lb = e[0:1, :] / jnp.sum(e, axis=0, keepdims=True)
    lf_ref[...] = _forget_gate_log(lb, group(4))
    i_ref[...] = group(5).astype(BF16)
    g_ref[...] = _silu(group(6)).astype(BF16)


def _proj_decode_kernel(x_ref, sc_ref, sh_ref, n1_ref, w_ref, wt_ref, qg_ref, kg_ref, kgc_ref, lblt_ref, bd_ref,
                        q_ref, k_ref, v_ref, kt_ref, vt_ref, hq_ref, lf_ref, i_ref, g_ref):
    hb = _norm1(x_ref, sc_ref, sh_ref, n1_ref)
    bd = bd_ref[...]

    def group(g):
        return _dot(hb, w_ref[:, g * GROUP_W:(g + 1) * GROUP_W])

    def group_cm(g):
        return _dot_nt(wt_ref[g * GROUP_W:(g + 1) * GROUP_W, :], hb)

    q_ref[...] = _head_norm(group(0), qg_ref[...], bd) * (SB_SCALE * LOG2E)
    k_ref[...] = _head_norm(group(1), kg_ref[...], bd)
    v_ref[...] = group(2)
    kt = _head_norm_cm(group_cm(1), kgc_ref[...], bd)
    vt = group_cm(2)
    for tok in range(kt_ref.shape[0]):
        kt_ref[tok] = kt[:, tok * LANES:(tok + 1) * LANES]
        vt_ref[tok] = vt[:, tok * LANES:(tok + 1) * LANES]
    hq_ref[...] = _silu(group_cm(3))
    lblt = lblt_ref[...]
    l0, l1 = lblt[:, 0:1], lblt[:, 1:2]
    m = jnp.maximum(l0, l1)
    e0, e1 = jnp.exp(l0 - m), jnp.exp(l1 - m)
    lf_ref[...] = _forget_gate_log(e0 / (e0 + e1), group_cm(4))
    i_ref[...] = group_cm(5)
    g_ref[...] = _silu(group_cm(6))


def _mod_spec(mods, m, tm):
    if mods.shape[1] == 1:
        return pl.BlockSpec((None, 1, D_MODEL), lambda b, i, *_: (b, 0, m))
    return pl.BlockSpec((None, tm, D_MODEL), lambda b, i, *_: (b, i, m))


def _resident(shape):
    return pl.BlockSpec(shape, lambda *_: (0,) * len(shape), pipeline_mode=pl.Buffered(1))


def _proj(x3, mods, n1, w_in_bf, w_in_t, qg_t, kg_cm, lb_logits, bd, tm):
    bx, r, _ = x3.shape
    nt = r // tm
    row_spec = pl.BlockSpec((tm, GROUP_W), lambda b, i: (b * nt + i, 0))
    row_shape = lambda dt: jax.ShapeDtypeStruct((bx * r, GROUP_W), dt)
    cm_spec = pl.BlockSpec((None, GROUP_W, tm), lambda b, i: (b, 0, i))
    cm_shape = jax.ShapeDtypeStruct((bx, GROUP_W, r), F32)
    return pl.pallas_call(
        _proj_kernel,
        grid=(bx, nt),
        in_specs=[pl.BlockSpec((None, tm, D_MODEL), lambda b, i: (b, i, 0)),
                  _mod_spec(mods, ADA_SC1, tm), _mod_spec(mods, ADA_SH1, tm),
                  _resident((1, D_MODEL)), _resident(w_in_bf.shape), _resident(w_in_t.shape),
                  _resident((1, GROUP_W)), _resident(kg_cm.shape), _resident(lb_logits.shape),
                  _resident((2 * LANES, 2 * LANES))],
        out_specs=[row_spec, cm_spec, cm_spec, row_spec, row_spec, row_spec, row_spec],
        out_shape=[row_shape(BF16), cm_shape, cm_shape, row_shape(BF16), row_shape(F32),
                   row_shape(BF16), row_shape(BF16)],
        compiler_params=pltpu.CompilerParams(dimension_semantics=("parallel", "parallel"),
                                             vmem_limit_bytes=VMEM_LIMIT),
        name="proj",
    )(x3, mods, mods, n1, w_in_bf, w_in_t, qg_t, kg_cm, lb_logits, bd)


def _proj_decode(x3, mods, n1, w_in_bf, w_in_t, qg_t, kg_t, kg_cm, lb_logits_t, bd, n_tok):
    _, r, _ = x3.shape
    nseq = r // n_tok
    row_spec = pl.BlockSpec((r, GROUP_W), lambda b, i: (0, 0))
    row_shape = jax.ShapeDtypeStruct((r, GROUP_W), F32)
    tok_spec = pl.BlockSpec((n_tok, GROUP_W, nseq), lambda b, i: (0, 0, 0))
    tok_shape = jax.ShapeDtypeStruct((n_tok, GROUP_W, nseq), F32)
    cm_spec = pl.BlockSpec((GROUP_W, r), lambda b, i: (0, 0))
    cm_shape = jax.ShapeDtypeStruct((GROUP_W, r), F32)
    return pl.pallas_call(
        _proj_decode_kernel,
        grid=(1, 1),
        in_specs=[pl.BlockSpec((None, r, D_MODEL), lambda b, i: (b, i, 0)),
                  _mod_spec(mods, ADA_SC1, r), _mod_spec(mods, ADA_SH1, r),
                  _resident((1, D_MODEL)), _resident(w_in_bf.shape), _resident(w_in_t.shape),
                  _resident((1, GROUP_W)), _resident((1, GROUP_W)), _resident(kg_cm.shape),
                  _resident(lb_logits_t.shape), _resident((2 * LANES, 2 * LANES))],
        out_specs=[row_spec, row_spec, row_spec, tok_spec, tok_spec, cm_spec, cm_spec, cm_spec, cm_spec],
        out_shape=[row_shape, row_shape, row_shape, tok_shape, tok_shape,
                   cm_shape, cm_shape, cm_shape, cm_shape],
        compiler_params=pltpu.CompilerParams(dimension_semantics=("parallel", "parallel"),
                                             vmem_limit_bytes=VMEM_LIMIT),
        name="proj_decode",
    )(x3, mods, mods, n1, w_in_bf, w_in_t, qg_t, kg_t, kg_cm, lb_logits_t, bd)


def _sb_prompt_kernel(bias_ref, q_ref, qe_ref, kt_ref, vt_ref, tri_ref, o_ref,
                      qx_ref, r_ref, acc_ref, z0_ref, hl0_ref, w0_ref, z1_ref, hl1_ref, w1_ref):
    p = pl.program_id(1)
    t = q_ref.shape[0]
    n_blocks = t // LANES
    lane2 = lax.broadcasted_iota(jnp.int32, (1, 2 * LANES), 1)
    bias_row = jnp.where(lane2 < LANES, bias_ref[2 * p], bias_ref[2 * p + 1])
    brow = lax.broadcasted_iota(jnp.int32, (LANES, 2 * LANES), 0)
    bias_blk = jnp.zeros((LANES, 2 * LANES), F32)
    for n, part in enumerate(_split_bf16(bias_row, 3)):
        bias_blk = jnp.where(brow == n, part.astype(F32), bias_blk)
    bias_blk = bias_blk.astype(BF16)
    zeros_half = jnp.zeros((HEAD_DIM, LANES), F32)
    qx_ref[:, :LANES] = q_ref[...]
    qx_ref[:, LANES:] = qe_ref[...]
    row = lax.broadcasted_iota(jnp.int32, (LANES, LANES), 0)
    col = lax.broadcasted_iota(jnp.int32, (LANES, LANES), 1)
    strictly_before = col < row

    def stacked(ref, c):
        blk = ref[:, c * LANES:(c + 1) * LANES]
        top = jnp.concatenate([blk[:HEAD_DIM], zeros_half], axis=0)
        bot = jnp.concatenate([zeros_half, blk[HEAD_DIM:]], axis=0)
        return jnp.concatenate([top, bot], axis=1).astype(BF16)

    def diag_masked(x):
        top = jnp.where(strictly_before, x[:LANES], 0.0)
        return top if x.shape[0] == LANES else jnp.concatenate([top, x[LANES:]], axis=0)

    def scores(c, z_ref, hl_ref):
        rows = slice(c * LANES, t)
        z = _dot(qx_ref[rows, :], jnp.concatenate([stacked(kt_ref, c), bias_blk], axis=0))
        l = _softplus2(z)
        z_ref[rows, :] = z
        for hh in range(2):
            lh = diag_masked(l[:, hh * LANES:(hh + 1) * LANES])
            hl_ref[rows, 2 * hh * LANES:2 * (hh + 1) * LANES] = _hi_lo(lh)

    def tails(c, z_ref, hl_ref, w_ref):
        rows = slice(c * LANES, t)
        tri = tri_ref[...]
        for hh in range(2):
            cs2 = _dot(hl_ref[rows, 2 * hh * LANES:2 * (hh + 1) * LANES], tri)
            w = jnp.exp2(z_ref[rows, hh * LANES:(hh + 1) * LANES] - cs2[:, :LANES] - r_ref[hh, rows, :])
            r_ref[hh, rows, :] += cs2[:, LANES:]
            w_ref[rows, hh * LANES:(hh + 1) * LANES] = diag_masked(w).astype(BF16)

    def values(c, w_ref):
        rows = slice(c * LANES, t)
        acc_ref[rows, :] += _dot_nt(w_ref[rows, :], stacked(vt_ref, c))

    r_ref[...] = jnp.zeros_like(r_ref)
    acc_ref[...] = jnp.zeros_like(acc_ref)
    slots = ((z0_ref, hl0_ref, w0_ref), (z1_ref, hl1_ref, w1_ref))
    order = list(reversed(range(n_blocks)))
    for step in range(n_blocks + 2):
        if step < n_blocks:
            z_ref, hl_ref, _ = slots[step % 2]
            scores(order[step], z_ref, hl_ref)
        if 1 <= step <= n_blocks:
            tails(order[step - 1], *slots[(step - 1) % 2])
        if step >= 2:
            values(order[step - 2], slots[step % 2][2])
    o_ref[...] = acc_ref[...].astype(BF16)


def _sb_prompt(q, kt, vt, bias2, tri2, b, t):
    q3 = q.reshape(b, t, GROUP_W)
    pair_cm = pl.BlockSpec((None, LANES, t), lambda bb, p: (bb, p, 0))
    tile = pl.BlockSpec((None, t, LANES), lambda bb, p: (bb, 0, p))
    n_bias_cols = 3
    qe = jnp.broadcast_to((jnp.arange(LANES) < n_bias_cols).astype(BF16)[None, :], (t, LANES))
    stage_bufs = [pltpu.VMEM((t, 2 * LANES), F32), pltpu.VMEM((t, 4 * LANES), BF16),
                  pltpu.VMEM((t, 2 * LANES), BF16)]
    out = pl.pallas_call(
        _sb_prompt_kernel,
        grid=(b, N_HEADS // 2),
        in_specs=[pl.BlockSpec(memory_space=pltpu.SMEM), tile, _resident((t, LANES)), pair_cm, pair_cm,
                  _resident((2 * LANES, 2 * LANES))],
        out_specs=tile,
        out_shape=jax.ShapeDtypeStruct((b, t, GROUP_W), BF16),
        scratch_shapes=[pltpu.VMEM((t, 2 * LANES), BF16),
                        pltpu.VMEM((2, t, LANES), F32), pltpu.VMEM((t, LANES), F32)] + stage_bufs + stage_bufs,
        compiler_params=pltpu.CompilerParams(dimension_semantics=("parallel", "parallel"),
                                             vmem_limit_bytes=VMEM_LIMIT),
        name="sb_prompt",
    )(bias2, q3, qe, kt, vt, tri2)
    return out.reshape(b * t, GROUP_W)


N_QROWS = N_HEADS * SUBLANES


def _sb_decode_kernel(pt_ref, q_ref, kn_ref, vn_ref, *rest, n_pages):
    del pt_ref
    k_refs, v_refs = rest[:n_pages], rest[n_pages:2 * n_pages]
    bias_ref, tri_ref, o_ref = rest[2 * n_pages:]
    gsz = PAGE_GROUP
    tri = tri_ref[...]
    bias = bias_ref[...]
    row = lax.broadcasted_iota(jnp.int32, (N_QROWS, GROUP_W), 0)
    lane = lax.broadcasted_iota(jnp.int32, (N_QROWS, GROUP_W), 1)
    own_head = (row // SUBLANES) == (lane // HEAD_DIM)
    qm = jnp.where(own_head, jnp.concatenate([q_ref[...]] * N_HEADS, axis=0), 0.0).astype(BF16)

    def log_terms(z, nblk, valid):
        l = _softplus2(z)
        if valid is not None:
            l = jnp.where(valid, l, 0.0)
        return jnp.concatenate([_hi_lo(l[:, j * LANES:(j + 1) * LANES]) for j in range(nblk)], axis=0)

    def weights(z, cs2, r, nblk, valid):
        ws = []
        for j in range(nblk):
            blk = cs2[j * N_QROWS:(j + 1) * N_QROWS]
            w = jnp.exp2(z[:, j * LANES:(j + 1) * LANES] - blk[:, :LANES] - r)
            if valid is not None:
                w = jnp.where(valid, w, 0.0)
            r = r + blk[:, LANES:]
            ws.append(w.astype(BF16))
        return jnp.concatenate(ws, axis=1), r

    def group_scores(g):
        kt = jnp.concatenate([k_refs[g * gsz + j][...].astype(BF16) for j in range(gsz)], axis=1)
        z = _dot(qm, kt) + jnp.concatenate([bias] * gsz, axis=1)
        return z, log_terms(z, gsz, None)

    def group_pv(g, w):
        vt = jnp.concatenate([v_refs[g * gsz + j][...].astype(BF16) for j in range(gsz)], axis=1)
        return _dot_nt(w, vt)

    pad = jnp.zeros((PAGE - SUBLANES, GROUP_W), F32)
    kb = jnp.concatenate([kn_ref[...], pad], axis=0).astype(BF16)
    vb = jnp.concatenate([vn_ref[...], pad], axis=0).astype(BF16)
    r2 = lax.broadcasted_iota(jnp.int32, (N_QROWS, PAGE), 0)
    c2 = lax.broadcasted_iota(jnp.int32, (N_QROWS, PAGE), 1)
    own_valid = c2 < (r2 % SUBLANES)
    z_own = _dot_nt(qm, kb) + bias
    hl_own = log_terms(z_own, 1, own_valid)

    n_groups = n_pages // gsz
    z, hl = group_scores(0)
    w_own, r = weights(z_own, _dot(hl_own, tri), jnp.zeros((N_QROWS, LANES), F32), 1, own_valid)
    acc = _dot(w_own, vb)
    cs2 = _dot(hl, tri)
    for g in range(n_groups):
        if g + 1 < n_groups:
            z_next, hl_next = group_scores(g + 1)
        w, r = weights(z, cs2, r, gsz, None)
        acc = acc + group_pv(g, w)
        if g + 1 < n_groups:
            z, cs2 = z_next, _dot(hl_next, tri)

    own = jnp.where(own_head, acc, 0.0)
    o = own[0:SUBLANES]
    for h in range(1, N_HEADS):
        o = o + own[h * SUBLANES:(h + 1) * SUBLANES]
    o_ref[...] = o


def _sb_decode(q8, kn8, vn8, kc_cm, vc_cm, page_table, bias_bc, tri2):
    db = q8.shape[0]
    n_pages = page_table.shape[1]
    pt_flat = page_table.reshape(-1)

    def page_spec(j):
        return pl.BlockSpec((None, GROUP_W, PAGE),
                            lambda s, pt: (pt[s * n_pages + n_pages - 1 - j], 0, 0))

    seq_spec = pl.BlockSpec((None, SUBLANES, GROUP_W), lambda s, pt: (s, 0, 0))
    grid_spec = pltpu.PrefetchScalarGridSpec(
        num_scalar_prefetch=1,
        grid=(db,),
        in_specs=[seq_spec, seq_spec, seq_spec]
                 + [page_spec(j) for j in range(n_pages)] + [page_spec(j) for j in range(n_pages)]
                 + [pl.BlockSpec((N_QROWS, LANES), lambda s, pt: (0, 0)),
                    pl.BlockSpec((2 * LANES, 2 * LANES), lambda s, pt: (0, 0))],
        out_specs=seq_spec)
    return pl.pallas_call(
        functools.partial(_sb_decode_kernel, n_pages=n_pages),
        grid_spec=grid_spec,
        out_shape=jax.ShapeDtypeStruct((db, SUBLANES, GROUP_W), F32),
        compiler_params=pltpu.CompilerParams(dimension_semantics=("parallel",),
                                             vmem_limit_bytes=VMEM_LIMIT),
        name="sb_decode",
    )(pt_flat, q8, kn8, vn8, *([kc_cm] * n_pages), *([vc_cm] * n_pages), bias_bc, tri2)


def _hgrn_step_kernel(q_ref, lf_ref, v_ref, sg_ref, s_ref, og_ref, o_ref, sout_ref, f_scr, k_scr, *, n_tok):
    f = jnp.exp(lf_ref[...])
    f_scr[...] = f
    k_scr[...] = 1.0 - f
    tile = (HEAD_DIM, LANES)

    def key_rows(g, outs):
        base = pl.multiple_of(g * SUBLANES, SUBLANES)
        outs = list(outs)
        gates = [[ref[pl.ds(base, SUBLANES), tok * LANES:(tok + 1) * LANES] for ref in (f_scr, k_scr, q_ref)]
                 for tok in range(n_tok)]
        for j in range(SUBLANES):
            s = s_ref[base + j]
            for tok in range(n_tok):
                fb, kb, qb = (jnp.broadcast_to(a[j:j + 1, :], tile) for a in gates[tok])
                s = fb * s + kb * v_ref[:, tok * LANES:(tok + 1) * LANES]
                outs[tok] = outs[tok] + qb * s
            sout_ref[base + j] = s
        return tuple(outs)

    outs = lax.fori_loop(0, HEAD_DIM // SUBLANES, key_rows,
                         tuple(jnp.zeros(tile, F32) for _ in range(n_tok)))
    for tok in range(n_tok):
        cols = slice(tok * LANES, (tok + 1) * LANES)
        o = outs[tok]
        ms = jnp.mean(o * o, axis=0, keepdims=True)
        o_ref[:, cols] = o * lax.rsqrt(ms + EPS) * og_ref[...] * sg_ref[:, cols]


def _hgrn_step(hq_t, lf_t, iv_t, sg_t, state, og_col, n_tok):
    width = n_tok * LANES
    head_rows = pl.BlockSpec((HEAD_DIM, width), lambda h: (h, 0))
    state_spec = pl.BlockSpec((None, HEAD_DIM, HEAD_DIM, LANES), lambda h: (h, 0, 0, 0))
    return pl.pallas_call(
        functools.partial(_hgrn_step_kernel, n_tok=n_tok),
        grid=(N_HEADS,),
        in_specs=[head_rows, head_rows, head_rows, head_rows, state_spec,
                  pl.BlockSpec((HEAD_DIM, LANES), lambda h: (0, 0))],
        out_specs=[head_rows, state_spec],
        out_shape=[jax.ShapeDtypeStruct((GROUP_W, width), F32), jax.ShapeDtypeStruct(state.shape, F32)],
        scratch_shapes=[pltpu.VMEM((HEAD_DIM, width), F32), pltpu.VMEM((HEAD_DIM, width), F32)],
        compiler_params=pltpu.CompilerParams(dimension_semantics=("parallel",),
                                             vmem_limit_bytes=VMEM_LIMIT),
        name="hgrn_step",
    )(hq_t, lf_t, iv_t, sg_t, state, og_col)


def _hgrn_prompt_kernel(hq_ref, lf_ref, iv_ref, sg_ref, s0_ref, og_ref, bd_ref, ltri_ref,
                        o_ref, sout_ref, s_scr, *, n_chunks):
    i = pl.program_id(1)

    @pl.when(i == 0)
    def _():
        s_scr[...] = s0_ref[...]

    c_len = HG_CHUNK
    gw = HG_GROUP * HEAD_DIM
    n_groups = N_HEADS // HG_GROUP
    ltri = ltri_ref[...]
    r4 = lax.broadcasted_iota(jnp.int32, (gw, gw), 0)
    c4 = lax.broadcasted_iota(jnp.int32, (gw, gw), 1)
    same_head = (r4 // HEAD_DIM) == (c4 // HEAD_DIM)
    rt = lax.broadcasted_iota(jnp.int32, (c_len, gw), 0)
    ct = lax.broadcasted_iota(jnp.int32, (c_len, gw), 1)
    causal = (ct % c_len) <= rt

    def block_diag(a):
        return jnp.where(same_head, jnp.concatenate([a] * HG_GROUP, axis=0), 0.0).astype(BF16)

    states = [s_scr[g] for g in range(n_groups)]
    for c in range(n_chunks):
        rows = slice(c * c_len, (c + 1) * c_len)
        lf = lf_ref[rows, :]
        b = sum(_dot(ltri, part) for part in _split_bf16(lf, 3))
        b_last = b[c_len - 1:c_len, :]
        hk = 1.0 - jnp.exp(lf)
        qd = (hq_ref[rows, :].astype(F32) * jnp.exp(b)).astype(BF16)
        kd = hk * jnp.exp(-b)
        kk = (hk * jnp.exp(b_last - b)).astype(BF16)
        v = iv_ref[rows, :]
        outs = []
        for g in range(n_groups):
            sl = slice(g * gw, (g + 1) * gw)
            s = states[g]
            vg = v[:, sl]
            att = jnp.where(causal, _dot_nt(qd[:, sl], block_diag(kd[:, sl])), 0.0).astype(BF16)
            outs.append(_dot(att, block_diag(vg.astype(F32))) + _dot_nt(qd[:, sl], s.astype(BF16)))
            ds = jnp.where(same_head, _dot_tn(vg, kk[:, sl]), 0.0)
            states[g] = s * jnp.exp(b_last[:, sl]) + ds
        o = jnp.concatenate(outs, axis=1)
        gated = _head_norm(o, og_ref[...], bd_ref[...]) * sg_ref[rows, :].astype(F32)
        o_ref[rows, :] = gated.astype(o_ref.dtype)
    for g in range(n_groups):
        s_scr[g] = states[g]

    @pl.when(i == pl.num_programs(1) - 1)
    def _():
        sout_ref[...] = s_scr[...]


def _hgrn_prompt(hq, lf, iv, sg, s0_bd, og_t, bd, bx, r, tc):
    shp = (bx, r, GROUP_W)
    gw = HG_GROUP * HEAD_DIM
    tok_spec = pl.BlockSpec((None, tc, GROUP_W), lambda b, i: (b, i, 0))
    state_spec = pl.BlockSpec((None, N_HEADS // HG_GROUP, gw, gw), lambda b, i: (b, 0, 0, 0))
    const = lambda shape: pl.BlockSpec(shape, lambda b, i: (0,) * len(shape))
    ltri = (jnp.arange(HG_CHUNK)[None, :] <= jnp.arange(HG_CHUNK)[:, None]).astype(BF16)
    return pl.pallas_call(
        functools.partial(_hgrn_prompt_kernel, n_chunks=tc // HG_CHUNK),
        grid=(bx, r // tc),
        in_specs=[tok_spec, tok_spec, tok_spec, tok_spec, state_spec,
                  const((1, GROUP_W)), const((2 * LANES, 2 * LANES)), const((HG_CHUNK, HG_CHUNK))],
        out_specs=[tok_spec, state_spec],
        out_shape=[jax.ShapeDtypeStruct(shp, BF16), jax.ShapeDtypeStruct(s0_bd.shape, F32)],
        scratch_shapes=[pltpu.VMEM(s0_bd.shape[1:], F32)],
        compiler_params=pltpu.CompilerParams(dimension_semantics=("parallel", "arbitrary"),
                                             vmem_limit_bytes=VMEM_LIMIT),
        name="hgrn_prompt",
    )(hq.reshape(shp), lf.reshape(shp), iv.reshape(shp), sg.reshape(shp), s0_bd, og_t, bd, ltri)


def _tail_kernel(x_ref, oa_ref, ob_ref, g1_ref, sc_ref, sh_ref, g2_ref, n2_ref, wo_ref, wu_ref, wd_ref,
                 o_ref, acc_ref, *, ob_channel_major):
    wo_b = wo_ref[GROUP_W:2 * GROUP_W, :]
    ob = ob_ref[...].astype(BF16)
    mix = _dot(oa_ref[...].astype(BF16), wo_ref[0:GROUP_W, :])
    mix = mix + (_dot_tn(ob, wo_b) if ob_channel_major else _dot(ob, wo_b))
    x1 = x_ref[...] + g1_ref[...] * mix
    o_ref[...] = x1
    h2 = x1 * lax.rsqrt(jnp.mean(x1 * x1, axis=-1, keepdims=True) + EPS) * n2_ref[...]
    h2 = (h2 * (1.0 + sc_ref[...]) + sh_ref[...]).astype(BF16)
    for j in range(D_FF // MLP_TF):
        cols = slice(j * MLP_TF, (j + 1) * MLP_TF)
        a = jnp.square(jnp.maximum(_dot(h2, wu_ref[:, cols]), 0.0)).astype(BF16)
        part = _dot(a, wd_ref[cols, :])
        if j == 0:
            acc_ref[...] = part
        else:
            acc_ref[...] += part
    o_ref[...] += g2_ref[...] * acc_ref[...]


def _tail(x3, oa, ob, mods, n2, w_out_bf, w_up_bf, w_down_bf, tm, ob_channel_major):
    bx, r, _ = x3.shape
    row_spec = pl.BlockSpec((None, tm, D_MODEL), lambda b, i: (b, i, 0))
    half_spec = pl.BlockSpec((None, tm, GROUP_W), lambda b, i: (b, i, 0))
    ob_spec = pl.BlockSpec((None, GROUP_W, tm), lambda b, i: (b, 0, i)) if ob_channel_major else half_spec
    resident = lambda shape: pl.BlockSpec(shape, lambda b, i: (0,) * len(shape), pipeline_mode=pl.Buffered(1))
    return pl.pallas_call(
        functools.partial(_tail_kernel, ob_channel_major=ob_channel_major),
        grid=(bx, r // tm),
        in_specs=[row_spec, half_spec, ob_spec,
                  _mod_spec(mods, ADA_G1, tm), _mod_spec(mods, ADA_SC2, tm), _mod_spec(mods, ADA_SH2, tm),
                  _mod_spec(mods, ADA_G2, tm), resident((1, D_MODEL)),
                  resident((2 * GROUP_W, D_MODEL)), resident((D_MODEL, D_FF)), resident((D_FF, D_MODEL))],
        out_specs=row_spec,
        out_shape=jax.ShapeDtypeStruct(x3.shape, F32),
        scratch_shapes=[pltpu.VMEM((tm, D_MODEL), F32)],
        compiler_params=pltpu.CompilerParams(dimension_semantics=("parallel", "parallel"),
                                             vmem_limit_bytes=VMEM_LIMIT),
        name="tail",
    )(x3, oa, ob, mods, mods, mods, mods, n2, w_out_bf, w_up_bf, w_down_bf)


def _pad_rows(a, rows):
    return jnp.pad(a, ((0, 0), (0, rows - a.shape[1]), (0, 0)))


def kernel(x_prompt, x_sample, cache_k, cache_v, state_hgrn, page_table, c_prompt, c_sample,
           w_ada, b_ada, norm1_g, norm2_g, w_in, q_norm_g, k_norm_g, sb_bias, hg_lb_logits, hg_out_g,
           w_out, w_up, w_down):
    assert w_ada.shape[0] == 1 and hg_lb_logits.shape[0] == 2, "single-layer step"
    b, t, _ = x_prompt.shape
    db, dt, _ = x_sample.shape
    n_phys = cache_k.shape[1]
    tm_p = 512

    w_in_bf = w_in[0].astype(BF16)
    w_in_t = w_in[0].T.astype(BF16)
    w_out_bf = w_out[0].astype(BF16)
    w_up_bf = w_up[0].astype(BF16)
    w_down_bf = w_down[0].astype(BF16)
    n1 = norm1_g[0].reshape(1, D_MODEL)
    n2 = norm2_g[0].reshape(1, D_MODEL)
    qg_t = jnp.tile(q_norm_g[0], N_HEADS).reshape(1, GROUP_W)
    kg_t = jnp.tile(k_norm_g[0], N_HEADS).reshape(1, GROUP_W)
    kg_cm = jnp.broadcast_to(kg_t.reshape(GROUP_W, 1), (GROUP_W, tm_p))
    og_t = jnp.tile(hg_out_g[0], N_HEADS).reshape(1, GROUP_W)
    bias2 = sb_bias[0].astype(F32) * LOG2E
    bias_bc = jnp.broadcast_to(jnp.repeat(bias2, SUBLANES)[:, None], (N_QROWS, LANES))
    idx = jnp.arange(2 * LANES)
    bd = (idx[:, None] // HEAD_DIM == idx[None, :] // HEAD_DIM).astype(BF16)
    kk = jnp.arange(LANES)
    tri = jnp.concatenate([(kk[:, None] >= kk[None, :]).astype(BF16), jnp.ones((LANES, LANES), BF16)], axis=1)
    tri2 = jnp.concatenate([tri, tri], axis=0)

    ada = _ada(jnp.concatenate([c_prompt, c_sample], axis=0), w_ada[0], b_ada[0])
    mods_p = ada[:b].reshape(b, 1, N_ADA * D_MODEL)
    mods_s = jnp.tile(ada[b:], (dt, 1)).reshape(1, dt * db, N_ADA * D_MODEL)

    q, kt_p, vt_p, hq, lf, iv, sg = _proj(x_prompt, mods_p, n1, w_in_bf, w_in_t, qg_t, kg_cm,
                                           hg_lb_logits, bd, tm=tm_p)
    oa = _sb_prompt(q, kt_p, vt_p, bias2, tri2, b, t)
    n_hg = N_HEADS // HG_GROUP
    gw = HG_GROUP * HEAD_DIM
    ob, s_bd = _hgrn_prompt(hq, lf, iv, sg, jnp.zeros((b, n_hg, gw, gw), F32), og_t, bd, b, t, tc=512)
    s_bd = s_bd.reshape(b, n_hg, HG_GROUP, HEAD_DIM, HG_GROUP, HEAD_DIM)
    s_p = jnp.stack([s_bd[:, :, j, :, j, :] for j in range(HG_GROUP)], axis=2)
    s_p = jnp.swapaxes(s_p, -1, -2).reshape(b, N_HEADS, HEAD_DIM, HEAD_DIM)
    y_p = _tail(x_prompt, oa.reshape(b, t, GROUP_W), ob, mods_p, n2, w_out_bf, w_up_bf, w_down_bf,
                tm=512, ob_channel_major=False)

    assert db == LANES, "decode batch fills the lane dimension"
    ns = dt * db
    xs3 = jnp.transpose(x_sample, (1, 0, 2)).reshape(1, ns, D_MODEL)
    q, k_s, v_s, kt_s, vt_s, hq_t, lf_t, iv_t, sg_t = _proj_decode(
        xs3, mods_s, n1, w_in_bf, w_in_t, qg_t, kg_t, kg_cm, hg_lb_logits.T, bd, n_tok=dt)
    seq = lambda a: _pad_rows(jnp.transpose(a.reshape(dt, db, GROUP_W), (1, 0, 2)), SUBLANES)
    cm = lambda c: jnp.transpose(c[0], (0, 2, 3, 1)).reshape(n_phys, GROUP_W, PAGE)
    oa8 = _sb_decode(seq(q), seq(k_s), seq(v_s), cm(cache_k), cm(cache_v), page_table, bias_bc, tri2)
    oa = jnp.transpose(oa8[:, :dt], (1, 0, 2)).reshape(1, ns, GROUP_W)
    og_col = jnp.broadcast_to(hg_out_g[0][:, None], (HEAD_DIM, LANES))
    state_cm = jnp.transpose(state_hgrn[0], (1, 2, 3, 0))
    ob_t, s_s = _hgrn_step(hq_t, lf_t, iv_t, sg_t, state_cm, og_col, n_tok=dt)
    y_s = _tail(xs3, oa, ob_t[None], mods_s, n2, w_out_bf, w_up_bf, w_down_bf, tm=ns, ob_channel_major=True)
    y_s = jnp.transpose(y_s.reshape(dt, db, D_MODEL), (1, 0, 2))

    heads_cm = lambda a: jnp.transpose(a.reshape(b, N_HEADS, HEAD_DIM, t), (0, 3, 1, 2))[None]
    heads_tok = lambda a: jnp.transpose(a.reshape(dt, N_HEADS, HEAD_DIM, db), (3, 0, 1, 2))[None]
    return (y_p, y_s, heads_cm(kt_p), heads_cm(vt_p), heads_tok(kt_s), heads_tok(vt_s),
            s_p[None], jnp.transpose(s_s, (3, 0, 1, 2))[None])
```

```python
import functools
import math

import jax
import jax.numpy as jnp
from jax import lax
from jax.experimental import pallas as pl
from jax.experimental.pallas import tpu as pltpu

F32 = jnp.float32
BF16 = jnp.bfloat16

D_MODEL = 1024
N_HEADS = 8
HEAD_DIM = 64
GROUP_W = N_HEADS * HEAD_DIM
N_GROUPS = 7
D_FF = 4 * D_MODEL
N_ADA = 6
ADA_SH1, ADA_SC1, ADA_G1, ADA_SH2, ADA_SC2, ADA_G2 = range(N_ADA)
EPS = 1e-6
SB_SCALE = HEAD_DIM ** -0.5
LOG2E = math.log2(math.e)
HG_CHUNK = 64
PAGE = 128
LANES = 128
SUBLANES = 8
VMEM_LIMIT = 48 * 1024 * 1024
PAGE_GROUP = 4
HG_GROUP = 4
MLP_TF = 512
MLP_SLICES = D_FF // MLP_TF
VMEM_LIMIT_TAIL_DECODE = 56 * 1024 * 1024


def _dot(a, b):
    return jnp.dot(a, b, preferred_element_type=F32)


def _dot_nt(a, b):
    return lax.dot_general(a, b, (((1,), (1,)), ((), ())), preferred_element_type=F32)


def _dot_tn(a, b):
    return lax.dot_general(a, b, (((0,), (0,)), ((), ())), preferred_element_type=F32)


def _split_bf16(x, n):
    parts = []
    r = x
    for i in range(n):
        p = r.astype(BF16)
        parts.append(p)
        if i + 1 < n:
            r = r - p.astype(F32)
    return parts


def _silu(x):
    return x * jax.nn.sigmoid(x)


def _head_norm(y, gain, bd):
    hi, lo = _split_bf16(y * y, 2)
    half = 2 * LANES
    ss = jnp.concatenate(
        [_dot(hi[:, c:c + half], bd) + _dot(lo[:, c:c + half], bd) for c in range(0, GROUP_W, half)],
        axis=1)
    return y * lax.rsqrt(ss * (1.0 / HEAD_DIM) + EPS) * gain


def _head_norm_cm(yt, gain, bd):
    hi, lo = _split_bf16(yt * yt, 2)
    half = 2 * LANES
    ss = jnp.concatenate(
        [_dot(bd, hi[c:c + half]) + _dot(bd, lo[c:c + half]) for c in range(0, GROUP_W, half)],
        axis=0)
    return yt * lax.rsqrt(ss * (1.0 / HEAD_DIM) + EPS) * gain


def _softplus2(z2):
    return jnp.maximum(z2, 0.0) + jnp.log(1.0 + jnp.exp2(-jnp.abs(z2))) * LOG2E


def _hi_lo(x):
    hi, lo = _split_bf16(x, 2)
    return jnp.concatenate([hi, lo], axis=1)


def _ada_kernel(c_ref, w_ref, b_ref, o_ref):
    s = _silu(c_ref[...]).astype(BF16)
    o_ref[...] = _dot(s, w_ref[...].astype(BF16)) + b_ref[...]


def _ada(c_all, w_ada, b_ada):
    m = c_all.shape[0]
    n = w_ada.shape[1]
    tn = 1024
    return pl.pallas_call(
        _ada_kernel,
        grid=(n // tn,),
        in_specs=[pl.BlockSpec((m, D_MODEL), lambda j: (0, 0)),
                  pl.BlockSpec((D_MODEL, tn), lambda j: (0, j)),
                  pl.BlockSpec((1, tn), lambda j: (0, j))],
        out_specs=pl.BlockSpec((m, tn), lambda j: (0, j)),
        out_shape=jax.ShapeDtypeStruct((m, n), F32),
        compiler_params=pltpu.CompilerParams(dimension_semantics=("parallel",),
                                             vmem_limit_bytes=VMEM_LIMIT),
        name="ada",
    )(c_all, w_ada, b_ada.reshape(1, n))


def _norm1(x_ref, sc_ref, sh_ref, n1_ref):
    x = x_ref[...]
    h = x * lax.rsqrt(jnp.mean(x * x, axis=-1, keepdims=True) + EPS) * n1_ref[...]
    return (h * (1.0 + sc_ref[...]) + sh_ref[...]).astype(BF16)


def _forget_gate_log(lb, y):
    return jnp.log(lb + (1.0 - lb) * jax.nn.sigmoid(y))


def _proj_kernel(x_ref, sc_ref, sh_ref, n1_ref, w_ref, wt_ref, qg_ref, kgc_ref, lbl_ref, bd_ref,
                 q_ref, kt_ref, vt_ref, hq_ref, lf_ref, i_ref, g_ref):
    hb = _norm1(x_ref, sc_ref, sh_ref, n1_ref)
    bd = bd_ref[...]

    def group(g):
        return _dot(hb, w_ref[:, g * GROUP_W:(g + 1) * GROUP_W])

    def group_cm(g):
        return _dot_nt(wt_ref[g * GROUP_W:(g + 1) * GROUP_W, :], hb)

    q_ref[...] = (_head_norm(group(0), qg_ref[...], bd) * (SB_SCALE * LOG2E)).astype(BF16)
    kt_ref[...] = _head_norm_cm(group_cm(1), kgc_ref[...], bd)
    vt_ref[...] = group_cm(2)
    hq_ref[...] = _silu(group(3)).astype(BF16)
    lbl = lbl_ref[...]
    e = jnp.exp(lbl - jnp.max(lbl, axis=0, keepdims=True))
    lb = e[0:1, :] / jnp.sum(e, axis=0, keepdims=True)
    lf_ref[...] = _forget_gate_log(lb, group(4))
    i_ref[...] = group(5).astype(BF16)
    g_ref[...] = _silu(group(6)).astype(BF16)


def _proj_decode_kernel(x_ref, sc_ref, sh_ref, n1_ref, w_ref, wt_ref, qg_ref, kg_ref, kgc_ref, lblt_ref, bd_ref,
                        q_ref, k_ref, v_ref, kt_ref, vt_ref, hq_ref, lf_ref, i_ref, g_ref):
    hb = _norm1(x_ref, sc_ref, sh_ref, n1_ref)
    bd = bd_ref[...]

    def group(g):
        return _dot(hb, w_ref[:, g * GROUP_W:(g + 1) * GROUP_W])

    def group_cm(g):
        return _dot_nt(wt_ref[g * GROUP_W:(g + 1) * GROUP_W, :], hb)

    q_ref[...] = _head_norm(group(0), qg_ref[...], bd) * (SB_SCALE * LOG2E)
    k_ref[...] = _head_norm(group(1), kg_ref[...], bd)
    v_ref[...] = group(2)
    kt = _head_norm_cm(group_cm(1), kgc_ref[...], bd)
    vt = group_cm(2)
    for tok in range(kt_ref.shape[0]):
        kt_ref[tok] = kt[:, tok * LANES:(tok + 1) * LANES]
        vt_ref[tok] = vt[:, tok * LANES:(tok + 1) * LANES]
    hq_ref[...] = _silu(group_cm(3))
    lblt = lblt_ref[...]
    l0, l1 = lblt[:, 0:1], lblt[:, 1:2]
    m = jnp.maximum(l0, l1)
    e0, e1 = jnp.exp(l0 - m), jnp.exp(l1 - m)
    lf_ref[...] = _forget_gate_log(e0 / (e0 + e1), group_cm(4))
    i_ref[...] = group_cm(5)
    g_ref[...] = _silu(group_cm(6))


def _mod_spec(mods, m, tm):
    if mods.shape[1] == 1:
        return pl.BlockSpec((None, 1, D_MODEL), lambda b, i, *_: (b, 0, m))
    return pl.BlockSpec((None, tm, D_MODEL), lambda b, i, *_: (b, i, m))


def _resident(shape):
    return pl.BlockSpec(shape, lambda *_: (0,) * len(shape), pipeline_mode=pl.Buffered(1))


def _proj(x3, mods, n1, w_in_bf, w_in_t, qg_t, kg_cm, lb_logits, bd, tm):
    bx, r, _ = x3.shape
    nt = r // tm
    row_spec = pl.BlockSpec((tm, GROUP_W), lambda b, i: (b * nt + i, 0))
    row_shape = lambda dt: jax.ShapeDtypeStruct((bx * r, GROUP_W), dt)
    cm_spec = pl.BlockSpec((None, GROUP_W, tm), lambda b, i: (b, 0, i))
    cm_shape = jax.ShapeDtypeStruct((bx, GROUP_W, r), F32)
    return pl.pallas_call(
        _proj_kernel,
        grid=(bx, nt),
        in_specs=[pl.BlockSpec((None, tm, D_MODEL), lambda b, i: (b, i, 0)),
                  _mod_spec(mods, ADA_SC1, tm), _mod_spec(mods, ADA_SH1, tm),
                  _resident((1, D_MODEL)), _resident(w_in_bf.shape), _resident(w_in_t.shape),
                  _resident((1, GROUP_W)), _resident(kg_cm.shape), _resident(lb_logits.shape),
                  _resident((2 * LANES, 2 * LANES))],
        out_specs=[row_spec, cm_spec, cm_spec, row_spec, row_spec, row_spec, row_spec],
        out_shape=[row_shape(BF16), cm_shape, cm_shape, row_shape(BF16), row_shape(F32),
                   row_shape(BF16), row_shape(BF16)],
        compiler_params=pltpu.CompilerParams(dimension_semantics=("parallel", "parallel"),
                                             vmem_limit_bytes=VMEM_LIMIT),
        name="proj",
    )(x3, mods, mods, n1, w_in_bf, w_in_t, qg_t, kg_cm, lb_logits, bd)


def _proj_decode(x3, mods, n1, w_in_bf, w_in_t, qg_t, kg_t, kg_cm, lb_logits_t, bd, n_tok):
    _, r, _ = x3.shape
    nseq = r // n_tok
    row_spec = pl.BlockSpec((r, GROUP_W), lambda b, i: (0, 0))
    row_shape = jax.ShapeDtypeStruct((r, GROUP_W), F32)
    tok_spec = pl.BlockSpec((n_tok, GROUP_W, nseq), lambda b, i: (0, 0, 0))
    tok_shape = jax.ShapeDtypeStruct((n_tok, GROUP_W, nseq), F32)
    cm_spec = pl.BlockSpec((GROUP_W, r), lambda b, i: (0, 0))
    cm_shape = jax.ShapeDtypeStruct((GROUP_W, r), F32)
    return pl.pallas_call(
        _proj_decode_kernel,
        grid=(1, 1),
        in_specs=[pl.BlockSpec((None, r, D_MODEL), lambda b, i: (b, i, 0)),
                  _mod_spec(mods, ADA_SC1, r), _mod_spec(mods, ADA_SH1, r),
                  _resident((1, D_MODEL)), _resident(w_in_bf.shape), _resident(w_in_t.shape),
                  _resident((1, GROUP_W)), _resident((1, GROUP_W)), _resident(kg_cm.shape),
                  _resident(lb_logits_t.shape), _resident((2 * LANES, 2 * LANES))],
        out_specs=[row_spec, row_spec, row_spec, tok_spec, tok_spec, cm_spec, cm_spec, cm_spec, cm_spec],
        out_shape=[row_shape, row_shape, row_shape, tok_shape, tok_shape,
                   cm_shape, cm_shape, cm_shape, cm_shape],
        compiler_params=pltpu.CompilerParams(dimension_semantics=("parallel", "parallel"),
                                             vmem_limit_bytes=VMEM_LIMIT),
        name="proj_decode",
    )(x3, mods, mods, n1, w_in_bf, w_in_t, qg_t, kg_t, kg_cm, lb_logits_t, bd)


def _sb_prompt_kernel(bias_ref, q_ref, qe_ref, kt_ref, vt_ref, tri_ref, o_ref,
                      qx_ref, r_ref, acc_ref, z0_ref, hl0_ref, w0_ref, z1_ref, hl1_ref, w1_ref):
    p = pl.program_id(1)
    t = q_ref.shape[0]
    n_blocks = t // LANES
    lane2 = lax.broadcasted_iota(jnp.int32, (1, 2 * LANES), 1)
    bias_row = jnp.where(lane2 < LANES, bias_ref[2 * p], bias_ref[2 * p + 1])
    brow = lax.broadcasted_iota(jnp.int32, (LANES, 2 * LANES), 0)
    bias_blk = jnp.zeros((LANES, 2 * LANES), F32)
    for n, part in enumerate(_split_bf16(bias_row, 3)):
        bias_blk = jnp.where(brow == n, part.astype(F32), bias_blk)
    bias_blk = bias_blk.astype(BF16)
    zeros_half = jnp.zeros((HEAD_DIM, LANES), F32)
    qx_ref[:, :LANES] = q_ref[...]
    qx_ref[:, LANES:] = qe_ref[...]
    row = lax.broadcasted_iota(jnp.int32, (LANES, LANES), 0)
    col = lax.broadcasted_iota(jnp.int32, (LANES, LANES), 1)
    strictly_before = col < row

    def stacked(ref, c):
        blk = ref[:, c * LANES:(c + 1) * LANES]
        top = jnp.concatenate([blk[:HEAD_DIM], zeros_half], axis=0)
        bot = jnp.concatenate([zeros_half, blk[HEAD_DIM:]], axis=0)
        return jnp.concatenate([top, bot], axis=1).astype(BF16)

    def diag_masked(x):
        top = jnp.where(strictly_before, x[:LANES], 0.0)
        return top if x.shape[0] == LANES else jnp.concatenate([top, x[LANES:]], axis=0)

    def scores(c, z_ref, hl_ref):
        rows = slice(c * LANES, t)
        z = _dot(qx_ref[rows, :], jnp.concatenate([stacked(kt_ref, c), bias_blk], axis=0))
        l = _softplus2(z)
        z_ref[rows, :] = z
        for hh in range(2):
            lh = diag_masked(l[:, hh * LANES:(hh + 1) * LANES])
            hl_ref[rows, 2 * hh * LANES:2 * (hh + 1) * LANES] = _hi_lo(lh)

    def tails(c, z_ref, hl_ref, w_ref):
        rows = slice(c * LANES, t)
        tri = tri_ref[...]
        for hh in range(2):
            cs2 = _dot(hl_ref[rows, 2 * hh * LANES:2 * (hh + 1) * LANES], tri)
            w = jnp.exp2(z_ref[rows, hh * LANES:(hh + 1) * LANES] - cs2[:, :LANES] - r_ref[hh, rows, :])
            r_ref[hh, rows, :] += cs2[:, LANES:]
            w_ref[rows, hh * LANES:(hh + 1) * LANES] = diag_masked(w).astype(BF16)

    def values(c, w_ref):
        rows = slice(c * LANES, t)
        acc_ref[rows, :] += _dot_nt(w_ref[rows, :], stacked(vt_ref, c))

    r_ref[...] = jnp.zeros_like(r_ref)
    acc_ref[...] = jnp.zeros_like(acc_ref)
    slots = ((z0_ref, hl0_ref, w0_ref), (z1_ref, hl1_ref, w1_ref))
    order = list(reversed(range(n_blocks)))
    for step in range(n_blocks + 2):
        if step < n_blocks:
            z_ref, hl_ref, _ = slots[step % 2]
            scores(order[step], z_ref, hl_ref)
        if 1 <= step <= n_blocks:
            tails(order[step - 1], *slots[(step - 1) % 2])
        if step >= 2:
            values(order[step - 2], slots[step % 2][2])
    o_ref[...] = acc_ref[...].astype(BF16)


def _sb_prompt(q, kt, vt, bias2, tri2, b, t):
    q3 = q.reshape(b, t, GROUP_W)
    pair_cm = pl.BlockSpec((None, LANES, t), lambda bb, p: (bb, p, 0))
    tile = pl.BlockSpec((None, t, LANES), lambda bb, p: (bb, 0, p))
    n_bias_cols = 3
    qe = jnp.broadcast_to((jnp.arange(LANES) < n_bias_cols).astype(BF16)[None, :], (t, LANES))
    stage_bufs = [pltpu.VMEM((t, 2 * LANES), F32), pltpu.VMEM((t, 4 * LANES), BF16),
                  pltpu.VMEM((t, 2 * LANES), BF16)]
    out = pl.pallas_call(
        _sb_prompt_kernel,
        grid=(b, N_HEADS // 2),
        in_specs=[pl.BlockSpec(memory_space=pltpu.SMEM), tile, _resident((t, LANES)), pair_cm, pair_cm,
                  _resident((2 * LANES, 2 * LANES))],
        out_specs=tile,
        out_shape=jax.ShapeDtypeStruct((b, t, GROUP_W), BF16),
        scratch_shapes=[pltpu.VMEM((t, 2 * LANES), BF16),
                        pltpu.VMEM((2, t, LANES), F32), pltpu.VMEM((t, LANES), F32)] + stage_bufs + stage_bufs,
        compiler_params=pltpu.CompilerParams(dimension_semantics=("parallel", "parallel"),
                                             vmem_limit_bytes=VMEM_LIMIT),
        name="sb_prompt",
    )(bias2, q3, qe, kt, vt, tri2)
    return out.reshape(b * t, GROUP_W)


N_QROWS = N_HEADS * SUBLANES


def _decode_attend(q8, kn8, vn8, k_page, v_page, n_pages, bias, tri):
    gsz = PAGE_GROUP
    row = lax.broadcasted_iota(jnp.int32, (N_QROWS, GROUP_W), 0)
    lane = lax.broadcasted_iota(jnp.int32, (N_QROWS, GROUP_W), 1)
    own_head = (row // SUBLANES) == (lane // HEAD_DIM)
    qm = jnp.where(own_head, jnp.concatenate([q8] * N_HEADS, axis=0), 0.0).astype(BF16)

    def log_terms(z, nblk, valid):
        l = _softplus2(z)
        if valid is not None:
            l = jnp.where(valid, l, 0.0)
        return jnp.concatenate([_hi_lo(l[:, j * LANES:(j + 1) * LANES]) for j in range(nblk)], axis=0)

    def weights(z, cs2, r, nblk, valid):
        ws = []
        for j in range(nblk):
            blk = cs2[j * N_QROWS:(j + 1) * N_QROWS]
            w = jnp.exp2(z[:, j * LANES:(j + 1) * LANES] - blk[:, :LANES] - r)
            if valid is not None:
                w = jnp.where(valid, w, 0.0)
            r = r + blk[:, LANES:]
            ws.append(w.astype(BF16))
        return jnp.concatenate(ws, axis=1), r

    def group_scores(g):
        kt = jnp.concatenate([k_page(g * gsz + j).astype(BF16) for j in range(gsz)], axis=1)
        z = _dot(qm, kt) + jnp.concatenate([bias] * gsz, axis=1)
        return z, log_terms(z, gsz, None)

    def group_pv(g, w):
        vt = jnp.concatenate([v_page(g * gsz + j).astype(BF16) for j in range(gsz)], axis=1)
        return _dot_nt(w, vt)

    pad = jnp.zeros((PAGE - SUBLANES, GROUP_W), F32)
    kb = jnp.concatenate([kn8, pad], axis=0).astype(BF16)
    vb = jnp.concatenate([vn8, pad], axis=0).astype(BF16)
    r2 = lax.broadcasted_iota(jnp.int32, (N_QROWS, PAGE), 0)
    c2 = lax.broadcasted_iota(jnp.int32, (N_QROWS, PAGE), 1)
    own_valid = c2 < (r2 % SUBLANES)
    n_groups = n_pages // gsz
    live = {}

    def score_stage():
        z_own = _dot_nt(qm, kb) + bias
        live["z"] = [z_own] + [None] * n_groups
        live["hl"] = [log_terms(z_own, 1, own_valid)] + [None] * n_groups
        for g in range(n_groups):
            live["z"][g + 1], live["hl"][g + 1] = group_scores(g)

    def cumsum_stage():
        live["cs2"] = [_dot(hl, tri) for hl in live["hl"]]

    def weight_stage():
        w_own, r = weights(live["z"][0], live["cs2"][0], jnp.zeros((N_QROWS, LANES), F32), 1, own_valid)
        live["w"] = [w_own]
        for g in range(n_groups):
            w, r = weights(live["z"][g + 1], live["cs2"][g + 1], r, gsz, None)
            live["w"].append(w)

    def value_stage():
        acc = _dot(live["w"][0], vb)
        for g in range(n_groups):
            acc = acc + group_pv(g, live["w"][g + 1])
        own = jnp.where(own_head, acc, 0.0)
        o = own[0:SUBLANES]
        for h in range(1, N_HEADS):
            o = o + own[h * SUBLANES:(h + 1) * SUBLANES]
        return o

    return score_stage, cumsum_stage, weight_stage, value_stage


def _hgrn_step_kernel(q_ref, lf_ref, v_ref, sg_ref, s_ref, og_ref, o_ref, sout_ref, f_scr, k_scr, *, n_tok):
    f = jnp.exp(lf_ref[...])
    f_scr[...] = f
    k_scr[...] = 1.0 - f
    tile = (HEAD_DIM, LANES)

    def key_rows(g, outs):
        base = pl.multiple_of(g * SUBLANES, SUBLANES)
        outs = list(outs)
        gates = [[ref[pl.ds(base, SUBLANES), tok * LANES:(tok + 1) * LANES] for ref in (f_scr, k_scr, q_ref)]
                 for tok in range(n_tok)]
        for j in range(SUBLANES):
            s = s_ref[base + j]
            for tok in range(n_tok):
                fb, kb, qb = (jnp.broadcast_to(a[j:j + 1, :], tile) for a in gates[tok])
                s = fb * s + kb * v_ref[:, tok * LANES:(tok + 1) * LANES]
                outs[tok] = outs[tok] + qb * s
            sout_ref[base + j] = s
        return tuple(outs)

    outs = lax.fori_loop(0, HEAD_DIM // SUBLANES, key_rows,
                         tuple(jnp.zeros(tile, F32) for _ in range(n_tok)))
    for tok in range(n_tok):
        cols = slice(tok * LANES, (tok + 1) * LANES)
        o = outs[tok]
        ms = jnp.mean(o * o, axis=0, keepdims=True)
        o_ref[:, cols] = o * lax.rsqrt(ms + EPS) * og_ref[...] * sg_ref[:, cols]


def _hgrn_step(hq_t, lf_t, iv_t, sg_t, state, og_col, n_tok):
    width = n_tok * LANES
    head_rows = pl.BlockSpec((HEAD_DIM, width), lambda h: (h, 0))
    state_spec = pl.BlockSpec((None, HEAD_DIM, HEAD_DIM, LANES), lambda h: (h, 0, 0, 0))
    return pl.pallas_call(
        functools.partial(_hgrn_step_kernel, n_tok=n_tok),
        grid=(N_HEADS,),
        in_specs=[head_rows, head_rows, head_rows, head_rows, state_spec,
                  pl.BlockSpec((HEAD_DIM, LANES), lambda h: (0, 0))],
        out_specs=[head_rows, state_spec],
        out_shape=[jax.ShapeDtypeStruct((GROUP_W, width), F32), jax.ShapeDtypeStruct(state.shape, F32)],
        scratch_shapes=[pltpu.VMEM((HEAD_DIM, width), F32), pltpu.VMEM((HEAD_DIM, width), F32)],
        compiler_params=pltpu.CompilerParams(dimension_semantics=("parallel",),
                                             vmem_limit_bytes=VMEM_LIMIT),
        name="hgrn_step",
    )(hq_t, lf_t, iv_t, sg_t, state, og_col)


def _hgrn_prompt_kernel(hq_ref, lf_ref, iv_ref, sg_ref, s0_ref, og_ref, bd_ref, ltri_ref,
                        o_ref, sout_ref, s_scr, *, n_chunks):
    i = pl.program_id(1)

    @pl.when(i == 0)
    def _():
        s_scr[...] = s0_ref[...]

    c_len = HG_CHUNK
    gw = HG_GROUP * HEAD_DIM
    n_groups = N_HEADS // HG_GROUP
    ltri = ltri_ref[...]
    r4 = lax.broadcasted_iota(jnp.int32, (gw, gw), 0)
    c4 = lax.broadcasted_iota(jnp.int32, (gw, gw), 1)
    same_head = (r4 // HEAD_DIM) == (c4 // HEAD_DIM)
    rt = lax.broadcasted_iota(jnp.int32, (c_len, gw), 0)
    ct = lax.broadcasted_iota(jnp.int32, (c_len, gw), 1)
    causal = (ct % c_len) <= rt

    def block_diag(a):
        return jnp.where(same_head, jnp.concatenate([a] * HG_GROUP, axis=0), 0.0).astype(BF16)

    states = [s_scr[g] for g in range(n_groups)]
    for c in range(n_chunks):
        rows = slice(c * c_len, (c + 1) * c_len)
        lf = lf_ref[rows, :]
        b = sum(_dot(ltri, part) for part in _split_bf16(lf, 3))
        b_last = b[c_len - 1:c_len, :]
        hk = 1.0 - jnp.exp(lf)
        qd = (hq_ref[rows, :].astype(F32) * jnp.exp(b)).astype(BF16)
        kd = hk * jnp.exp(-b)
        kk = (hk * jnp.exp(b_last - b)).astype(BF16)
        v = iv_ref[rows, :]
        outs = []
        for g in range(n_groups):
            sl = slice(g * gw, (g + 1) * gw)
            s = states[g]
            vg = v[:, sl]
            att = jnp.where(causal, _dot_nt(qd[:, sl], block_diag(kd[:, sl])), 0.0).astype(BF16)
            outs.append(_dot(att, block_diag(vg.astype(F32))) + _dot_nt(qd[:, sl], s.astype(BF16)))
            ds = jnp.where(same_head, _dot_tn(vg, kk[:, sl]), 0.0)
            states[g] = s * jnp.exp(b_last[:, sl]) + ds
        o = jnp.concatenate(outs, axis=1)
        gated = _head_norm(o, og_ref[...], bd_ref[...]) * sg_ref[rows, :].astype(F32)
        o_ref[rows, :] = gated.astype(o_ref.dtype)
    for g in range(n_groups):
        s_scr[g] = states[g]

    @pl.when(i == pl.num_programs(1) - 1)
    def _():
        sout_ref[...] = s_scr[...]


def _hgrn_prompt(hq, lf, iv, sg, s0_bd, og_t, bd, bx, r, tc):
    shp = (bx, r, GROUP_W)
    gw = HG_GROUP * HEAD_DIM
    tok_spec = pl.BlockSpec((None, tc, GROUP_W), lambda b, i: (b, i, 0))
    state_spec = pl.BlockSpec((None, N_HEADS // HG_GROUP, gw, gw), lambda b, i: (b, 0, 0, 0))
    const = lambda shape: pl.BlockSpec(shape, lambda b, i: (0,) * len(shape))
    ltri = (jnp.arange(HG_CHUNK)[None, :] <= jnp.arange(HG_CHUNK)[:, None]).astype(BF16)
    return pl.pallas_call(
        functools.partial(_hgrn_prompt_kernel, n_chunks=tc // HG_CHUNK),
        grid=(bx, r // tc),
        in_specs=[tok_spec, tok_spec, tok_spec, tok_spec, state_spec,
                  const((1, GROUP_W)), const((2 * LANES, 2 * LANES)), const((HG_CHUNK, HG_CHUNK))],
        out_specs=[tok_spec, state_spec],
        out_shape=[jax.ShapeDtypeStruct(shp, BF16), jax.ShapeDtypeStruct(s0_bd.shape, F32)],
        scratch_shapes=[pltpu.VMEM(s0_bd.shape[1:], F32)],
        compiler_params=pltpu.CompilerParams(dimension_semantics=("parallel", "arbitrary"),
                                             vmem_limit_bytes=VMEM_LIMIT),
        name="hgrn_prompt",
    )(hq.reshape(shp), lf.reshape(shp), iv.reshape(shp), sg.reshape(shp), s0_bd, og_t, bd, ltri)


def _tail_compute(x_ref, oa_ref, ob_ref, g1_ref, sc_ref, sh_ref, g2_ref, n2_ref, wo_ref, wu_ref, wd_ref,
                  o_ref, acc_ref, ob_channel_major, side_work=None):
    wo_b = wo_ref[GROUP_W:2 * GROUP_W, :]
    ob = ob_ref[...].astype(BF16)
    mix = _dot(oa_ref[...].astype(BF16), wo_ref[0:GROUP_W, :])
    mix = mix + (_dot_tn(ob, wo_b) if ob_channel_major else _dot(ob, wo_b))
    x1 = x_ref[...] + g1_ref[...] * mix
    o_ref[...] = x1
    h2 = x1 * lax.rsqrt(jnp.mean(x1 * x1, axis=-1, keepdims=True) + EPS) * n2_ref[...]
    h2 = (h2 * (1.0 + sc_ref[...]) + sh_ref[...]).astype(BF16)
    for j in range(MLP_SLICES):
        cols = slice(j * MLP_TF, (j + 1) * MLP_TF)
        if side_work is not None:
            side_work(j, 0)
        a = jnp.square(jnp.maximum(_dot(h2, wu_ref[:, cols]), 0.0)).astype(BF16)
        if side_work is not None:
            side_work(j, 1)
        part = _dot(a, wd_ref[cols, :])
        if j == 0:
            acc_ref[...] = part
        else:
            acc_ref[...] += part
    o_ref[...] += g2_ref[...] * acc_ref[...]


def _tail_kernel(*refs, ob_channel_major):
    _tail_compute(*refs, ob_channel_major=ob_channel_major)


def _tail_specs(mods, tm, ob_channel_major):
    row_spec = pl.BlockSpec((None, tm, D_MODEL), lambda b, i, *_: (b, i, 0))
    half_spec = pl.BlockSpec((None, tm, GROUP_W), lambda b, i, *_: (b, i, 0))
    ob_spec = pl.BlockSpec((None, GROUP_W, tm), lambda b, i, *_: (b, 0, i)) if ob_channel_major else half_spec
    in_specs = [row_spec, half_spec, ob_spec,
                _mod_spec(mods, ADA_G1, tm), _mod_spec(mods, ADA_SC2, tm), _mod_spec(mods, ADA_SH2, tm),
                _mod_spec(mods, ADA_G2, tm), _resident((1, D_MODEL)),
                _resident((2 * GROUP_W, D_MODEL)), _resident((D_MODEL, D_FF)), _resident((D_FF, D_MODEL))]
    return in_specs, row_spec


def _tail(x3, oa, ob, mods, n2, w_out_bf, w_up_bf, w_down_bf, tm, ob_channel_major):
    bx, r, _ = x3.shape
    in_specs, row_spec = _tail_specs(mods, tm, ob_channel_major)
    return pl.pallas_call(
        functools.partial(_tail_kernel, ob_channel_major=ob_channel_major),
        grid=(bx, r // tm),
        in_specs=in_specs,
        out_specs=row_spec,
        out_shape=jax.ShapeDtypeStruct(x3.shape, F32),
        scratch_shapes=[pltpu.VMEM((tm, D_MODEL), F32)],
        compiler_params=pltpu.CompilerParams(dimension_semantics=("parallel", "parallel"),
                                             vmem_limit_bytes=VMEM_LIMIT),
        name="tail",
    )(x3, oa, ob, mods, mods, mods, mods, n2, w_out_bf, w_up_bf, w_down_bf)


def _tail_decode_kernel(pt_ref, x_ref, oa_ref, ob_ref, g1_ref, sc_ref, sh_ref, g2_ref, n2_ref, wo_ref, wu_ref,
                        wd_ref, q_ref, kn_ref, vn_ref, kc_hbm, vc_hbm, bias_ref, tri_ref,
                        o_ref, oa8_ref, acc_ref, kbuf, vbuf, sems, *, n_pages, seqs_per_step, n_seqs):
    step = pl.program_id(0) * pl.num_programs(1) + pl.program_id(1)

    def page_copies(seq, slot):
        copies = []
        for j in range(n_pages):
            page = pt_ref[seq * n_pages + (n_pages - 1 - j)]
            copies.append(pltpu.make_async_copy(kc_hbm.at[page], kbuf.at[slot, j], sems.at[slot]))
            copies.append(pltpu.make_async_copy(vc_hbm.at[page], vbuf.at[slot, j], sems.at[slot]))
        return copies

    @pl.when(step == 0)
    def _():
        for cp in page_copies(0, 0):
            cp.start()

    stages = {}

    def decode_stage(j, half):
        local, phase = divmod(2 * j + half, 2 * MLP_SLICES // seqs_per_step)
        slot = local % 2
        if phase == 0:
            seq = step * seqs_per_step + local
            for cp in page_copies(seq, slot):
                cp.wait()

            @pl.when(seq + 1 < n_seqs)
            def _():
                for cp in page_copies(seq + 1, 1 - slot):
                    cp.start()

            stages[local] = _decode_attend(
                q_ref[local], kn_ref[local], vn_ref[local],
                lambda p: kbuf[slot, p], lambda p: vbuf[slot, p], n_pages, bias_ref[...], tri_ref[...])
        if phase < len(stages[local]) - 1:
            stages[local][phase]()
        elif phase == len(stages[local]) - 1:
            oa8_ref[local] = stages[local][phase]()

    _tail_compute(x_ref, oa_ref, ob_ref, g1_ref, sc_ref, sh_ref, g2_ref, n2_ref, wo_ref, wu_ref, wd_ref,
                  o_ref, acc_ref, ob_channel_major=False, side_work=decode_stage)


def _tail_decode(x3, oa, ob, mods, n2, w_out_bf, w_up_bf, w_down_bf, tm,
                 q8, kn8, vn8, kc_cm, vc_cm, page_table, bias_bc, tri2):
    bx, r, _ = x3.shape
    n_seqs, n_pages = page_table.shape
    n_steps = bx * (r // tm)
    seqs_per_step = n_seqs // n_steps
    assert seqs_per_step * n_steps == n_seqs and seqs_per_step % 2 == 0
    assert 2 * MLP_SLICES == 4 * seqs_per_step, "one attention stage (of 4) per MLP matmul"
    in_specs, row_spec = _tail_specs(mods, tm, ob_channel_major=False)
    nt = r // tm
    seq_spec = pl.BlockSpec((seqs_per_step, SUBLANES, GROUP_W), lambda b, i, pt: (b * nt + i, 0, 0))
    page_buf = pltpu.VMEM((2, n_pages, GROUP_W, PAGE), F32)
    grid_spec = pltpu.PrefetchScalarGridSpec(
        num_scalar_prefetch=1,
        grid=(bx, nt),
        in_specs=in_specs + [seq_spec, seq_spec, seq_spec,
                             pl.BlockSpec(memory_space=pl.ANY), pl.BlockSpec(memory_space=pl.ANY),
                             _resident((N_QROWS, LANES)), _resident((2 * LANES, 2 * LANES))],
        out_specs=[row_spec, seq_spec],
        scratch_shapes=[pltpu.VMEM((tm, D_MODEL), F32), page_buf, page_buf, pltpu.SemaphoreType.DMA((2,))])
    return pl.pallas_call(
        functools.partial(_tail_decode_kernel, n_pages=n_pages, seqs_per_step=seqs_per_step, n_seqs=n_seqs),
        grid_spec=grid_spec,
        out_shape=[jax.ShapeDtypeStruct(x3.shape, F32),
                   jax.ShapeDtypeStruct((n_seqs, SUBLANES, GROUP_W), F32)],
        compiler_params=pltpu.CompilerParams(dimension_semantics=("arbitrary", "arbitrary"),
                                             vmem_limit_bytes=VMEM_LIMIT_TAIL_DECODE),
        name="tail_decode",
    )(page_table.reshape(-1), x3, oa, ob, mods, mods, mods, mods, n2, w_out_bf, w_up_bf, w_down_bf,
      q8, kn8, vn8, kc_cm, vc_cm, bias_bc, tri2)


def _pad_rows(a, rows):
    return jnp.pad(a, ((0, 0), (0, rows - a.shape[1]), (0, 0)))


def kernel(x_prompt, x_sample, cache_k, cache_v, state_hgrn, page_table, c_prompt, c_sample,
           w_ada, b_ada, norm1_g, norm2_g, w_in, q_norm_g, k_norm_g, sb_bias, hg_lb_logits, hg_out_g,
           w_out, w_up, w_down):
    assert w_ada.shape[0] == 1 and hg_lb_logits.shape[0] == 2, "single-layer step"
    b, t, _ = x_prompt.shape
    db, dt, _ = x_sample.shape
    n_phys = cache_k.shape[1]
    tm_p = 512

    w_in_bf = w_in[0].astype(BF16)
    w_in_t = w_in[0].T.astype(BF16)
    w_out_bf = w_out[0].astype(BF16)
    w_up_bf = w_up[0].astype(BF16)
    w_down_bf = w_down[0].astype(BF16)
    n1 = norm1_g[0].reshape(1, D_MODEL)
    n2 = norm2_g[0].reshape(1, D_MODEL)
    qg_t = jnp.tile(q_norm_g[0], N_HEADS).reshape(1, GROUP_W)
    kg_t = jnp.tile(k_norm_g[0], N_HEADS).reshape(1, GROUP_W)
    kg_cm = jnp.broadcast_to(kg_t.reshape(GROUP_W, 1), (GROUP_W, tm_p))
    og_t = jnp.tile(hg_out_g[0], N_HEADS).reshape(1, GROUP_W)
    bias2 = sb_bias[0].astype(F32) * LOG2E
    bias_bc = jnp.broadcast_to(jnp.repeat(bias2, SUBLANES)[:, None], (N_QROWS, LANES))
    idx = jnp.arange(2 * LANES)
    bd = (idx[:, None] // HEAD_DIM == idx[None, :] // HEAD_DIM).astype(BF16)
    kk = jnp.arange(LANES)
    tri = jnp.concatenate([(kk[:, None] >= kk[None, :]).astype(BF16), jnp.ones((LANES, LANES), BF16)], axis=1)
    tri2 = jnp.concatenate([tri, tri], axis=0)

    ada = _ada(jnp.concatenate([c_prompt, c_sample], axis=0), w_ada[0], b_ada[0])
    mods_p = ada[:b].reshape(b, 1, N_ADA * D_MODEL)
    mods_s = jnp.tile(ada[b:], (dt, 1)).reshape(1, dt * db, N_ADA * D_MODEL)

    q, kt_p, vt_p, hq, lf, iv, sg = _proj(x_prompt, mods_p, n1, w_in_bf, w_in_t, qg_t, kg_cm,
                                           hg_lb_logits, bd, tm=tm_p)
    oa = _sb_prompt(q, kt_p, vt_p, bias2, tri2, b, t)
    n_hg = N_HEADS // HG_GROUP
    gw = HG_GROUP * HEAD_DIM
    ob, s_bd = _hgrn_prompt(hq, lf, iv, sg, jnp.zeros((b, n_hg, gw, gw), F32), og_t, bd, b, t, tc=512)
    s_bd = s_bd.reshape(b, n_hg, HG_GROUP, HEAD_DIM, HG_GROUP, HEAD_DIM)
    s_p = jnp.stack([s_bd[:, :, j, :, j, :] for j in range(HG_GROUP)], axis=2)
    s_p = jnp.swapaxes(s_p, -1, -2).reshape(b, N_HEADS, HEAD_DIM, HEAD_DIM)
    oa_p, ob_p = oa.reshape(b, t, GROUP_W), ob

    assert db == LANES, "decode batch fills the lane dimension"
    ns = dt * db
    xs3 = jnp.transpose(x_sample, (1, 0, 2)).reshape(1, ns, D_MODEL)
    q, k_s, v_s, kt_s, vt_s, hq_t, lf_t, iv_t, sg_t = _proj_decode(
        xs3, mods_s, n1, w_in_bf, w_in_t, qg_t, kg_t, kg_cm, hg_lb_logits.T, bd, n_tok=dt)
    seq = lambda a: _pad_rows(jnp.transpose(a.reshape(dt, db, GROUP_W), (1, 0, 2)), SUBLANES)
    cm = lambda c: jnp.transpose(c[0], (0, 2, 3, 1)).reshape(n_phys, GROUP_W, PAGE)
    y_p, oa8 = _tail_decode(x_prompt, oa_p, ob_p, mods_p, n2, w_out_bf, w_up_bf, w_down_bf, 512,
                            seq(q), seq(k_s), seq(v_s), cm(cache_k), cm(cache_v), page_table, bias_bc, tri2)
    oa = jnp.transpose(oa8[:, :dt], (1, 0, 2)).reshape(1, ns, GROUP_W)
    og_col = jnp.broadcast_to(hg_out_g[0][:, None], (HEAD_DIM, LANES))
    state_cm = jnp.transpose(state_hgrn[0], (1, 2, 3, 0))
    ob_t, s_s = _hgrn_step(hq_t, lf_t, iv_t, sg_t, state_cm, og_col, n_tok=dt)
    y_s = _tail(xs3, oa, ob_t[None], mods_s, n2, w_out_bf, w_up_bf, w_down_bf, tm=ns, ob_channel_major=True)
    y_s = jnp.transpose(y_s.reshape(dt, db, D_MODEL), (1, 0, 2))

    heads_cm = lambda a: jnp.transpose(a.reshape(b, N_HEADS, HEAD_DIM, t), (0, 3, 1, 2))[None]
    heads_tok = lambda a: jnp.transpose(a.reshape(dt, N_HEADS, HEAD_DIM, db), (3, 0, 1, 2))[None]
    return (y_p, y_s, heads_cm(kt_p), heads_cm(vt_p), heads_tok(kt_s), heads_tok(vt_s),
            s_p[None], jnp.transpose(s_s, (3, 0, 1, 2))[None])
```

```python
import functools
import math

import jax
import jax.numpy as jnp
from jax import lax
from jax.experimental import pallas as pl
from jax.experimental.pallas import tpu as pltpu

F32 = jnp.float32
BF16 = jnp.bfloat16

D_MODEL = 1024
N_HEADS = 8
HEAD_DIM = 64
GROUP_W = N_HEADS * HEAD_DIM
N_GROUPS = 7
D_FF = 4 * D_MODEL
N_ADA = 6
ADA_SH1, ADA_SC1, ADA_G1, ADA_SH2, ADA_SC2, ADA_G2 = range(N_ADA)
EPS = 1e-6
SB_SCALE = HEAD_DIM ** -0.5
LOG2E = math.log2(math.e)
HG_CHUNK = 64
PAGE = 128
LANES = 128
SUBLANES = 8
VMEM_LIMIT = 48 * 1024 * 1024
PAGE_GROUP = 4
HG_GROUP = 4
MLP_TF = 512
MLP_SLICES = D_FF // MLP_TF
VMEM_LIMIT_TAIL_DECODE = 56 * 1024 * 1024


def _dot(a, b):
    return jnp.dot(a, b, preferred_element_type=F32)


def _dot_nt(a, b):
    return lax.dot_general(a, b, (((1,), (1,)), ((), ())), preferred_element_type=F32)


def _dot_tn(a, b):
    return lax.dot_general(a, b, (((0,), (0,)), ((), ())), preferred_element_type=F32)


def _split_bf16(x, n):
    parts = []
    r = x
    for i in range(n):
        p = r.astype(BF16)
        parts.append(p)
        if i + 1 < n:
            r = r - p.astype(F32)
    return parts


def _silu(x):
    return x * jax.nn.sigmoid(x)


def _head_norm(y, gain, bd):
    hi, lo = _split_bf16(y * y, 2)
    half = 2 * LANES
    ss = jnp.concatenate(
        [_dot(hi[:, c:c + half], bd) + _dot(lo[:, c:c + half], bd) for c in range(0, GROUP_W, half)],
        axis=1)
    return y * lax.rsqrt(ss * (1.0 / HEAD_DIM) + EPS) * gain


def _head_norm_cm(yt, gain, bd):
    hi, lo = _split_bf16(yt * yt, 2)
    half = 2 * LANES
    ss = jnp.concatenate(
        [_dot(bd, hi[c:c + half]) + _dot(bd, lo[c:c + half]) for c in range(0, GROUP_W, half)],
        axis=0)
    return yt * lax.rsqrt(ss * (1.0 / HEAD_DIM) + EPS) * gain


def _softplus2(z2):
    return jnp.maximum(z2, 0.0) + jnp.log(1.0 + jnp.exp2(-jnp.abs(z2))) * LOG2E


def _hi_lo(x):
    hi, lo = _split_bf16(x, 2)
    return jnp.concatenate([hi, lo], axis=1)


def _ada_kernel(c_ref, w_ref, b_ref, o_ref):
    s = _silu(c_ref[...]).astype(BF16)
    o_ref[...] = _dot(s, w_ref[...].astype(BF16)) + b_ref[...]


def _ada(c_all, w_ada, b_ada):
    m = c_all.shape[0]
    n = w_ada.shape[1]
    tn = 1024
    return pl.pallas_call(
        _ada_kernel,
        grid=(n // tn,),
        in_specs=[pl.BlockSpec((m, D_MODEL), lambda j: (0, 0)),
                  pl.BlockSpec((D_MODEL, tn), lambda j: (0, j)),
                  pl.BlockSpec((1, tn), lambda j: (0, j))],
        out_specs=pl.BlockSpec((m, tn), lambda j: (0, j)),
        out_shape=jax.ShapeDtypeStruct((m, n), F32),
        compiler_params=pltpu.CompilerParams(dimension_semantics=("parallel",),
                                             vmem_limit_bytes=VMEM_LIMIT),
        name="ada",
    )(c_all, w_ada, b_ada.reshape(1, n))


def _norm1(x_ref, sc_ref, sh_ref, n1_ref):
    x = x_ref[...]
    h = x * lax.rsqrt(jnp.mean(x * x, axis=-1, keepdims=True) + EPS) * n1_ref[...]
    return (h * (1.0 + sc_ref[...]) + sh_ref[...]).astype(BF16)


def _forget_gate_log(lb, y):
    return jnp.log(lb + (1.0 - lb) * jax.nn.sigmoid(y))


def _proj_kernel(x_ref, sc_ref, sh_ref, n1_ref, w_ref, wt_ref, qg_ref, kgc_ref, lbl_ref, bd_ref,
                 q_ref, kt_ref, vt_ref, hq_ref, lf_ref, i_ref, g_ref):
    hb = _norm1(x_ref, sc_ref, sh_ref, n1_ref)
    bd = bd_ref[...]

    def group(g):
        return _dot(hb, w_ref[:, g * GROUP_W:(g + 1) * GROUP_W])

    def group_cm(g):
        return _dot_nt(wt_ref[g * GROUP_W:(g + 1) * GROUP_W, :], hb)

    yq = group(0)
    ykt = group_cm(1)
    vt_ref[...] = group_cm(2)
    hq_ref[...] = _silu(group(3)).astype(BF16)
    q_ref[...] = (_head_norm(yq, qg_ref[...], bd) * (SB_SCALE * LOG2E)).astype(BF16)
    lbl = lbl_ref[...]
    e = jnp.exp(lbl - jnp.max(lbl, axis=0, keepdims=True))
    lb = e[0:1, :] / jnp.sum(e, axis=0, keepdims=True)
    lf_ref[...] = _forget_gate_log(lb, group(4))
    kt_ref[...] = _head_norm_cm(ykt, kgc_ref[...], bd)
    i_ref[...] = group(5).astype(BF16)
    g_ref[...] = _silu(group(6)).astype(BF16)


def _proj_decode_kernel(x_ref, sc_ref, sh_ref, n1_ref, w_ref, wt_ref, qg_ref, kg_ref, kgc_ref, lblt_ref, bd_ref,
                        q_ref, k_ref, v_ref, kt_ref, vt_ref, hq_ref, lf_ref, i_ref, g_ref):
    hb = _norm1(x_ref, sc_ref, sh_ref, n1_ref)
    bd = bd_ref[...]

    def group(g):
        return _dot(hb, w_ref[:, g * GROUP_W:(g + 1) * GROUP_W])

    def group_cm(g):
        return _dot_nt(wt_ref[g * GROUP_W:(g + 1) * GROUP_W, :], hb)

    q_ref[...] = _head_norm(group(0), qg_ref[...], bd) * (SB_SCALE * LOG2E)
    k_ref[...] = _head_norm(group(1), kg_ref[...], bd)
    v_ref[...] = group(2)
    kt = _head_norm_cm(group_cm(1), kgc_ref[...], bd)
    vt = group_cm(2)
    for tok in range(kt_ref.shape[0]):
        kt_ref[tok] = kt[:, tok * LANES:(tok + 1) * LANES]
        vt_ref[tok] = vt[:, tok * LANES:(tok + 1) * LANES]
    hq_ref[...] = _silu(group_cm(3))
    lblt = lblt_ref[...]
    l0, l1 = lblt[:, 0:1], lblt[:, 1:2]
    m = jnp.maximum(l0, l1)
    e0, e1 = jnp.exp(l0 - m), jnp.exp(l1 - m)
    lf_ref[...] = _forget_gate_log(e0 / (e0 + e1), group_cm(4))
    i_ref[...] = group_cm(5)
    g_ref[...] = _silu(group_cm(6))


def _mod_spec(mods, m, tm):
    if mods.shape[1] == 1:
        return pl.BlockSpec((None, 1, D_MODEL), lambda b, i, *_: (b, 0, m))
    return pl.BlockSpec((None, tm, D_MODEL), lambda b, i, *_: (b, i, m))


def _resident(shape):
    return pl.BlockSpec(shape, lambda *_: (0,) * len(shape), pipeline_mode=pl.Buffered(1))


def _proj(x3, mods, n1, w_in_bf, w_in_t, qg_t, kg_cm, lb_logits, bd, tm):
    bx, r, _ = x3.shape
    nt = r // tm
    row_spec = pl.BlockSpec((tm, GROUP_W), lambda b, i: (b * nt + i, 0))
    row_shape = lambda dt: jax.ShapeDtypeStruct((bx * r, GROUP_W), dt)
    cm_spec = pl.BlockSpec((None, GROUP_W, tm), lambda b, i: (b, 0, i))
    cm_shape = jax.ShapeDtypeStruct((bx, GROUP_W, r), F32)
    return pl.pallas_call(
        _proj_kernel,
        grid=(bx, nt),
        in_specs=[pl.BlockSpec((None, tm, D_MODEL), lambda b, i: (b, i, 0)),
                  _mod_spec(mods, ADA_SC1, tm), _mod_spec(mods, ADA_SH1, tm),
                  _resident((1, D_MODEL)), _resident(w_in_bf.shape), _resident(w_in_t.shape),
                  _resident((1, GROUP_W)), _resident(kg_cm.shape), _resident(lb_logits.shape),
                  _resident((2 * LANES, 2 * LANES))],
        out_specs=[row_spec, cm_spec, cm_spec, row_spec, row_spec, row_spec, row_spec],
        out_shape=[row_shape(BF16), cm_shape, cm_shape, row_shape(BF16), row_shape(F32),
                   row_shape(BF16), row_shape(BF16)],
        compiler_params=pltpu.CompilerParams(dimension_semantics=("parallel", "parallel"),
                                             vmem_limit_bytes=VMEM_LIMIT),
        name="proj",
    )(x3, mods, mods, n1, w_in_bf, w_in_t, qg_t, kg_cm, lb_logits, bd)


def _proj_decode(x3, mods, n1, w_in_bf, w_in_t, qg_t, kg_t, kg_cm, lb_logits_t, bd, n_tok):
    _, r, _ = x3.shape
    nseq = r // n_tok
    row_spec = pl.BlockSpec((r, GROUP_W), lambda b, i: (0, 0))
    row_shape = jax.ShapeDtypeStruct((r, GROUP_W), F32)
    tok_spec = pl.BlockSpec((n_tok, GROUP_W, nseq), lambda b, i: (0, 0, 0))
    tok_shape = jax.ShapeDtypeStruct((n_tok, GROUP_W, nseq), F32)
    cm_spec = pl.BlockSpec((GROUP_W, r), lambda b, i: (0, 0))
    cm_shape = jax.ShapeDtypeStruct((GROUP_W, r), F32)
    return pl.pallas_call(
        _proj_decode_kernel,
        grid=(1, 1),
        in_specs=[pl.BlockSpec((None, r, D_MODEL), lambda b, i: (b, i, 0)),
                  _mod_spec(mods, ADA_SC1, r), _mod_spec(mods, ADA_SH1, r),
                  _resident((1, D_MODEL)), _resident(w_in_bf.shape), _resident(w_in_t.shape),
                  _resident((1, GROUP_W)), _resident((1, GROUP_W)), _resident(kg_cm.shape),
                  _resident(lb_logits_t.shape), _resident((2 * LANES, 2 * LANES))],
        out_specs=[row_spec, row_spec, row_spec, tok_spec, tok_spec, cm_spec, cm_spec, cm_spec, cm_spec],
        out_shape=[row_shape, row_shape, row_shape, tok_shape, tok_shape,
                   cm_shape, cm_shape, cm_shape, cm_shape],
        compiler_params=pltpu.CompilerParams(dimension_semantics=("parallel", "parallel"),
                                             vmem_limit_bytes=VMEM_LIMIT),
        name="proj_decode",
    )(x3, mods, mods, n1, w_in_bf, w_in_t, qg_t, kg_t, kg_cm, lb_logits_t, bd)


def _sb_prompt_kernel(bias_ref, q_ref, qe_ref, kt_ref, vt_ref, tri_ref, o_ref,
                      qx_ref, r_ref, acc_ref, z0_ref, hl0_ref, w0_ref, z1_ref, hl1_ref, w1_ref):
    p = pl.program_id(1)
    t = q_ref.shape[0]
    n_blocks = t // LANES
    lane2 = lax.broadcasted_iota(jnp.int32, (1, 2 * LANES), 1)
    bias_row = jnp.where(lane2 < LANES, bias_ref[2 * p], bias_ref[2 * p + 1])
    brow = lax.broadcasted_iota(jnp.int32, (LANES, 2 * LANES), 0)
    bias_blk = jnp.zeros((LANES, 2 * LANES), F32)
    for n, part in enumerate(_split_bf16(bias_row, 3)):
        bias_blk = jnp.where(brow == n, part.astype(F32), bias_blk)
    bias_blk = bias_blk.astype(BF16)
    zeros_half = jnp.zeros((HEAD_DIM, LANES), F32)
    qx_ref[:, :LANES] = q_ref[...]
    qx_ref[:, LANES:] = qe_ref[...]
    row = lax.broadcasted_iota(jnp.int32, (LANES, LANES), 0)
    col = lax.broadcasted_iota(jnp.int32, (LANES, LANES), 1)
    strictly_before = col < row

    def stacked(ref, c):
        blk = ref[:, c * LANES:(c + 1) * LANES]
        top = jnp.concatenate([blk[:HEAD_DIM], zeros_half], axis=0)
        bot = jnp.concatenate([zeros_half, blk[HEAD_DIM:]], axis=0)
        return jnp.concatenate([top, bot], axis=1).astype(BF16)

    def diag_masked(x):
        top = jnp.where(strictly_before, x[:LANES], 0.0)
        return top if x.shape[0] == LANES else jnp.concatenate([top, x[LANES:]], axis=0)

    def scores(c, z_ref, hl_ref):
        rows = slice(c * LANES, t)
        z = _dot(qx_ref[rows, :], jnp.concatenate([stacked(kt_ref, c), bias_blk], axis=0))
        l = _softplus2(z)
        z_ref[rows, :] = z
        for hh in range(2):
            lh = diag_masked(l[:, hh * LANES:(hh + 1) * LANES])
            hl_ref[rows, 2 * hh * LANES:2 * (hh + 1) * LANES] = _hi_lo(lh)

    def tails(c, z_ref, hl_ref, w_ref):
        rows = slice(c * LANES, t)
        tri = tri_ref[...]
        for hh in range(2):
            cs2 = _dot(hl_ref[rows, 2 * hh * LANES:2 * (hh + 1) * LANES], tri)
            w = jnp.exp2(z_ref[rows, hh * LANES:(hh + 1) * LANES] - cs2[:, :LANES] - r_ref[hh, rows, :])
            r_ref[hh, rows, :] += cs2[:, LANES:]
            w_ref[rows, hh * LANES:(hh + 1) * LANES] = diag_masked(w).astype(BF16)

    def values(c, w_ref):
        rows = slice(c * LANES, t)
        acc_ref[rows, :] += _dot_nt(w_ref[rows, :], stacked(vt_ref, c))

    r_ref[...] = jnp.zeros_like(r_ref)
    acc_ref[...] = jnp.zeros_like(acc_ref)
    slots = ((z0_ref, hl0_ref, w0_ref), (z1_ref, hl1_ref, w1_ref))
    order = list(reversed(range(n_blocks)))
    for step in range(n_blocks + 2):
        if step < n_blocks:
            z_ref, hl_ref, _ = slots[step % 2]
            scores(order[step], z_ref, hl_ref)
        if 1 <= step <= n_blocks:
            tails(order[step - 1], *slots[(step - 1) % 2])
        if step >= 2:
            values(order[step - 2], slots[step % 2][2])
    o_ref[...] = acc_ref[...].astype(BF16)


def _sb_prompt(q, kt, vt, bias2, tri2, b, t):
    q3 = q.reshape(b, t, GROUP_W)
    pair_cm = pl.BlockSpec((None, LANES, t), lambda bb, p: (bb, p, 0))
    tile = pl.BlockSpec((None, t, LANES), lambda bb, p: (bb, 0, p))
    n_bias_cols = 3
    qe = jnp.broadcast_to((jnp.arange(LANES) < n_bias_cols).astype(BF16)[None, :], (t, LANES))
    stage_bufs = [pltpu.VMEM((t, 2 * LANES), F32), pltpu.VMEM((t, 4 * LANES), BF16),
                  pltpu.VMEM((t, 2 * LANES), BF16)]
    out = pl.pallas_call(
        _sb_prompt_kernel,
        grid=(b, N_HEADS // 2),
        in_specs=[pl.BlockSpec(memory_space=pltpu.SMEM), tile, _resident((t, LANES)), pair_cm, pair_cm,
                  _resident((2 * LANES, 2 * LANES))],
        out_specs=tile,
        out_shape=jax.ShapeDtypeStruct((b, t, GROUP_W), BF16),
        scratch_shapes=[pltpu.VMEM((t, 2 * LANES), BF16),
                        pltpu.VMEM((2, t, LANES), F32), pltpu.VMEM((t, LANES), F32)] + stage_bufs + stage_bufs,
        compiler_params=pltpu.CompilerParams(dimension_semantics=("parallel", "parallel"),
                                             vmem_limit_bytes=VMEM_LIMIT),
        name="sb_prompt",
    )(bias2, q3, qe, kt, vt, tri2)
    return out.reshape(b * t, GROUP_W)


N_QROWS = N_HEADS * SUBLANES


def _decode_attend(q8, kn8, vn8, k_page, v_page, n_pages, bias, tri):
    gsz = PAGE_GROUP
    row = lax.broadcasted_iota(jnp.int32, (N_QROWS, GROUP_W), 0)
    lane = lax.broadcasted_iota(jnp.int32, (N_QROWS, GROUP_W), 1)
    own_head = (row // SUBLANES) == (lane // HEAD_DIM)
    qm = jnp.where(own_head, jnp.concatenate([q8] * N_HEADS, axis=0), 0.0).astype(BF16)

    def log_terms(z, nblk, valid):
        l = _softplus2(z)
        if valid is not None:
            l = jnp.where(valid, l, 0.0)
        return jnp.concatenate([_hi_lo(l[:, j * LANES:(j + 1) * LANES]) for j in range(nblk)], axis=0)

    def weights(z, cs2, r, nblk, valid):
        ws = []
        for j in range(nblk):
            blk = cs2[j * N_QROWS:(j + 1) * N_QROWS]
            w = jnp.exp2(z[:, j * LANES:(j + 1) * LANES] - blk[:, :LANES] - r)
            if valid is not None:
                w = jnp.where(valid, w, 0.0)
            r = r + blk[:, LANES:]
            ws.append(w.astype(BF16))
        return jnp.concatenate(ws, axis=1), r

    def group_scores(g):
        kt = jnp.concatenate([k_page(g * gsz + j).astype(BF16) for j in range(gsz)], axis=1)
        z = _dot(qm, kt) + jnp.concatenate([bias] * gsz, axis=1)
        return z, log_terms(z, gsz, None)

    def group_pv(g, w):
        vt = jnp.concatenate([v_page(g * gsz + j).astype(BF16) for j in range(gsz)], axis=1)
        return _dot_nt(w, vt)

    pad = jnp.zeros((PAGE - SUBLANES, GROUP_W), F32)
    kb = jnp.concatenate([kn8, pad], axis=0).astype(BF16)
    vb = jnp.concatenate([vn8, pad], axis=0).astype(BF16)
    r2 = lax.broadcasted_iota(jnp.int32, (N_QROWS, PAGE), 0)
    c2 = lax.broadcasted_iota(jnp.int32, (N_QROWS, PAGE), 1)
    own_valid = c2 < (r2 % SUBLANES)
    n_groups = n_pages // gsz
    live = {}

    def score_stage():
        z_own = _dot_nt(qm, kb) + bias
        live["z"] = [z_own] + [None] * n_groups
        live["hl"] = [log_terms(z_own, 1, own_valid)] + [None] * n_groups
        for g in range(n_groups):
            live["z"][g + 1], live["hl"][g + 1] = group_scores(g)

    def cumsum_stage():
        live["cs2"] = [_dot(hl, tri) for hl in live["hl"]]

    def weight_stage():
        w_own, r = weights(live["z"][0], live["cs2"][0], jnp.zeros((N_QROWS, LANES), F32), 1, own_valid)
        live["w"] = [w_own]
        for g in range(n_groups):
            w, r = weights(live["z"][g + 1], live["cs2"][g + 1], r, gsz, None)
            live["w"].append(w)

    def value_stage():
        acc = _dot(live["w"][0], vb)
        for g in range(n_groups):
            acc = acc + group_pv(g, live["w"][g + 1])
        own = jnp.where(own_head, acc, 0.0)
        o = own[0:SUBLANES]
        for h in range(1, N_HEADS):
            o = o + own[h * SUBLANES:(h + 1) * SUBLANES]
        return o

    return score_stage, cumsum_stage, weight_stage, value_stage


def _hgrn_step_kernel(q_ref, lf_ref, v_ref, sg_ref, s_ref, og_ref, o_ref, sout_ref, f_scr, k_scr, *, n_tok):
    f = jnp.exp(lf_ref[...])
    f_scr[...] = f
    k_scr[...] = 1.0 - f
    tile = (HEAD_DIM, LANES)

    def key_rows(g, outs):
        base = pl.multiple_of(g * SUBLANES, SUBLANES)
        outs = list(outs)
        gates = [[ref[pl.ds(base, SUBLANES), tok * LANES:(tok + 1) * LANES] for ref in (f_scr, k_scr, q_ref)]
                 for tok in range(n_tok)]
        for j in range(SUBLANES):
            s = s_ref[base + j]
            for tok in range(n_tok):
                fb, kb, qb = (jnp.broadcast_to(a[j:j + 1, :], tile) for a in gates[tok])
                s = fb * s + kb * v_ref[:, tok * LANES:(tok + 1) * LANES]
                outs[tok] = outs[tok] + qb * s
            sout_ref[base + j] = s
        return tuple(outs)

    outs = lax.fori_loop(0, HEAD_DIM // SUBLANES, key_rows,
                         tuple(jnp.zeros(tile, F32) for _ in range(n_tok)))
    for tok in range(n_tok):
        cols = slice(tok * LANES, (tok + 1) * LANES)
        o = outs[tok]
        ms = jnp.mean(o * o, axis=0, keepdims=True)
        o_ref[:, cols] = o * lax.rsqrt(ms + EPS) * og_ref[...] * sg_ref[:, cols]


def _hgrn_step(hq_t, lf_t, iv_t, sg_t, state, og_col, n_tok):
    width = n_tok * LANES
    head_rows = pl.BlockSpec((HEAD_DIM, width), lambda h: (h, 0))
    state_spec = pl.BlockSpec((None, HEAD_DIM, HEAD_DIM, LANES), lambda h: (h, 0, 0, 0))
    return pl.pallas_call(
        functools.partial(_hgrn_step_kernel, n_tok=n_tok),
        grid=(N_HEADS,),
        in_specs=[head_rows, head_rows, head_rows, head_rows, state_spec,
                  pl.BlockSpec((HEAD_DIM, LANES), lambda h: (0, 0))],
        out_specs=[head_rows, state_spec],
        out_shape=[jax.ShapeDtypeStruct((GROUP_W, width), F32), jax.ShapeDtypeStruct(state.shape, F32)],
        scratch_shapes=[pltpu.VMEM((HEAD_DIM, width), F32), pltpu.VMEM((HEAD_DIM, width), F32)],
        compiler_params=pltpu.CompilerParams(dimension_semantics=("parallel",),
                                             vmem_limit_bytes=VMEM_LIMIT),
        name="hgrn_step",
    )(hq_t, lf_t, iv_t, sg_t, state, og_col)


def _hgrn_prompt_kernel(hq_ref, lf_ref, iv_ref, sg_ref, s0_ref, og_ref, bd_ref, ltri_ref,
                        o_ref, sout_ref, s_scr, *, n_chunks):
    i = pl.program_id(1)

    @pl.when(i == 0)
    def _():
        s_scr[...] = s0_ref[...]

    c_len = HG_CHUNK
    gw = HG_GROUP * HEAD_DIM
    n_groups = N_HEADS // HG_GROUP
    ltri = ltri_ref[...]
    r4 = lax.broadcasted_iota(jnp.int32, (gw, gw), 0)
    c4 = lax.broadcasted_iota(jnp.int32, (gw, gw), 1)
    same_head = (r4 // HEAD_DIM) == (c4 // HEAD_DIM)
    rt = lax.broadcasted_iota(jnp.int32, (c_len, gw), 0)
    ct = lax.broadcasted_iota(jnp.int32, (c_len, gw), 1)
    causal = (ct % c_len) <= rt

    def block_diag(a):
        return jnp.where(same_head, jnp.concatenate([a] * HG_GROUP, axis=0), 0.0).astype(BF16)

    chunk_rows = [slice(c * c_len, (c + 1) * c_len) for c in range(n_chunks)]
    groups = [slice(g * gw, (g + 1) * gw) for g in range(n_groups)]
    gates = []
    for rows in chunk_rows:
        lf = lf_ref[rows, :]
        b = sum(_dot(ltri, part) for part in _split_bf16(lf, 3))
        b_last = b[c_len - 1:c_len, :]
        hk = 1.0 - jnp.exp(lf)
        qd = (hq_ref[rows, :].astype(F32) * jnp.exp(b)).astype(BF16)
        gates.append((qd, hk * jnp.exp(-b), (hk * jnp.exp(b_last - b)).astype(BF16), iv_ref[rows, :],
                      jnp.exp(b_last)))
    atts = [[jnp.where(causal, _dot_nt(qd[:, sl], block_diag(kd[:, sl])), 0.0).astype(BF16) for sl in groups]
            for qd, kd, _, _, _ in gates]
    deltas = [[jnp.where(same_head, _dot_tn(v[:, sl], kk[:, sl]), 0.0) for sl in groups]
              for _, _, kk, v, _ in gates]
    states = [s_scr[g] for g in range(n_groups)]
    outs = []
    for c in range(n_chunks):
        qd, _, _, v, decay = gates[c]
        outs.append(jnp.concatenate(
            [_dot(atts[c][g], block_diag(v[:, sl].astype(F32))) + _dot_nt(qd[:, sl], states[g].astype(BF16))
             for g, sl in enumerate(groups)], axis=1))
        states = [states[g] * decay[:, sl] + deltas[c][g] for g, sl in enumerate(groups)]
    o = jnp.concatenate(outs, axis=0)
    gated = _head_norm(o, og_ref[...], bd_ref[...]) * sg_ref[...].astype(F32)
    o_ref[...] = gated.astype(o_ref.dtype)
    for g in range(n_groups):
        s_scr[g] = states[g]

    @pl.when(i == pl.num_programs(1) - 1)
    def _():
        sout_ref[...] = s_scr[...]


def _hgrn_prompt(hq, lf, iv, sg, s0_bd, og_t, bd, bx, r, tc):
    shp = (bx, r, GROUP_W)
    gw = HG_GROUP * HEAD_DIM
    tok_spec = pl.BlockSpec((None, tc, GROUP_W), lambda b, i: (b, i, 0))
    state_spec = pl.BlockSpec((None, N_HEADS // HG_GROUP, gw, gw), lambda b, i: (b, 0, 0, 0))
    const = lambda shape: pl.BlockSpec(shape, lambda b, i: (0,) * len(shape))
    ltri = (jnp.arange(HG_CHUNK)[None, :] <= jnp.arange(HG_CHUNK)[:, None]).astype(BF16)
    return pl.pallas_call(
        functools.partial(_hgrn_prompt_kernel, n_chunks=tc // HG_CHUNK),
        grid=(bx, r // tc),
        in_specs=[tok_spec, tok_spec, tok_spec, tok_spec, state_spec,
                  const((1, GROUP_W)), const((2 * LANES, 2 * LANES)), const((HG_CHUNK, HG_CHUNK))],
        out_specs=[tok_spec, state_spec],
        out_shape=[jax.ShapeDtypeStruct(shp, BF16), jax.ShapeDtypeStruct(s0_bd.shape, F32)],
        scratch_shapes=[pltpu.VMEM(s0_bd.shape[1:], F32)],
        compiler_params=pltpu.CompilerParams(dimension_semantics=("parallel", "arbitrary"),
                                             vmem_limit_bytes=VMEM_LIMIT),
        name="hgrn_prompt",
    )(hq.reshape(shp), lf.reshape(shp), iv.reshape(shp), sg.reshape(shp), s0_bd, og_t, bd, ltri)


def _tail_compute(x_ref, oa_ref, ob_ref, g1_ref, sc_ref, sh_ref, g2_ref, n2_ref, wo_ref, wu_ref, wd_ref,
                  o_ref, acc_ref, ob_channel_major, side_work=None):
    wo_b = wo_ref[GROUP_W:2 * GROUP_W, :]
    ob = ob_ref[...].astype(BF16)
    mix = _dot(oa_ref[...].astype(BF16), wo_ref[0:GROUP_W, :])
    mix = mix + (_dot_tn(ob, wo_b) if ob_channel_major else _dot(ob, wo_b))
    x1 = x_ref[...] + g1_ref[...] * mix
    o_ref[...] = x1
    h2 = x1 * lax.rsqrt(jnp.mean(x1 * x1, axis=-1, keepdims=True) + EPS) * n2_ref[...]
    h2 = (h2 * (1.0 + sc_ref[...]) + sh_ref[...]).astype(BF16)
    for j in range(MLP_SLICES):
        cols = slice(j * MLP_TF, (j + 1) * MLP_TF)
        if side_work is not None:
            side_work(j, 0)
        a = jnp.square(jnp.maximum(_dot(h2, wu_ref[:, cols]), 0.0)).astype(BF16)
        if side_work is not None:
            side_work(j, 1)
        part = _dot(a, wd_ref[cols, :])
        if j == 0:
            acc_ref[...] = part
        else:
            acc_ref[...] += part
    o_ref[...] += g2_ref[...] * acc_ref[...]


def _tail_kernel(*refs, ob_channel_major):
    _tail_compute(*refs, ob_channel_major=ob_channel_major)


def _tail_specs(mods, tm, ob_channel_major):
    row_spec = pl.BlockSpec((None, tm, D_MODEL), lambda b, i, *_: (b, i, 0))
    half_spec = pl.BlockSpec((None, tm, GROUP_W), lambda b, i, *_: (b, i, 0))
    ob_spec = pl.BlockSpec((None, GROUP_W, tm), lambda b, i, *_: (b, 0, i)) if ob_channel_major else half_spec
    in_specs = [row_spec, half_spec, ob_spec,
                _mod_spec(mods, ADA_G1, tm), _mod_spec(mods, ADA_SC2, tm), _mod_spec(mods, ADA_SH2, tm),
                _mod_spec(mods, ADA_G2, tm), _resident((1, D_MODEL)),
                _resident((2 * GROUP_W, D_MODEL)), _resident((D_MODEL, D_FF)), _resident((D_FF, D_MODEL))]
    return in_specs, row_spec


def _tail(x3, oa, ob, mods, n2, w_out_bf, w_up_bf, w_down_bf, tm, ob_channel_major):
    bx, r, _ = x3.shape
    in_specs, row_spec = _tail_specs(mods, tm, ob_channel_major)
    return pl.pallas_call(
        functools.partial(_tail_kernel, ob_channel_major=ob_channel_major),
        grid=(bx, r // tm),
        in_specs=in_specs,
        out_specs=row_spec,
        out_shape=jax.ShapeDtypeStruct(x3.shape, F32),
        scratch_shapes=[pltpu.VMEM((tm, D_MODEL), F32)],
        compiler_params=pltpu.CompilerParams(dimension_semantics=("parallel", "parallel"),
                                             vmem_limit_bytes=VMEM_LIMIT),
        name="tail",
    )(x3, oa, ob, mods, mods, mods, mods, n2, w_out_bf, w_up_bf, w_down_bf)


def _tail_decode_kernel(pt_ref, x_ref, oa_ref, ob_ref, g1_ref, sc_ref, sh_ref, g2_ref, n2_ref, wo_ref, wu_ref,
                        wd_ref, q_ref, kn_ref, vn_ref, kc_hbm, vc_hbm, bias_ref, tri_ref,
                        o_ref, oa8_ref, acc_ref, kbuf, vbuf, sems, *, n_pages, seqs_per_step, n_seqs):
    step = pl.program_id(0) * pl.num_programs(1) + pl.program_id(1)

    def page_copies(seq, slot):
        copies = []
        for j in range(n_pages):
            page = pt_ref[seq * n_pages + (n_pages - 1 - j)]
            copies.append(pltpu.make_async_copy(kc_hbm.at[page], kbuf.at[slot, j], sems.at[slot]))
            copies.append(pltpu.make_async_copy(vc_hbm.at[page], vbuf.at[slot, j], sems.at[slot]))
        return copies

    @pl.when(step == 0)
    def _():
        for cp in page_copies(0, 0):
            cp.start()

    stages = {}

    def decode_stage(j, half):
        local, phase = divmod(2 * j + half, 2 * MLP_SLICES // seqs_per_step)
        slot = local % 2
        if phase == 0:
            seq = step * seqs_per_step + local
            for cp in page_copies(seq, slot):
                cp.wait()

            @pl.when(seq + 1 < n_seqs)
            def _():
                for cp in page_copies(seq + 1, 1 - slot):
                    cp.start()

            stages[local] = _decode_attend(
                q_ref[local], kn_ref[local], vn_ref[local],
                lambda p: kbuf[slot, p], lambda p: vbuf[slot, p], n_pages, bias_ref[...], tri_ref[...])
        if phase < len(stages[local]) - 1:
            stages[local][phase]()
        elif phase == len(stages[local]) - 1:
            oa8_ref[local] = stages[local][phase]()

    _tail_compute(x_ref, oa_ref, ob_ref, g1_ref, sc_ref, sh_ref, g2_ref, n2_ref, wo_ref, wu_ref, wd_ref,
                  o_ref, acc_ref, ob_channel_major=False, side_work=decode_stage)


def _tail_decode(x3, oa, ob, mods, n2, w_out_bf, w_up_bf, w_down_bf, tm,
                 q8, kn8, vn8, kc_cm, vc_cm, page_table, bias_bc, tri2):
    bx, r, _ = x3.shape
    n_seqs, n_pages = page_table.shape
    n_steps = bx * (r // tm)
    seqs_per_step = n_seqs // n_steps
    assert seqs_per_step * n_steps == n_seqs and seqs_per_step % 2 == 0
    assert 2 * MLP_SLICES == 4 * seqs_per_step, "one attention stage (of 4) per MLP matmul"
    in_specs, row_spec = _tail_specs(mods, tm, ob_channel_major=False)
    nt = r // tm
    seq_spec = pl.BlockSpec((seqs_per_step, SUBLANES, GROUP_W), lambda b, i, pt: (b * nt + i, 0, 0))
    page_buf = pltpu.VMEM((2, n_pages, GROUP_W, PAGE), F32)
    grid_spec = pltpu.PrefetchScalarGridSpec(
        num_scalar_prefetch=1,
        grid=(bx, nt),
        in_specs=in_specs + [seq_spec, seq_spec, seq_spec,
                             pl.BlockSpec(memory_space=pl.ANY), pl.BlockSpec(memory_space=pl.ANY),
                             _resident((N_QROWS, LANES)), _resident((2 * LANES, 2 * LANES))],
        out_specs=[row_spec, seq_spec],
        scratch_shapes=[pltpu.VMEM((tm, D_MODEL), F32), page_buf, page_buf, pltpu.SemaphoreType.DMA((2,))])
    return pl.pallas_call(
        functools.partial(_tail_decode_kernel, n_pages=n_pages, seqs_per_step=seqs_per_step, n_seqs=n_seqs),
        grid_spec=grid_spec,
        out_shape=[jax.ShapeDtypeStruct(x3.shape, F32),
                   jax.ShapeDtypeStruct((n_seqs, SUBLANES, GROUP_W), F32)],
        compiler_params=pltpu.CompilerParams(dimension_semantics=("arbitrary", "arbitrary"),
                                             vmem_limit_bytes=VMEM_LIMIT_TAIL_DECODE),
        name="tail_decode",
    )(page_table.reshape(-1), x3, oa, ob, mods, mods, mods, mods, n2, w_out_bf, w_up_bf, w_down_bf,
      q8, kn8, vn8, kc_cm, vc_cm, bias_bc, tri2)


def _pad_rows(a, rows):
    return jnp.pad(a, ((0, 0), (0, rows - a.shape[1]), (0, 0)))


def kernel(x_prompt, x_sample, cache_k, cache_v, state_hgrn, page_table, c_prompt, c_sample,
           w_ada, b_ada, norm1_g, norm2_g, w_in, q_norm_g, k_norm_g, sb_bias, hg_lb_logits, hg_out_g,
           w_out, w_up, w_down):
    assert w_ada.shape[0] == 1 and hg_lb_logits.shape[0] == 2, "single-layer step"
    b, t, _ = x_prompt.shape
    db, dt, _ = x_sample.shape
    n_phys = cache_k.shape[1]
    tm_p = 512

    w_in_bf = w_in[0].astype(BF16)
    w_in_t = w_in[0].T.astype(BF16)
    w_out_bf = w_out[0].astype(BF16)
    w_up_bf = w_up[0].astype(BF16)
    w_down_bf = w_down[0].astype(BF16)
    n1 = norm1_g[0].reshape(1, D_MODEL)
    n2 = norm2_g[0].reshape(1, D_MODEL)
    qg_t = jnp.tile(q_norm_g[0], N_HEADS).reshape(1, GROUP_W)
    kg_t = jnp.tile(k_norm_g[0], N_HEADS).reshape(1, GROUP_W)
    kg_cm = jnp.broadcast_to(kg_t.reshape(GROUP_W, 1), (GROUP_W, tm_p))
    og_t = jnp.tile(hg_out_g[0], N_HEADS).reshape(1, GROUP_W)
    bias2 = sb_bias[0].astype(F32) * LOG2E
    bias_bc = jnp.broadcast_to(jnp.repeat(bias2, SUBLANES)[:, None], (N_QROWS, LANES))
    idx = jnp.arange(2 * LANES)
    bd = (idx[:, None] // HEAD_DIM == idx[None, :] // HEAD_DIM).astype(BF16)
    kk = jnp.arange(LANES)
    tri = jnp.concatenate([(kk[:, None] >= kk[None, :]).astype(BF16), jnp.ones((LANES, LANES), BF16)], axis=1)
    tri2 = jnp.concatenate([tri, tri], axis=0)

    ada = _ada(jnp.concatenate([c_prompt, c_sample], axis=0), w_ada[0], b_ada[0])
    mods_p = ada[:b].reshape(b, 1, N_ADA * D_MODEL)
    mods_s = jnp.tile(ada[b:], (dt, 1)).reshape(1, dt * db, N_ADA * D_MODEL)

    q, kt_p, vt_p, hq, lf, iv, sg = _proj(x_prompt, mods_p, n1, w_in_bf, w_in_t, qg_t, kg_cm,
                                           hg_lb_logits, bd, tm=tm_p)
    oa = _sb_prompt(q, kt_p, vt_p, bias2, tri2, b, t)
    n_hg = N_HEADS // HG_GROUP
    gw = HG_GROUP * HEAD_DIM
    ob, s_bd = _hgrn_prompt(hq, lf, iv, sg, jnp.zeros((b, n_hg, gw, gw), F32), og_t, bd, b, t, tc=512)
    s_bd = s_bd.reshape(b, n_hg, HG_GROUP, HEAD_DIM, HG_GROUP, HEAD_DIM)
    s_p = jnp.stack([s_bd[:, :, j, :, j, :] for j in range(HG_GROUP)], axis=2)
    s_p = jnp.swapaxes(s_p, -1, -2).reshape(b, N_HEADS, HEAD_DIM, HEAD_DIM)
    oa_p, ob_p = oa.reshape(b, t, GROUP_W), ob

    assert db == LANES, "decode batch fills the lane dimension"
    ns = dt * db
    xs3 = jnp.transpose(x_sample, (1, 0, 2)).reshape(1, ns, D_MODEL)
    q, k_s, v_s, kt_s, vt_s, hq_t, lf_t, iv_t, sg_t = _proj_decode(
        xs3, mods_s, n1, w_in_bf, w_in_t, qg_t, kg_t, kg_cm, hg_lb_logits.T, bd, n_tok=dt)
    seq = lambda a: _pad_rows(jnp.transpose(a.reshape(dt, db, GROUP_W), (1, 0, 2)), SUBLANES)
    cm = lambda c: jnp.transpose(c[0], (0, 2, 3, 1)).reshape(n_phys, GROUP_W, PAGE)
    y_p, oa8 = _tail_decode(x_prompt, oa_p, ob_p, mods_p, n2, w_out_bf, w_up_bf, w_down_bf, 512,
                            seq(q), seq(k_s), seq(v_s), cm(cache_k), cm(cache_v), page_table, bias_bc, tri2)
    oa = jnp.transpose(oa8[:, :dt], (1, 0, 2)).reshape(1, ns, GROUP_W)
    og_col = jnp.broadcast_to(hg_out_g[0][:, None], (HEAD_DIM, LANES))
    state_cm = jnp.transpose(state_hgrn[0], (1, 2, 3, 0))
    ob_t, s_s = _hgrn_step(hq_t, lf_t, iv_t, sg_t, state_cm, og_col, n_tok=dt)
    y_s = _tail(xs3, oa, ob_t[None], mods_s, n2, w_out_bf, w_up_bf, w_down_bf, tm=ns, ob_channel_major=True)
    y_s = jnp.transpose(y_s.reshape(dt, db, D_MODEL), (1, 0, 2))

    heads_cm = lambda a: jnp.transpose(a.reshape(b, N_HEADS, HEAD_DIM, t), (0, 3, 1, 2))[None]
    heads_tok = lambda a: jnp.transpose(a.reshape(dt, N_HEADS, HEAD_DIM, db), (3, 0, 1, 2))[None]
    return (y_p, y_s, heads_cm(kt_p), heads_cm(vt_p), heads_tok(kt_s), heads_tok(vt_s),
            s_p[None], jnp.transpose(s_s, (3, 0, 1, 2))[None])
```

```python
import functools
import math

import jax
import jax.numpy as jnp
from jax import lax
from jax.experimental import pallas as pl
from jax.experimental.pallas import tpu as pltpu

F32 = jnp.float32
BF16 = jnp.bfloat16

D_MODEL = 1024
N_HEADS = 8
HEAD_DIM = 64
GROUP_W = N_HEADS * HEAD_DIM
N_GROUPS = 7
D_FF = 4 * D_MODEL
N_ADA = 6
ADA_SH1, ADA_SC1, ADA_G1, ADA_SH2, ADA_SC2, ADA_G2 = range(N_ADA)
EPS = 1e-6
SB_SCALE = HEAD_DIM ** -0.5
LOG2E = math.log2(math.e)
HG_CHUNK = 64
PAGE = 128
LANES = 128
SUBLANES = 8
VMEM_LIMIT = 48 * 1024 * 1024
PAGE_GROUP = 4
HG_GROUP = 4
MLP_TF = 512
MLP_SLICES = D_FF // MLP_TF
VMEM_LIMIT_TAIL_DECODE = 56 * 1024 * 1024


def _dot(a, b):
    return jnp.dot(a, b, preferred_element_type=F32)


def _dot_nt(a, b):
    return lax.dot_general(a, b, (((1,), (1,)), ((), ())), preferred_element_type=F32)


def _dot_tn(a, b):
    return lax.dot_general(a, b, (((0,), (0,)), ((), ())), preferred_element_type=F32)


def _split_bf16(x, n):
    parts = []
    r = x
    for i in range(n):
        p = r.astype(BF16)
        parts.append(p)
        if i + 1 < n:
            r = r - p.astype(F32)
    return parts


def _silu(x):
    return x * jax.nn.sigmoid(x)


def _head_norm(y, gain, bd):
    hi, lo = _split_bf16(y * y, 2)
    half = 2 * LANES
    ss = jnp.concatenate(
        [_dot(hi[:, c:c + half], bd) + _dot(lo[:, c:c + half], bd) for c in range(0, GROUP_W, half)],
        axis=1)
    return y * lax.rsqrt(ss * (1.0 / HEAD_DIM) + EPS) * gain


def _head_norm_cm(yt, gain, bd):
    hi, lo = _split_bf16(yt * yt, 2)
    half = 2 * LANES
    ss = jnp.concatenate(
        [_dot(bd, hi[c:c + half]) + _dot(bd, lo[c:c + half]) for c in range(0, GROUP_W, half)],
        axis=0)
    return yt * lax.rsqrt(ss * (1.0 / HEAD_DIM) + EPS) * gain


def _softplus2(z2):
    return jnp.maximum(z2, 0.0) + jnp.log(1.0 + jnp.exp2(-jnp.abs(z2))) * LOG2E


def _hi_lo(x):
    hi, lo = _split_bf16(x, 2)
    return jnp.concatenate([hi, lo], axis=1)


def _ada_kernel(c_ref, w_ref, b_ref, o_ref):
    s = _silu(c_ref[...]).astype(BF16)
    o_ref[...] = _dot(s, w_ref[...].astype(BF16)) + b_ref[...]


def _ada(c_all, w_ada, b_ada):
    m = c_all.shape[0]
    n = w_ada.shape[1]
    tn = 1024
    return pl.pallas_call(
        _ada_kernel,
        grid=(n // tn,),
        in_specs=[pl.BlockSpec((m, D_MODEL), lambda j: (0, 0)),
                  pl.BlockSpec((D_MODEL, tn), lambda j: (0, j)),
                  pl.BlockSpec((1, tn), lambda j: (0, j))],
        out_specs=pl.BlockSpec((m, tn), lambda j: (0, j)),
        out_shape=jax.ShapeDtypeStruct((m, n), F32),
        compiler_params=pltpu.CompilerParams(dimension_semantics=("parallel",),
                                             vmem_limit_bytes=VMEM_LIMIT),
        name="ada",
    )(c_all, w_ada, b_ada.reshape(1, n))


def _norm1(x_ref, sc_ref, sh_ref, n1_ref):
    x = x_ref[...]
    h = x * lax.rsqrt(jnp.mean(x * x, axis=-1, keepdims=True) + EPS) * n1_ref[...]
    return (h * (1.0 + sc_ref[...]) + sh_ref[...]).astype(BF16)


def _forget_gate_log(lb, y):
    return jnp.log(lb + (1.0 - lb) * jax.nn.sigmoid(y))


def _proj_kernel(x_ref, sc_ref, sh_ref, n1_ref, w_ref, wt_ref, qg_ref, kgc_ref, lbl_ref, bd_ref,
                 q_ref, kt_ref, vt_ref, hq_ref, lf_ref, i_ref, g_ref):
    hb = _norm1(x_ref, sc_ref, sh_ref, n1_ref)
    bd = bd_ref[...]

    def group(g):
        return _dot(hb, w_ref[:, g * GROUP_W:(g + 1) * GROUP_W])

    def group_cm(g):
        return _dot_nt(wt_ref[g * GROUP_W:(g + 1) * GROUP_W, :], hb)

    yq = group(0)
    ykt = group_cm(1)
    vt_ref[...] = group_cm(2)
    hq_ref[...] = _silu(group(3)).astype(BF16)
    q_ref[...] = (_head_norm(yq, qg_ref[...], bd) * (SB_SCALE * LOG2E)).astype(BF16)
    lbl = lbl_ref[...]
    e = jnp.exp(lbl - jnp.max(lbl, axis=0, keepdims=True))
    lb = e[0:1, :] / jnp.sum(e, axis=0, keepdims=True)
    lf_ref[...] = _forget_gate_log(lb, group(4))
    kt_ref[...] = _head_norm_cm(ykt, kgc_ref[...], bd)
    i_ref[...] = group(5).astype(BF16)
    g_ref[...] = _silu(group(6)).astype(BF16)


def _proj_decode_kernel(x_ref, sc_ref, sh_ref, n1_ref, w_ref, wt_ref, qg_ref, kg_ref, kgc_ref, lblt_ref, bd_ref,
                        q_ref, k_ref, v_ref, kt_ref, vt_ref, hq_ref, lf_ref, i_ref, g_ref):
    hb = _norm1(x_ref, sc_ref, sh_ref, n1_ref)
    bd = bd_ref[...]

    def group(g):
        return _dot(hb, w_ref[:, g * GROUP_W:(g + 1) * GROUP_W])

    def group_cm(g):
        return _dot_nt(wt_ref[g * GROUP_W:(g + 1) * GROUP_W, :], hb)

    q_ref[...] = _head_norm(group(0), qg_ref[...], bd) * (SB_SCALE * LOG2E)
    k_ref[...] = _head_norm(group(1), kg_ref[...], bd)
    v_ref[...] = group(2)
    kt = _head_norm_cm(group_cm(1), kgc_ref[...], bd)
    vt = group_cm(2)
    for tok in range(kt_ref.shape[0]):
        kt_ref[tok] = kt[:, tok * LANES:(tok + 1) * LANES]
        vt_ref[tok] = vt[:, tok * LANES:(tok + 1) * LANES]
    hq_ref[...] = _silu(group_cm(3))
    lblt = lblt_ref[...]
    l0, l1 = lblt[:, 0:1], lblt[:, 1:2]
    m = jnp.maximum(l0, l1)
    e0, e1 = jnp.exp(l0 - m), jnp.exp(l1 - m)
    lf_ref[...] = _forget_gate_log(e0 / (e0 + e1), group_cm(4))
    i_ref[...] = group_cm(5)
    g_ref[...] = _silu(group_cm(6))


def _mod_spec(mods, m, tm):
    if mods.shape[1] == 1:
        return pl.BlockSpec((None, 1, D_MODEL), lambda b, i, *_: (b, 0, m))
    return pl.BlockSpec((None, tm, D_MODEL), lambda b, i, *_: (b, i, m))


def _resident(shape):
    return pl.BlockSpec(shape, lambda *_: (0,) * len(shape), pipeline_mode=pl.Buffered(1))


def _proj(x3, mods, n1, w_in_bf, w_in_t, qg_t, kg_cm, lb_logits, bd, tm):
    bx, r, _ = x3.shape
    nt = r // tm
    row_spec = pl.BlockSpec((tm, GROUP_W), lambda b, i: (b * nt + i, 0))
    row_shape = lambda dt: jax.ShapeDtypeStruct((bx * r, GROUP_W), dt)
    cm_spec = pl.BlockSpec((None, GROUP_W, tm), lambda b, i: (b, 0, i))
    cm_shape = jax.ShapeDtypeStruct((bx, GROUP_W, r), F32)
    return pl.pallas_call(
        _proj_kernel,
        grid=(bx, nt),
        in_specs=[pl.BlockSpec((None, tm, D_MODEL), lambda b, i: (b, i, 0)),
                  _mod_spec(mods, ADA_SC1, tm), _mod_spec(mods, ADA_SH1, tm),
                  _resident((1, D_MODEL)), _resident(w_in_bf.shape), _resident(w_in_t.shape),
                  _resident((1, GROUP_W)), _resident(kg_cm.shape), _resident(lb_logits.shape),
                  _resident((2 * LANES, 2 * LANES))],
        out_specs=[row_spec, cm_spec, cm_spec, row_spec, row_spec, row_spec, row_spec],
        out_shape=[row_shape(BF16), cm_shape, cm_shape, row_shape(BF16), row_shape(F32),
                   row_shape(BF16), row_shape(BF16)],
        compiler_params=pltpu.CompilerParams(dimension_semantics=("parallel", "parallel"),
                                             vmem_limit_bytes=VMEM_LIMIT),
        name="proj",
    )(x3, mods, mods, n1, w_in_bf, w_in_t, qg_t, kg_cm, lb_logits, bd)


def _proj_decode(x3, mods, n1, w_in_bf, w_in_t, qg_t, kg_t, kg_cm, lb_logits_t, bd, n_tok):
    _, r, _ = x3.shape
    nseq = r // n_tok
    row_spec = pl.BlockSpec((r, GROUP_W), lambda b, i: (0, 0))
    row_shape = jax.ShapeDtypeStruct((r, GROUP_W), F32)
    tok_spec = pl.BlockSpec((n_tok, GROUP_W, nseq), lambda b, i: (0, 0, 0))
    tok_shape = jax.ShapeDtypeStruct((n_tok, GROUP_W, nseq), F32)
    cm_spec = pl.BlockSpec((GROUP_W, r), lambda b, i: (0, 0))
    cm_shape = jax.ShapeDtypeStruct((GROUP_W, r), F32)
    return pl.pallas_call(
        _proj_decode_kernel,
        grid=(1, 1),
        in_specs=[pl.BlockSpec((None, r, D_MODEL), lambda b, i: (b, i, 0)),
                  _mod_spec(mods, ADA_SC1, r), _mod_spec(mods, ADA_SH1, r),
                  _resident((1, D_MODEL)), _resident(w_in_bf.shape), _resident(w_in_t.shape),
                  _resident((1, GROUP_W)), _resident((1, GROUP_W)), _resident(kg_cm.shape),
                  _resident(lb_logits_t.shape), _resident((2 * LANES, 2 * LANES))],
        out_specs=[row_spec, row_spec, row_spec, tok_spec, tok_spec, cm_spec, cm_spec, cm_spec, cm_spec],
        out_shape=[row_shape, row_shape, row_shape, tok_shape, tok_shape,
                   cm_shape, cm_shape, cm_shape, cm_shape],
        compiler_params=pltpu.CompilerParams(dimension_semantics=("parallel", "parallel"),
                                             vmem_limit_bytes=VMEM_LIMIT),
        name="proj_decode",
    )(x3, mods, mods, n1, w_in_bf, w_in_t, qg_t, kg_t, kg_cm, lb_logits_t, bd)


def _sb_prompt_kernel(bias_ref, q_ref, qe_ref, kt_ref, vt_ref, tri_ref, o_ref,
                      qx_ref, r_ref, acc_ref, z0_ref, hl0_ref, w0_ref, z1_ref, hl1_ref, w1_ref):
    p = pl.program_id(1)
    t = q_ref.shape[0]
    n_blocks = t // LANES
    lane2 = lax.broadcasted_iota(jnp.int32, (1, 2 * LANES), 1)
    bias_row = jnp.where(lane2 < LANES, bias_ref[2 * p], bias_ref[2 * p + 1])
    brow = lax.broadcasted_iota(jnp.int32, (LANES, 2 * LANES), 0)
    bias_blk = jnp.zeros((LANES, 2 * LANES), F32)
    for n, part in enumerate(_split_bf16(bias_row, 3)):
        bias_blk = jnp.where(brow == n, part.astype(F32), bias_blk)
    bias_blk = bias_blk.astype(BF16)
    zeros_half = jnp.zeros((HEAD_DIM, LANES), F32)
    qx_ref[:, :LANES] = q_ref[...]
    qx_ref[:, LANES:] = qe_ref[...]
    row = lax.broadcasted_iota(jnp.int32, (LANES, LANES), 0)
    col = lax.broadcasted_iota(jnp.int32, (LANES, LANES), 1)
    strictly_before = col < row

    def stacked(ref, c):
        blk = ref[:, c * LANES:(c + 1) * LANES]
        top = jnp.concatenate([blk[:HEAD_DIM], zeros_half], axis=0)
        bot = jnp.concatenate([zeros_half, blk[HEAD_DIM:]], axis=0)
        return jnp.concatenate([top, bot], axis=1).astype(BF16)

    def diag_masked(x):
        top = jnp.where(strictly_before, x[:LANES], 0.0)
        return top if x.shape[0] == LANES else jnp.concatenate([top, x[LANES:]], axis=0)

    def scores(c, z_ref, hl_ref):
        rows = slice(c * LANES, t)
        z = _dot(qx_ref[rows, :], jnp.concatenate([stacked(kt_ref, c), bias_blk], axis=0))
        l = _softplus2(z)
        z_ref[rows, :] = z
        for hh in range(2):
            lh = diag_masked(l[:, hh * LANES:(hh + 1) * LANES])
            hl_ref[rows, hh * LANES:(hh + 1) * LANES] = lh.astype(BF16)

    def tails(c, z_ref, hl_ref, w_ref):
        rows = slice(c * LANES, t)
        cs = _dot(hl_ref[rows, :], tri_ref[...])
        for hh in range(2):
            cs_h = cs[:, hh * LANES:(hh + 1) * LANES]
            w = jnp.exp2(z_ref[rows, hh * LANES:(hh + 1) * LANES] - cs_h - r_ref[hh, rows, :])
            r_ref[hh, rows, :] += jnp.broadcast_to(cs_h[:, 0:1], cs_h.shape)
            w_ref[rows, hh * LANES:(hh + 1) * LANES] = diag_masked(w).astype(BF16)

    def values(c, w_ref):
        rows = slice(c * LANES, t)
        acc_ref[rows, :] += _dot_nt(w_ref[rows, :], stacked(vt_ref, c))

    r_ref[...] = jnp.zeros_like(r_ref)
    acc_ref[...] = jnp.zeros_like(acc_ref)
    slots = ((z0_ref, hl0_ref, w0_ref), (z1_ref, hl1_ref, w1_ref))
    order = list(reversed(range(n_blocks)))
    for step in range(n_blocks + 2):
        if step < n_blocks:
            z_ref, hl_ref, _ = slots[step % 2]
            scores(order[step], z_ref, hl_ref)
        if 1 <= step <= n_blocks:
            tails(order[step - 1], *slots[(step - 1) % 2])
        if step >= 2:
            values(order[step - 2], slots[step % 2][2])
    o_ref[...] = acc_ref[...].astype(BF16)


def _sb_prompt(q, kt, vt, bias2, tri_pair, b, t):
    q3 = q.reshape(b, t, GROUP_W)
    pair_cm = pl.BlockSpec((None, LANES, t), lambda bb, p: (bb, p, 0))
    tile = pl.BlockSpec((None, t, LANES), lambda bb, p: (bb, 0, p))
    n_bias_cols = 3
    qe = jnp.broadcast_to((jnp.arange(LANES) < n_bias_cols).astype(BF16)[None, :], (t, LANES))
    stage_bufs = [pltpu.VMEM((t, 2 * LANES), F32), pltpu.VMEM((t, 2 * LANES), BF16),
                  pltpu.VMEM((t, 2 * LANES), BF16)]
    out = pl.pallas_call(
        _sb_prompt_kernel,
        grid=(b, N_HEADS // 2),
        in_specs=[pl.BlockSpec(memory_space=pltpu.SMEM), tile, _resident((t, LANES)), pair_cm, pair_cm,
                  _resident((2 * LANES, 2 * LANES))],
        out_specs=tile,
        out_shape=jax.ShapeDtypeStruct((b, t, GROUP_W), BF16),
        scratch_shapes=[pltpu.VMEM((t, 2 * LANES), BF16),
                        pltpu.VMEM((2, t, LANES), F32), pltpu.VMEM((t, LANES), F32)] + stage_bufs + stage_bufs,
        compiler_params=pltpu.CompilerParams(dimension_semantics=("parallel", "parallel"),
                                             vmem_limit_bytes=VMEM_LIMIT),
        name="sb_prompt",
    )(bias2, q3, qe, kt, vt, tri_pair)
    return out.reshape(b * t, GROUP_W)


N_QROWS = N_HEADS * SUBLANES


def _decode_attend(q8, kn8, vn8, k_page, v_page, n_pages, bias, tri):
    gsz = PAGE_GROUP
    row = lax.broadcasted_iota(jnp.int32, (N_QROWS, GROUP_W), 0)
    lane = lax.broadcasted_iota(jnp.int32, (N_QROWS, GROUP_W), 1)
    own_head = (row // SUBLANES) == (lane // HEAD_DIM)
    qm = jnp.where(own_head, jnp.concatenate([q8] * N_HEADS, axis=0), 0.0).astype(BF16)

    def log_terms(z, nblk, valid):
        l = _softplus2(z)
        if valid is not None:
            l = jnp.where(valid, l, 0.0)
        return jnp.concatenate([_hi_lo(l[:, j * LANES:(j + 1) * LANES]) for j in range(nblk)], axis=0)

    def weights(z, cs2, r, nblk, valid):
        ws = []
        for j in range(nblk):
            blk = cs2[j * N_QROWS:(j + 1) * N_QROWS]
            w = jnp.exp2(z[:, j * LANES:(j + 1) * LANES] - blk[:, :LANES] - r)
            if valid is not None:
                w = jnp.where(valid, w, 0.0)
            r = r + blk[:, LANES:]
            ws.append(w.astype(BF16))
        return jnp.concatenate(ws, axis=1), r

    def group_scores(g):
        kt = jnp.concatenate([k_page(g * gsz + j).astype(BF16) for j in range(gsz)], axis=1)
        z = _dot(qm, kt) + jnp.concatenate([bias] * gsz, axis=1)
        return z, log_terms(z, gsz, None)

    def group_pv(g, w):
        vt = jnp.concatenate([v_page(g * gsz + j).astype(BF16) for j in range(gsz)], axis=1)
        return _dot_nt(w, vt)

    pad = jnp.zeros((PAGE - SUBLANES, GROUP_W), F32)
    kb = jnp.concatenate([kn8, pad], axis=0).astype(BF16)
    vb = jnp.concatenate([vn8, pad], axis=0).astype(BF16)
    r2 = lax.broadcasted_iota(jnp.int32, (N_QROWS, PAGE), 0)
    c2 = lax.broadcasted_iota(jnp.int32, (N_QROWS, PAGE), 1)
    own_valid = c2 < (r2 % SUBLANES)
    n_groups = n_pages // gsz
    live = {}

    def score_stage():
        z_own = _dot_nt(qm, kb) + bias
        live["z"] = [z_own] + [None] * n_groups
        live["hl"] = [log_terms(z_own, 1, own_valid)] + [None] * n_groups
        for g in range(n_groups):
            live["z"][g + 1], live["hl"][g + 1] = group_scores(g)

    def cumsum_stage():
        live["cs2"] = [_dot(hl, tri) for hl in live["hl"]]

    def weight_stage():
        w_own, r = weights(live["z"][0], live["cs2"][0], jnp.zeros((N_QROWS, LANES), F32), 1, own_valid)
        live["w"] = [w_own]
        for g in range(n_groups):
            w, r = weights(live["z"][g + 1], live["cs2"][g + 1], r, gsz, None)
            live["w"].append(w)

    def value_stage():
        acc = _dot(live["w"][0], vb)
        for g in range(n_groups):
            acc = acc + group_pv(g, live["w"][g + 1])
        own = jnp.where(own_head, acc, 0.0)
        o = own[0:SUBLANES]
        for h in range(1, N_HEADS):
            o = o + own[h * SUBLANES:(h + 1) * SUBLANES]
        return o

    return score_stage, cumsum_stage, weight_stage, value_stage


def _hgrn_step_kernel(q_ref, lf_ref, v_ref, sg_ref, s_ref, og_ref, o_ref, sout_ref, f_scr, k_scr, *, n_tok):
    f = jnp.exp(lf_ref[...])
    f_scr[...] = f
    k_scr[...] = 1.0 - f
    tile = (HEAD_DIM, LANES)

    def key_rows(g, outs):
        base = pl.multiple_of(g * SUBLANES, SUBLANES)
        outs = list(outs)
        gates = [[ref[pl.ds(base, SUBLANES), tok * LANES:(tok + 1) * LANES] for ref in (f_scr, k_scr, q_ref)]
                 for tok in range(n_tok)]
        for j in range(SUBLANES):
            s = s_ref[base + j]
            for tok in range(n_tok):
                fb, kb, qb = (jnp.broadcast_to(a[j:j + 1, :], tile) for a in gates[tok])
                s = fb * s + kb * v_ref[:, tok * LANES:(tok + 1) * LANES]
                outs[tok] = outs[tok] + qb * s
            sout_ref[base + j] = s
        return tuple(outs)

    outs = lax.fori_loop(0, HEAD_DIM // SUBLANES, key_rows,
                         tuple(jnp.zeros(tile, F32) for _ in range(n_tok)))
    for tok in range(n_tok):
        cols = slice(tok * LANES, (tok + 1) * LANES)
        o = outs[tok]
        ms = jnp.mean(o * o, axis=0, keepdims=True)
        o_ref[:, cols] = o * lax.rsqrt(ms + EPS) * og_ref[...] * sg_ref[:, cols]


def _hgrn_step(hq_t, lf_t, iv_t, sg_t, state, og_col, n_tok):
    width = n_tok * LANES
    head_rows = pl.BlockSpec((HEAD_DIM, width), lambda h: (h, 0))
    state_spec = pl.BlockSpec((None, HEAD_DIM, HEAD_DIM, LANES), lambda h: (h, 0, 0, 0))
    return pl.pallas_call(
        functools.partial(_hgrn_step_kernel, n_tok=n_tok),
        grid=(N_HEADS,),
        in_specs=[head_rows, head_rows, head_rows, head_rows, state_spec,
                  pl.BlockSpec((HEAD_DIM, LANES), lambda h: (0, 0))],
        out_specs=[head_rows, state_spec],
        out_shape=[jax.ShapeDtypeStruct((GROUP_W, width), F32), jax.ShapeDtypeStruct(state.shape, F32)],
        scratch_shapes=[pltpu.VMEM((HEAD_DIM, width), F32), pltpu.VMEM((HEAD_DIM, width), F32)],
        compiler_params=pltpu.CompilerParams(dimension_semantics=("parallel",),
                                             vmem_limit_bytes=VMEM_LIMIT),
        name="hgrn_step",
    )(hq_t, lf_t, iv_t, sg_t, state, og_col)


def _hgrn_prompt_kernel(hq_ref, lf_ref, iv_ref, sg_ref, s0_ref, og_ref, bd_ref, ltri_ref,
                        o_ref, sout_ref, s_scr, *, n_chunks):
    i = pl.program_id(1)

    @pl.when(i == 0)
    def _():
        s_scr[...] = s0_ref[...]

    c_len = HG_CHUNK
    gw = HG_GROUP * HEAD_DIM
    n_groups = N_HEADS // HG_GROUP
    ltri = ltri_ref[...]
    r4 = lax.broadcasted_iota(jnp.int32, (gw, gw), 0)
    c4 = lax.broadcasted_iota(jnp.int32, (gw, gw), 1)
    same_head = (r4 // HEAD_DIM) == (c4 // HEAD_DIM)
    rt = lax.broadcasted_iota(jnp.int32, (c_len, gw), 0)
    ct = lax.broadcasted_iota(jnp.int32, (c_len, gw), 1)
    causal = (ct % c_len) <= rt

    def block_diag(a):
        return jnp.where(same_head, jnp.concatenate([a] * HG_GROUP, axis=0), 0.0).astype(BF16)

    chunk_rows = [slice(c * c_len, (c + 1) * c_len) for c in range(n_chunks)]
    groups = [slice(g * gw, (g + 1) * gw) for g in range(n_groups)]
    gates = []
    for rows in chunk_rows:
        lf = lf_ref[rows, :]
        b = sum(_dot(ltri, part) for part in _split_bf16(lf, 3))
        b_last = b[c_len - 1:c_len, :]
        hk = 1.0 - jnp.exp(lf)
        qd = (hq_ref[rows, :].astype(F32) * jnp.exp(b)).astype(BF16)
        gates.append((qd, hk * jnp.exp(-b), (hk * jnp.exp(b_last - b)).astype(BF16), iv_ref[rows, :],
                      jnp.exp(b_last)))
    atts = [[jnp.where(causal, _dot_nt(qd[:, sl], block_diag(kd[:, sl])), 0.0).astype(BF16) for sl in groups]
            for qd, kd, _, _, _ in gates]
    deltas = [[jnp.where(same_head, _dot_tn(v[:, sl], kk[:, sl]), 0.0) for sl in groups]
              for _, _, kk, v, _ in gates]
    states = [s_scr[g] for g in range(n_groups)]
    outs = []
    for c in range(n_chunks):
        qd, _, _, v, decay = gates[c]
        outs.append(jnp.concatenate(
            [_dot(atts[c][g], block_diag(v[:, sl].astype(F32))) + _dot_nt(qd[:, sl], states[g].astype(BF16))
             for g, sl in enumerate(groups)], axis=1))
        states = [states[g] * decay[:, sl] + deltas[c][g] for g, sl in enumerate(groups)]
    o = jnp.concatenate(outs, axis=0)
    gated = _head_norm(o, og_ref[...], bd_ref[...]) * sg_ref[...].astype(F32)
    o_ref[...] = gated.astype(o_ref.dtype)
    for g in range(n_groups):
        s_scr[g] = states[g]

    @pl.when(i == pl.num_programs(1) - 1)
    def _():
        sout_ref[...] = s_scr[...]


def _hgrn_prompt(hq, lf, iv, sg, s0_bd, og_t, bd, bx, r, tc):
    shp = (bx, r, GROUP_W)
    gw = HG_GROUP * HEAD_DIM
    tok_spec = pl.BlockSpec((None, tc, GROUP_W), lambda b, i: (b, i, 0))
    state_spec = pl.BlockSpec((None, N_HEADS // HG_GROUP, gw, gw), lambda b, i: (b, 0, 0, 0))
    const = lambda shape: pl.BlockSpec(shape, lambda b, i: (0,) * len(shape))
    ltri = (jnp.arange(HG_CHUNK)[None, :] <= jnp.arange(HG_CHUNK)[:, None]).astype(BF16)
    return pl.pallas_call(
        functools.partial(_hgrn_prompt_kernel, n_chunks=tc // HG_CHUNK),
        grid=(bx, r // tc),
        in_specs=[tok_spec, tok_spec, tok_spec, tok_spec, state_spec,
                  const((1, GROUP_W)), const((2 * LANES, 2 * LANES)), const((HG_CHUNK, HG_CHUNK))],
        out_specs=[tok_spec, state_spec],
        out_shape=[jax.ShapeDtypeStruct(shp, BF16), jax.ShapeDtypeStruct(s0_bd.shape, F32)],
        scratch_shapes=[pltpu.VMEM(s0_bd.shape[1:], F32)],
        compiler_params=pltpu.CompilerParams(dimension_semantics=("parallel", "arbitrary"),
                                             vmem_limit_bytes=VMEM_LIMIT),
        name="hgrn_prompt",
    )(hq.reshape(shp), lf.reshape(shp), iv.reshape(shp), sg.reshape(shp), s0_bd, og_t, bd, ltri)


def _tail_compute(x_ref, oa_ref, ob_ref, g1_ref, sc_ref, sh_ref, g2_ref, n2_ref, wo_ref, wu_ref, wd_ref,
                  o_ref, acc_ref, ob_channel_major, side_work=None):
    wo_b = wo_ref[GROUP_W:2 * GROUP_W, :]
    ob = ob_ref[...].astype(BF16)
    mix = _dot(oa_ref[...].astype(BF16), wo_ref[0:GROUP_W, :])
    mix = mix + (_dot_tn(ob, wo_b) if ob_channel_major else _dot(ob, wo_b))
    x1 = x_ref[...] + g1_ref[...] * mix
    o_ref[...] = x1
    h2 = x1 * lax.rsqrt(jnp.mean(x1 * x1, axis=-1, keepdims=True) + EPS) * n2_ref[...]
    h2 = (h2 * (1.0 + sc_ref[...]) + sh_ref[...]).astype(BF16)
    for j in range(MLP_SLICES):
        cols = slice(j * MLP_TF, (j + 1) * MLP_TF)
        if side_work is not None:
            side_work(j, 0)
        a = jnp.square(jnp.maximum(_dot(h2, wu_ref[:, cols]), 0.0)).astype(BF16)
        if side_work is not None:
            side_work(j, 1)
        part = _dot(a, wd_ref[cols, :])
        if j == 0:
            acc_ref[...] = part
        else:
            acc_ref[...] += part
    o_ref[...] += g2_ref[...] * acc_ref[...]


def _tail_kernel(*refs, ob_channel_major):
    _tail_compute(*refs, ob_channel_major=ob_channel_major)


def _tail_specs(mods, tm, ob_channel_major):
    row_spec = pl.BlockSpec((None, tm, D_MODEL), lambda b, i, *_: (b, i, 0))
    half_spec = pl.BlockSpec((None, tm, GROUP_W), lambda b, i, *_: (b, i, 0))
    ob_spec = pl.BlockSpec((None, GROUP_W, tm), lambda b, i, *_: (b, 0, i)) if ob_channel_major else half_spec
    in_specs = [row_spec, half_spec, ob_spec,
                _mod_spec(mods, ADA_G1, tm), _mod_spec(mods, ADA_SC2, tm), _mod_spec(mods, ADA_SH2, tm),
                _mod_spec(mods, ADA_G2, tm), _resident((1, D_MODEL)),
                _resident((2 * GROUP_W, D_MODEL)), _resident((D_MODEL, D_FF)), _resident((D_FF, D_MODEL))]
    return in_specs, row_spec


def _tail(x3, oa, ob, mods, n2, w_out_bf, w_up_bf, w_down_bf, tm, ob_channel_major):
    bx, r, _ = x3.shape
    in_specs, row_spec = _tail_specs(mods, tm, ob_channel_major)
    return pl.pallas_call(
        functools.partial(_tail_kernel, ob_channel_major=ob_channel_major),
        grid=(bx, r // tm),
        in_specs=in_specs,
        out_specs=row_spec,
        out_shape=jax.ShapeDtypeStruct(x3.shape, F32),
        scratch_shapes=[pltpu.VMEM((tm, D_MODEL), F32)],
        compiler_params=pltpu.CompilerParams(dimension_semantics=("parallel", "parallel"),
                                             vmem_limit_bytes=VMEM_LIMIT),
        name="tail",
    )(x3, oa, ob, mods, mods, mods, mods, n2, w_out_bf, w_up_bf, w_down_bf)


def _tail_decode_kernel(pt_ref, x_ref, oa_ref, ob_ref, g1_ref, sc_ref, sh_ref, g2_ref, n2_ref, wo_ref, wu_ref,
                        wd_ref, q_ref, kn_ref, vn_ref, kc_hbm, vc_hbm, bias_ref, tri_ref,
                        o_ref, oa8_ref, acc_ref, kbuf, vbuf, sems, *, n_pages, seqs_per_step, n_seqs):
    step = pl.program_id(0) * pl.num_programs(1) + pl.program_id(1)

    def page_copies(seq, slot):
        copies = []
        for j in range(n_pages):
            page = pt_ref[seq * n_pages + (n_pages - 1 - j)]
            copies.append(pltpu.make_async_copy(kc_hbm.at[page], kbuf.at[slot, j], sems.at[slot]))
            copies.append(pltpu.make_async_copy(vc_hbm.at[page], vbuf.at[slot, j], sems.at[slot]))
        return copies

    @pl.when(step == 0)
    def _():
        for cp in page_copies(0, 0):
            cp.start()

    stages = {}

    def decode_stage(j, half):
        local, phase = divmod(2 * j + half, 2 * MLP_SLICES // seqs_per_step)
        slot = local % 2
        if phase == 0:
            seq = step * seqs_per_step + local
            for cp in page_copies(seq, slot):
                cp.wait()

            @pl.when(seq + 1 < n_seqs)
            def _():
                for cp in page_copies(seq + 1, 1 - slot):
                    cp.start()

            stages[local] = _decode_attend(
                q_ref[local], kn_ref[local], vn_ref[local],
                lambda p: kbuf[slot, p], lambda p: vbuf[slot, p], n_pages, bias_ref[...], tri_ref[...])
        if phase < len(stages[local]) - 1:
            stages[local][phase]()
        elif phase == len(stages[local]) - 1:
            oa8_ref[local] = stages[local][phase]()

    _tail_compute(x_ref, oa_ref, ob_ref, g1_ref, sc_ref, sh_ref, g2_ref, n2_ref, wo_ref, wu_ref, wd_ref,
                  o_ref, acc_ref, ob_channel_major=False, side_work=decode_stage)


def _tail_decode(x3, oa, ob, mods, n2, w_out_bf, w_up_bf, w_down_bf, tm,
                 q8, kn8, vn8, kc_cm, vc_cm, page_table, bias_bc, tri2):
    bx, r, _ = x3.shape
    n_seqs, n_pages = page_table.shape
    n_steps = bx * (r // tm)
    seqs_per_step = n_seqs // n_steps
    assert seqs_per_step * n_steps == n_seqs and seqs_per_step % 2 == 0
    assert 2 * MLP_SLICES == 4 * seqs_per_step, "one attention stage (of 4) per MLP matmul"
    in_specs, row_spec = _tail_specs(mods, tm, ob_channel_major=False)
    nt = r // tm
    seq_spec = pl.BlockSpec((seqs_per_step, SUBLANES, GROUP_W), lambda b, i, pt: (b * nt + i, 0, 0))
    page_buf = pltpu.VMEM((2, n_pages, GROUP_W, PAGE), F32)
    grid_spec = pltpu.PrefetchScalarGridSpec(
        num_scalar_prefetch=1,
        grid=(bx, nt),
        in_specs=in_specs + [seq_spec, seq_spec, seq_spec,
                             pl.BlockSpec(memory_space=pl.ANY), pl.BlockSpec(memory_space=pl.ANY),
                             _resident((N_QROWS, LANES)), _resident((2 * LANES, 2 * LANES))],
        out_specs=[row_spec, seq_spec],
        scratch_shapes=[pltpu.VMEM((tm, D_MODEL), F32), page_buf, page_buf, pltpu.SemaphoreType.DMA((2,))])
    return pl.pallas_call(
        functools.partial(_tail_decode_kernel, n_pages=n_pages, seqs_per_step=seqs_per_step, n_seqs=n_seqs),
        grid_spec=grid_spec,
        out_shape=[jax.ShapeDtypeStruct(x3.shape, F32),
                   jax.ShapeDtypeStruct((n_seqs, SUBLANES, GROUP_W), F32)],
        compiler_params=pltpu.CompilerParams(dimension_semantics=("arbitrary", "arbitrary"),
                                             vmem_limit_bytes=VMEM_LIMIT_TAIL_DECODE),
        name="tail_decode",
    )(page_table.reshape(-1), x3, oa, ob, mods, mods, mods, mods, n2, w_out_bf, w_up_bf, w_down_bf,
      q8, kn8, vn8, kc_cm, vc_cm, bias_bc, tri2)


def _pad_rows(a, rows):
    return jnp.pad(a, ((0, 0), (0, rows - a.shape[1]), (0, 0)))


def kernel(x_prompt, x_sample, cache_k, cache_v, state_hgrn, page_table, c_prompt, c_sample,
           w_ada, b_ada, norm1_g, norm2_g, w_in, q_norm_g, k_norm_g, sb_bias, hg_lb_logits, hg_out_g,
           w_out, w_up, w_down):
    assert w_ada.shape[0] == 1 and hg_lb_logits.shape[0] == 2, "single-layer step"
    b, t, _ = x_prompt.shape
    db, dt, _ = x_sample.shape
    n_phys = cache_k.shape[1]
    tm_p = 512

    w_in_bf = w_in[0].astype(BF16)
    w_in_t = w_in[0].T.astype(BF16)
    w_out_bf = w_out[0].astype(BF16)
    w_up_bf = w_up[0].astype(BF16)
    w_down_bf = w_down[0].astype(BF16)
    n1 = norm1_g[0].reshape(1, D_MODEL)
    n2 = norm2_g[0].reshape(1, D_MODEL)
    qg_t = jnp.tile(q_norm_g[0], N_HEADS).reshape(1, GROUP_W)
    kg_t = jnp.tile(k_norm_g[0], N_HEADS).reshape(1, GROUP_W)
    kg_cm = jnp.broadcast_to(kg_t.reshape(GROUP_W, 1), (GROUP_W, tm_p))
    og_t = jnp.tile(hg_out_g[0], N_HEADS).reshape(1, GROUP_W)
    bias2 = sb_bias[0].astype(F32) * LOG2E
    bias_bc = jnp.broadcast_to(jnp.repeat(bias2, SUBLANES)[:, None], (N_QROWS, LANES))
    idx = jnp.arange(2 * LANES)
    bd = (idx[:, None] // HEAD_DIM == idx[None, :] // HEAD_DIM).astype(BF16)
    kk = jnp.arange(LANES)
    tri = jnp.concatenate([(kk[:, None] >= kk[None, :]).astype(BF16), jnp.ones((LANES, LANES), BF16)], axis=1)
    tri2 = jnp.concatenate([tri, tri], axis=0)

    ada = _ada(jnp.concatenate([c_prompt, c_sample], axis=0), w_ada[0], b_ada[0])
    mods_p = ada[:b].reshape(b, 1, N_ADA * D_MODEL)
    mods_s = jnp.tile(ada[b:], (dt, 1)).reshape(1, dt * db, N_ADA * D_MODEL)

    q, kt_p, vt_p, hq, lf, iv, sg = _proj(x_prompt, mods_p, n1, w_in_bf, w_in_t, qg_t, kg_cm,
                                           hg_lb_logits, bd, tm=tm_p)
    tri_pair = jnp.kron(jnp.eye(2, dtype=BF16), tri[:, :LANES])
    oa = _sb_prompt(q, kt_p, vt_p, bias2, tri_pair, b, t)
    n_hg = N_HEADS // HG_GROUP
    gw = HG_GROUP * HEAD_DIM
    ob, s_bd = _hgrn_prompt(hq, lf, iv, sg, jnp.zeros((b, n_hg, gw, gw), F32), og_t, bd, b, t, tc=1024)
    s_bd = s_bd.reshape(b, n_hg, HG_GROUP, HEAD_DIM, HG_GROUP, HEAD_DIM)
    s_p = jnp.stack([s_bd[:, :, j, :, j, :] for j in range(HG_GROUP)], axis=2)
    s_p = jnp.swapaxes(s_p, -1, -2).reshape(b, N_HEADS, HEAD_DIM, HEAD_DIM)
    oa_p, ob_p = oa.reshape(b, t, GROUP_W), ob

    assert db == LANES, "decode batch fills the lane dimension"
    ns = dt * db
    xs3 = jnp.transpose(x_sample, (1, 0, 2)).reshape(1, ns, D_MODEL)
    q, k_s, v_s, kt_s, vt_s, hq_t, lf_t, iv_t, sg_t = _proj_decode(
        xs3, mods_s, n1, w_in_bf, w_in_t, qg_t, kg_t, kg_cm, hg_lb_logits.T, bd, n_tok=dt)
    seq = lambda a: _pad_rows(jnp.transpose(a.reshape(dt, db, GROUP_W), (1, 0, 2)), SUBLANES)
    cm = lambda c: jnp.transpose(c[0], (0, 2, 3, 1)).reshape(n_phys, GROUP_W, PAGE)
    y_p, oa8 = _tail_decode(x_prompt, oa_p, ob_p, mods_p, n2, w_out_bf, w_up_bf, w_down_bf, 512,
                            seq(q), seq(k_s), seq(v_s), cm(cache_k), cm(cache_v), page_table, bias_bc, tri2)
    oa = jnp.transpose(oa8[:, :dt], (1, 0, 2)).reshape(1, ns, GROUP_W)
    og_col = jnp.broadcast_to(hg_out_g[0][:, None], (HEAD_DIM, LANES))
    state_cm = jnp.transpose(state_hgrn[0], (1, 2, 3, 0))
    ob_t, s_s = _hgrn_step(hq_t, lf_t, iv_t, sg_t, state_cm, og_col, n_tok=dt)
    y_s = _tail(xs3, oa, ob_t[None], mods_s, n2, w_out_bf, w_up_bf, w_down_bf, tm=ns, ob_channel_major=True)
    y_s = jnp.transpose(y_s.reshape(dt, db, D_MODEL), (1, 0, 2))

    heads_cm = lambda a: jnp.transpose(a.reshape(b, N_HEADS, HEAD_DIM, t), (0, 3, 1, 2))[None]
    heads_tok = lambda a: jnp.transpose(a.reshape(dt, N_HEADS, HEAD_DIM, db), (3, 0, 1, 2))[None]
    return (y_p, y_s, heads_cm(kt_p), heads_cm(vt_p), heads_tok(kt_s), heads_tok(vt_s),
            s_p[None], jnp.transpose(s_s, (3, 0, 1, 2))[None])
```

```python
import functools
import math

import jax
import jax.numpy as jnp
from jax import lax
from jax.experimental import pallas as pl
from jax.experimental.pallas import tpu as pltpu

F32 = jnp.float32
BF16 = jnp.bfloat16

D_MODEL = 1024
N_HEADS = 8
HEAD_DIM = 64
GROUP_W = N_HEADS * HEAD_DIM
N_GROUPS = 7
D_FF = 4 * D_MODEL
N_ADA = 6
ADA_SH1, ADA_SC1, ADA_G1, ADA_SH2, ADA_SC2, ADA_G2 = range(N_ADA)
EPS = 1e-6
SB_SCALE = HEAD_DIM ** -0.5
LOG2E = math.log2(math.e)
HG_CHUNK = 64
PAGE = 128
LANES = 128
SUBLANES = 8
VMEM_LIMIT = 48 * 1024 * 1024
PAGE_GROUP = 4
HG_GROUP = 4
MLP_TF = 512
MLP_SLICES = D_FF // MLP_TF
MLP_ROW_SPLIT = 2
MLP_DOTS = 2 * MLP_SLICES * MLP_ROW_SPLIT
VMEM_LIMIT_TAIL_DECODE = 56 * 1024 * 1024


def _dot(a, b):
    return jnp.dot(a, b, preferred_element_type=F32)


def _dot_nt(a, b):
    return lax.dot_general(a, b, (((1,), (1,)), ((), ())), preferred_element_type=F32)


def _dot_tn(a, b):
    return lax.dot_general(a, b, (((0,), (0,)), ((), ())), preferred_element_type=F32)


def _split_bf16(x, n):
    parts = []
    r = x
    for i in range(n):
        p = r.astype(BF16)
        parts.append(p)
        if i + 1 < n:
            r = r - p.astype(F32)
    return parts


def _silu(x):
    return x * jax.nn.sigmoid(x)


def _head_norm(y, gain, bd):
    sq = (y * y).astype(BF16)
    half = 2 * LANES
    ss = jnp.concatenate([_dot(sq[:, c:c + half], bd) for c in range(0, GROUP_W, half)], axis=1)
    return y * lax.rsqrt(ss * (1.0 / HEAD_DIM) + EPS) * gain


def _head_norm_cm(yt, gain, bd):
    sq = (yt * yt).astype(BF16)
    half = 2 * LANES
    ss = jnp.concatenate([_dot(bd, sq[c:c + half]) for c in range(0, GROUP_W, half)], axis=0)
    return yt * lax.rsqrt(ss * (1.0 / HEAD_DIM) + EPS) * gain


def _softplus2(z2):
    return jnp.maximum(z2, 0.0) + jnp.log(1.0 + jnp.exp2(-jnp.abs(z2))) * LOG2E


def _ada_kernel(c_ref, w_ref, b_ref, o_ref):
    s = _silu(c_ref[...]).astype(BF16)
    o_ref[...] = _dot(s, w_ref[...].astype(BF16)) + b_ref[...]


def _ada(c_all, w_ada, b_ada):
    m = c_all.shape[0]
    n = w_ada.shape[1]
    tn = 1024
    return pl.pallas_call(
        _ada_kernel,
        grid=(n // tn,),
        in_specs=[pl.BlockSpec((m, D_MODEL), lambda j: (0, 0)),
                  pl.BlockSpec((D_MODEL, tn), lambda j: (0, j)),
                  pl.BlockSpec((1, tn), lambda j: (0, j))],
        out_specs=pl.BlockSpec((m, tn), lambda j: (0, j)),
        out_shape=jax.ShapeDtypeStruct((m, n), F32),
        compiler_params=pltpu.CompilerParams(dimension_semantics=("parallel",),
                                             vmem_limit_bytes=VMEM_LIMIT),
        name="ada",
    )(c_all, w_ada, b_ada.reshape(1, n))


def _norm1(x_ref, sc_ref, sh_ref, n1_ref):
    x = x_ref[...]
    h = x * lax.rsqrt(jnp.mean(x * x, axis=-1, keepdims=True) + EPS) * n1_ref[...]
    return (h * (1.0 + sc_ref[...]) + sh_ref[...]).astype(BF16)


def _forget_gate_log(lb, y):
    return jnp.log(lb + (1.0 - lb) * jax.nn.sigmoid(y))


def _proj_kernel(x_ref, sc_ref, sh_ref, n1_ref, w_ref, wt_ref, qg_ref, kgc_ref, lbl_ref, bd_ref,
                 q_ref, kt_ref, vt_ref, hq_ref, lf_ref, i_ref, g_ref):
    hb = _norm1(x_ref, sc_ref, sh_ref, n1_ref)
    bd = bd_ref[...]

    def group(g):
        return _dot(hb, w_ref[:, g * GROUP_W:(g + 1) * GROUP_W])

    def group_cm(g):
        return _dot_nt(wt_ref[g * GROUP_W:(g + 1) * GROUP_W, :], hb)

    yq = group(0)
    ykt = group_cm(1)
    vt_ref[...] = group_cm(2)
    hq_ref[...] = _silu(group(3)).astype(BF16)
    q_ref[...] = (_head_norm(yq, qg_ref[...], bd) * (SB_SCALE * LOG2E)).astype(BF16)
    lbl = lbl_ref[...]
    e = jnp.exp(lbl - jnp.max(lbl, axis=0, keepdims=True))
    lb = e[0:1, :] / jnp.sum(e, axis=0, keepdims=True)
    lf_ref[...] = _forget_gate_log(lb, group(4))
    kt_ref[...] = _head_norm_cm(ykt, kgc_ref[...], bd)
    i_ref[...] = group(5).astype(BF16)
    g_ref[...] = _silu(group(6)).astype(BF16)


def _proj_decode_kernel(x_ref, sc_ref, sh_ref, n1_ref, w_ref, wt_ref, qg_ref, kg_ref, kgc_ref, lblt_ref, bd_ref,
                        q_ref, k_ref, v_ref, kt_ref, vt_ref, hq_ref, lf_ref, i_ref, g_ref):
    hb = _norm1(x_ref, sc_ref, sh_ref, n1_ref)
    bd = bd_ref[...]

    def group(g):
        return _dot(hb, w_ref[:, g * GROUP_W:(g + 1) * GROUP_W])

    def group_cm(g):
        return _dot_nt(wt_ref[g * GROUP_W:(g + 1) * GROUP_W, :], hb)

    q_ref[...] = _head_norm(group(0), qg_ref[...], bd) * (SB_SCALE * LOG2E)
    k_ref[...] = _head_norm(group(1), kg_ref[...], bd)
    v_ref[...] = group(2)
    kt = _head_norm_cm(group_cm(1), kgc_ref[...], bd)
    vt = group_cm(2)
    for tok in range(kt_ref.shape[0]):
        kt_ref[tok] = kt[:, tok * LANES:(tok + 1) * LANES]
        vt_ref[tok] = vt[:, tok * LANES:(tok + 1) * LANES]
    hq_ref[...] = _silu(group_cm(3))
    lblt = lblt_ref[...]
    l0, l1 = lblt[:, 0:1], lblt[:, 1:2]
    m = jnp.maximum(l0, l1)
    e0, e1 = jnp.exp(l0 - m), jnp.exp(l1 - m)
    lf_ref[...] = _forget_gate_log(e0 / (e0 + e1), group_cm(4))
    i_ref[...] = group_cm(5)
    g_ref[...] = _silu(group_cm(6))


def _mod_spec(mods, m, tm):
    if mods.shape[1] == 1:
        return pl.BlockSpec((None, 1, D_MODEL), lambda b, i, *_: (b, 0, m))
    return pl.BlockSpec((None, tm, D_MODEL), lambda b, i, *_: (b, i, m))


def _resident(shape):
    return pl.BlockSpec(shape, lambda *_: (0,) * len(shape), pipeline_mode=pl.Buffered(1))


def _proj(x3, mods, n1, w_in_bf, w_in_t, qg_t, kg_cm, lb_logits, bd, tm):
    bx, r, _ = x3.shape
    nt = r // tm
    row_spec = pl.BlockSpec((tm, GROUP_W), lambda b, i: (b * nt + i, 0))
    row_shape = lambda dt: jax.ShapeDtypeStruct((bx * r, GROUP_W), dt)
    cm_spec = pl.BlockSpec((None, GROUP_W, tm), lambda b, i: (b, 0, i))
    cm_shape = jax.ShapeDtypeStruct((bx, GROUP_W, r), F32)
    return pl.pallas_call(
        _proj_kernel,
        grid=(bx, nt),
        in_specs=[pl.BlockSpec((None, tm, D_MODEL), lambda b, i: (b, i, 0)),
                  _mod_spec(mods, ADA_SC1, tm), _mod_spec(mods, ADA_SH1, tm),
                  _resident((1, D_MODEL)), _resident(w_in_bf.shape), _resident(w_in_t.shape),
                  _resident((1, GROUP_W)), _resident(kg_cm.shape), _resident(lb_logits.shape),
                  _resident((2 * LANES, 2 * LANES))],
        out_specs=[row_spec, cm_spec, cm_spec, row_spec, row_spec, row_spec, row_spec],
        out_shape=[row_shape(BF16), cm_shape, cm_shape, row_shape(BF16), row_shape(F32),
                   row_shape(BF16), row_shape(BF16)],
        compiler_params=pltpu.CompilerParams(dimension_semantics=("parallel", "parallel"),
                                             vmem_limit_bytes=VMEM_LIMIT),
        name="proj",
    )(x3, mods, mods, n1, w_in_bf, w_in_t, qg_t, kg_cm, lb_logits, bd)


def _proj_decode(x3, mods, n1, w_in_bf, w_in_t, qg_t, kg_t, kg_cm, lb_logits_t, bd, n_tok):
    _, r, _ = x3.shape
    nseq = r // n_tok
    row_spec = pl.BlockSpec((r, GROUP_W), lambda b, i: (0, 0))
    row_shape = jax.ShapeDtypeStruct((r, GROUP_W), F32)
    tok_spec = pl.BlockSpec((n_tok, GROUP_W, nseq), lambda b, i: (0, 0, 0))
    tok_shape = jax.ShapeDtypeStruct((n_tok, GROUP_W, nseq), F32)
    cm_spec = pl.BlockSpec((GROUP_W, r), lambda b, i: (0, 0))
    cm_shape = jax.ShapeDtypeStruct((GROUP_W, r), F32)
    return pl.pallas_call(
        _proj_decode_kernel,
        grid=(1, 1),
        in_specs=[pl.BlockSpec((None, r, D_MODEL), lambda b, i: (b, i, 0)),
                  _mod_spec(mods, ADA_SC1, r), _mod_spec(mods, ADA_SH1, r),
                  _resident((1, D_MODEL)), _resident(w_in_bf.shape), _resident(w_in_t.shape),
                  _resident((1, GROUP_W)), _resident((1, GROUP_W)), _resident(kg_cm.shape),
                  _resident(lb_logits_t.shape), _resident((2 * LANES, 2 * LANES))],
        out_specs=[row_spec, row_spec, row_spec, tok_spec, tok_spec, cm_spec, cm_spec, cm_spec, cm_spec],
        out_shape=[row_shape, row_shape, row_shape, tok_shape, tok_shape,
                   cm_shape, cm_shape, cm_shape, cm_shape],
        compiler_params=pltpu.CompilerParams(dimension_semantics=("parallel", "parallel"),
                                             vmem_limit_bytes=VMEM_LIMIT),
        name="proj_decode",
    )(x3, mods, mods, n1, w_in_bf, w_in_t, qg_t, kg_t, kg_cm, lb_logits_t, bd)


def _sb_prompt_kernel(bias_ref, q_ref, qe_ref, kt_ref, vt_ref, tri_ref, o_ref,
                      qx_ref, r_ref, acc_ref, z0_ref, hl0_ref, w0_ref, z1_ref, hl1_ref, w1_ref):
    p = pl.program_id(1)
    t = q_ref.shape[0]
    n_blocks = t // LANES
    lane2 = lax.broadcasted_iota(jnp.int32, (1, 2 * LANES), 1)
    bias_row = jnp.where(lane2 < LANES, bias_ref[2 * p], bias_ref[2 * p + 1])
    brow = lax.broadcasted_iota(jnp.int32, (LANES, 2 * LANES), 0)
    bias_blk = jnp.zeros((LANES, 2 * LANES), F32)
    for n, part in enumerate(_split_bf16(bias_row, 3)):
        bias_blk = jnp.where(brow == n, part.astype(F32), bias_blk)
    bias_blk = bias_blk.astype(BF16)
    zeros_half = jnp.zeros((HEAD_DIM, LANES), F32)
    qx_ref[:, :LANES] = q_ref[...]
    qx_ref[:, LANES:] = qe_ref[...]
    row = lax.broadcasted_iota(jnp.int32, (LANES, LANES), 0)
    col = lax.broadcasted_iota(jnp.int32, (LANES, LANES), 1)
    strictly_before = col < row

    def stacked(ref, c):
        blk = ref[:, c * LANES:(c + 1) * LANES]
        top = jnp.concatenate([blk[:HEAD_DIM], zeros_half], axis=0)
        bot = jnp.concatenate([zeros_half, blk[HEAD_DIM:]], axis=0)
        return jnp.concatenate([top, bot], axis=1).astype(BF16)

    def diag_masked(x):
        top = jnp.where(strictly_before, x[:LANES], 0.0)
        return top if x.shape[0] == LANES else jnp.concatenate([top, x[LANES:]], axis=0)

    def scores(c, z_ref, hl_ref):
        rows = slice(c * LANES, t)
        z = _dot(qx_ref[rows, :], jnp.concatenate([stacked(kt_ref, c), bias_blk], axis=0))
        l = _softplus2(z)
        z_ref[rows, :] = z
        for hh in range(2):
            lh = diag_masked(l[:, hh * LANES:(hh + 1) * LANES])
            hl_ref[rows, hh * LANES:(hh + 1) * LANES] = lh.astype(BF16)

    def tails(c, z_ref, hl_ref, w_ref):
        rows = slice(c * LANES, t)
        cs = _dot(hl_ref[rows, :], tri_ref[...])
        for hh in range(2):
            cs_h = cs[:, hh * LANES:(hh + 1) * LANES]
            w = jnp.exp2(z_ref[rows, hh * LANES:(hh + 1) * LANES] - cs_h - r_ref[hh, rows, :])
            r_ref[hh, rows, :] += jnp.broadcast_to(cs_h[:, 0:1], cs_h.shape)
            w_ref[rows, hh * LANES:(hh + 1) * LANES] = diag_masked(w).astype(BF16)

    def values(c, w_ref):
        rows = slice(c * LANES, t)
        acc_ref[rows, :] += _dot_nt(w_ref[rows, :], stacked(vt_ref, c))

    r_ref[...] = jnp.zeros_like(r_ref)
    acc_ref[...] = jnp.zeros_like(acc_ref)
    slots = ((z0_ref, hl0_ref, w0_ref), (z1_ref, hl1_ref, w1_ref))
    order = list(reversed(range(n_blocks)))
    for step in range(n_blocks + 2):
        if step < n_blocks:
            z_ref, hl_ref, _ = slots[step % 2]
            scores(order[step], z_ref, hl_ref)
        if 1 <= step <= n_blocks:
            tails(order[step - 1], *slots[(step - 1) % 2])
        if step >= 2:
            values(order[step - 2], slots[step % 2][2])
    o_ref[...] = acc_ref[...].astype(BF16)


def _sb_prompt(q, kt, vt, bias2, tri_pair, b, t):
    q3 = q.reshape(b, t, GROUP_W)
    pair_cm = pl.BlockSpec((None, LANES, t), lambda bb, p: (bb, p, 0))
    tile = pl.BlockSpec((None, t, LANES), lambda bb, p: (bb, 0, p))
    n_bias_cols = 3
    qe = jnp.broadcast_to((jnp.arange(LANES) < n_bias_cols).astype(BF16)[None, :], (t, LANES))
    stage_bufs = [pltpu.VMEM((t, 2 * LANES), F32), pltpu.VMEM((t, 2 * LANES), BF16),
                  pltpu.VMEM((t, 2 * LANES), BF16)]
    out = pl.pallas_call(
        _sb_prompt_kernel,
        grid=(b, N_HEADS // 2),
        in_specs=[pl.BlockSpec(memory_space=pltpu.SMEM), tile, _resident((t, LANES)), pair_cm, pair_cm,
                  _resident((2 * LANES, 2 * LANES))],
        out_specs=tile,
        out_shape=jax.ShapeDtypeStruct((b, t, GROUP_W), BF16),
        scratch_shapes=[pltpu.VMEM((t, 2 * LANES), BF16),
                        pltpu.VMEM((2, t, LANES), F32), pltpu.VMEM((t, LANES), F32)] + stage_bufs + stage_bufs,
        compiler_params=pltpu.CompilerParams(dimension_semantics=("parallel", "parallel"),
                                             vmem_limit_bytes=VMEM_LIMIT),
        name="sb_prompt",
    )(bias2, q3, qe, kt, vt, tri_pair)
    return out.reshape(b * t, GROUP_W)


N_QROWS = N_HEADS * SUBLANES


def _decode_attend(q8, kn8, vn8, k_page, v_page, n_pages, bias, tri):
    gsz = PAGE_GROUP
    row = lax.broadcasted_iota(jnp.int32, (N_QROWS, GROUP_W), 0)
    lane = lax.broadcasted_iota(jnp.int32, (N_QROWS, GROUP_W), 1)
    own_head = (row // SUBLANES) == (lane // HEAD_DIM)
    qm = jnp.where(own_head, jnp.concatenate([q8] * N_HEADS, axis=0), 0.0).astype(BF16)

    def log_terms(z, nblk, valid):
        l = _softplus2(z)
        if valid is not None:
            l = jnp.where(valid, l, 0.0)
        l = l.astype(BF16)
        return jnp.concatenate([l[:, j * LANES:(j + 1) * LANES] for j in range(nblk)], axis=0)

    def weights(z, cs2, r, nblk, valid):
        ws = []
        for j in range(nblk):
            blk = cs2[j * N_QROWS:(j + 1) * N_QROWS]
            w = jnp.exp2(z[:, j * LANES:(j + 1) * LANES] - blk[:, :LANES] - r)
            if valid is not None:
                w = jnp.where(valid, w, 0.0)
            r = r + blk[:, LANES:]
            ws.append(w.astype(BF16))
        return jnp.concatenate(ws, axis=1), r

    def group_scores(g):
        kt = jnp.concatenate([k_page(g * gsz + j).astype(BF16) for j in range(gsz)], axis=1)
        z = _dot(qm, kt) + jnp.concatenate([bias] * gsz, axis=1)
        return z, log_terms(z, gsz, None)

    def group_pv(g, w):
        vt = jnp.concatenate([v_page(g * gsz + j).astype(BF16) for j in range(gsz)], axis=1)
        return _dot_nt(w, vt)

    pad = jnp.zeros((PAGE - SUBLANES, GROUP_W), F32)
    kb = jnp.concatenate([kn8, pad], axis=0).astype(BF16)
    vb = jnp.concatenate([vn8, pad], axis=0).astype(BF16)
    r2 = lax.broadcasted_iota(jnp.int32, (N_QROWS, PAGE), 0)
    c2 = lax.broadcasted_iota(jnp.int32, (N_QROWS, PAGE), 1)
    own_valid = c2 < (r2 % SUBLANES)
    n_groups = n_pages // gsz
    n_blk = n_groups + 1
    z, hl, cs2, w = ([None] * n_blk for _ in range(4))
    acc = [None]

    def scores(i):
        if i == 0:
            z[0] = _dot_nt(qm, kb) + bias
            hl[0] = log_terms(z[0], 1, own_valid)
        else:
            z[i], hl[i] = group_scores(i - 1)

    def cumsum(i):
        cs2[i] = _dot(hl[i], tri)

    def all_weights():
        w[0], r = weights(z[0], cs2[0], jnp.zeros((N_QROWS, LANES), F32), 1, own_valid)
        for i in range(1, n_blk):
            w[i], r = weights(z[i], cs2[i], r, gsz, None)

    def values(i):
        pv = _dot(w[0], vb) if i == 0 else group_pv(i - 1, w[i])
        acc[0] = pv if i == 0 else acc[0] + pv
        if i < n_blk - 1:
            return None
        own = jnp.where(own_head, acc[0], 0.0)
        o = own[0:SUBLANES]
        for h in range(1, N_HEADS):
            o = o + own[h * SUBLANES:(h + 1) * SUBLANES]
        return o

    bind = lambda f, i: (lambda: f(i))
    return ([bind(scores, i) for i in range(n_blk)] + [bind(cumsum, i) for i in range(n_blk)]
            + [all_weights] + [bind(values, i) for i in range(n_blk)])


def _hgrn_step_kernel(q_ref, lf_ref, v_ref, sg_ref, s_ref, og_ref, o_ref, sout_ref, f_scr, k_scr, *, n_tok):
    f = jnp.exp(lf_ref[...])
    f_scr[...] = f
    k_scr[...] = 1.0 - f
    tile = (HEAD_DIM, LANES)

    def key_rows(g, outs):
        base = pl.multiple_of(g * SUBLANES, SUBLANES)
        outs = list(outs)
        gates = [[ref[pl.ds(base, SUBLANES), tok * LANES:(tok + 1) * LANES] for ref in (f_scr, k_scr, q_ref)]
                 for tok in range(n_tok)]
        for j in range(SUBLANES):
            s = s_ref[base + j]
            for tok in range(n_tok):
                fb, kb, qb = (jnp.broadcast_to(a[j:j + 1, :], tile) for a in gates[tok])
                s = fb * s + kb * v_ref[:, tok * LANES:(tok + 1) * LANES]
                outs[tok] = outs[tok] + qb * s
            sout_ref[base + j] = s
        return tuple(outs)

    outs = lax.fori_loop(0, HEAD_DIM // SUBLANES, key_rows,
                         tuple(jnp.zeros(tile, F32) for _ in range(n_tok)))
    for tok in range(n_tok):
        cols = slice(tok * LANES, (tok + 1) * LANES)
        o = outs[tok]
        ms = jnp.mean(o * o, axis=0, keepdims=True)
        o_ref[:, cols] = o * lax.rsqrt(ms + EPS) * og_ref[...] * sg_ref[:, cols]


def _hgrn_step(hq_t, lf_t, iv_t, sg_t, state, og_col, n_tok):
    width = n_tok * LANES
    head_rows = pl.BlockSpec((HEAD_DIM, width), lambda h: (h, 0))
    state_spec = pl.BlockSpec((None, HEAD_DIM, HEAD_DIM, LANES), lambda h: (h, 0, 0, 0))
    return pl.pallas_call(
        functools.partial(_hgrn_step_kernel, n_tok=n_tok),
        grid=(N_HEADS,),
        in_specs=[head_rows, head_rows, head_rows, head_rows, state_spec,
                  pl.BlockSpec((HEAD_DIM, LANES), lambda h: (0, 0))],
        out_specs=[head_rows, state_spec],
        out_shape=[jax.ShapeDtypeStruct((GROUP_W, width), F32), jax.ShapeDtypeStruct(state.shape, F32)],
        scratch_shapes=[pltpu.VMEM((HEAD_DIM, width), F32), pltpu.VMEM((HEAD_DIM, width), F32)],
        compiler_params=pltpu.CompilerParams(dimension_semantics=("parallel",),
                                             vmem_limit_bytes=VMEM_LIMIT),
        name="hgrn_step",
    )(hq_t, lf_t, iv_t, sg_t, state, og_col)


def _hgrn_prompt_kernel(hq_ref, lf_ref, iv_ref, sg_ref, s0_ref, og_ref, bd_ref, ltri_ref,
                        o_ref, sout_ref, s_scr, *, n_chunks):
    i = pl.program_id(1)

    @pl.when(i == 0)
    def _():
        s_scr[...] = s0_ref[...]

    c_len = HG_CHUNK
    gw = HG_GROUP * HEAD_DIM
    n_groups = N_HEADS // HG_GROUP
    ltri = ltri_ref[...]
    r4 = lax.broadcasted_iota(jnp.int32, (gw, gw), 0)
    c4 = lax.broadcasted_iota(jnp.int32, (gw, gw), 1)
    same_head = (r4 // HEAD_DIM) == (c4 // HEAD_DIM)
    rt = lax.broadcasted_iota(jnp.int32, (c_len, gw), 0)
    ct = lax.broadcasted_iota(jnp.int32, (c_len, gw), 1)
    causal = (ct % c_len) <= rt

    def block_diag(a):
        return jnp.where(same_head, jnp.concatenate([a] * HG_GROUP, axis=0), 0.0).astype(BF16)

    chunk_rows = [slice(c * c_len, (c + 1) * c_len) for c in range(n_chunks)]
    groups = [slice(g * gw, (g + 1) * gw) for g in range(n_groups)]
    gates = []
    for rows in chunk_rows:
        lf = lf_ref[rows, :]
        b = sum(_dot(ltri, part) for part in _split_bf16(lf, 3))
        b_last = b[c_len - 1:c_len, :]
        hk = 1.0 - jnp.exp(lf)
        qd = (hq_ref[rows, :].astype(F32) * jnp.exp(b)).astype(BF16)
        gates.append((qd, hk * jnp.exp(-b), (hk * jnp.exp(b_last - b)).astype(BF16), iv_ref[rows, :],
                      jnp.exp(b_last)))
    atts = [[jnp.where(causal, _dot_nt(qd[:, sl], block_diag(kd[:, sl])), 0.0).astype(BF16) for sl in groups]
            for qd, kd, _, _, _ in gates]
    deltas = [[jnp.where(same_head, _dot_tn(v[:, sl], kk[:, sl]), 0.0) for sl in groups]
              for _, _, kk, v, _ in gates]
    states = [s_scr[g] for g in range(n_groups)]
    outs = []
    for c in range(n_chunks):
        qd, _, _, v, decay = gates[c]
        outs.append(jnp.concatenate(
            [_dot(atts[c][g], block_diag(v[:, sl].astype(F32))) + _dot_nt(qd[:, sl], states[g].astype(BF16))
             for g, sl in enumerate(groups)], axis=1))
        states = [states[g] * decay[:, sl] + deltas[c][g] for g, sl in enumerate(groups)]
    o = jnp.concatenate(outs, axis=0)
    gated = _head_norm(o, og_ref[...], bd_ref[...]) * sg_ref[...].astype(F32)
    o_ref[...] = gated.astype(o_ref.dtype)
    for g in range(n_groups):
        s_scr[g] = states[g]

    @pl.when(i == pl.num_programs(1) - 1)
    def _():
        sout_ref[...] = s_scr[...]


def _hgrn_prompt(hq, lf, iv, sg, s0_bd, og_t, bd, bx, r, tc):
    shp = (bx, r, GROUP_W)
    gw = HG_GROUP * HEAD_DIM
    tok_spec = pl.BlockSpec((None, tc, GROUP_W), lambda b, i: (b, i, 0))
    state_spec = pl.BlockSpec((None, N_HEADS // HG_GROUP, gw, gw), lambda b, i: (b, 0, 0, 0))
    const = lambda shape: pl.BlockSpec(shape, lambda b, i: (0,) * len(shape))
    ltri = (jnp.arange(HG_CHUNK)[None, :] <= jnp.arange(HG_CHUNK)[:, None]).astype(BF16)
    return pl.pallas_call(
        functools.partial(_hgrn_prompt_kernel, n_chunks=tc // HG_CHUNK),
        grid=(bx, r // tc),
        in_specs=[tok_spec, tok_spec, tok_spec, tok_spec, state_spec,
                  const((1, GROUP_W)), const((2 * LANES, 2 * LANES)), const((HG_CHUNK, HG_CHUNK))],
        out_specs=[tok_spec, state_spec],
        out_shape=[jax.ShapeDtypeStruct(shp, BF16), jax.ShapeDtypeStruct(s0_bd.shape, F32)],
        scratch_shapes=[pltpu.VMEM(s0_bd.shape[1:], F32)],
        compiler_params=pltpu.CompilerParams(dimension_semantics=("parallel", "arbitrary"),
                                             vmem_limit_bytes=VMEM_LIMIT),
        name="hgrn_prompt",
    )(hq.reshape(shp), lf.reshape(shp), iv.reshape(shp), sg.reshape(shp), s0_bd, og_t, bd, ltri)


def _tail_compute(x_ref, oa_ref, ob_ref, g1_ref, sc_ref, sh_ref, g2_ref, n2_ref, wo_ref, wu_ref, wd_ref,
                  o_ref, acc_ref, ob_channel_major, side_work=None):
    wo_b = wo_ref[GROUP_W:2 * GROUP_W, :]
    ob = ob_ref[...].astype(BF16)
    mix = _dot(oa_ref[...].astype(BF16), wo_ref[0:GROUP_W, :])
    mix = mix + (_dot_tn(ob, wo_b) if ob_channel_major else _dot(ob, wo_b))
    x1 = x_ref[...] + g1_ref[...] * mix
    o_ref[...] = x1
    h2 = x1 * lax.rsqrt(jnp.mean(x1 * x1, axis=-1, keepdims=True) + EPS) * n2_ref[...]
    h2 = (h2 * (1.0 + sc_ref[...]) + sh_ref[...]).astype(BF16)
    tm = h2.shape[0]
    row_blocks = [slice(i * tm // MLP_ROW_SPLIT, (i + 1) * tm // MLP_ROW_SPLIT) for i in range(MLP_ROW_SPLIT)]
    k = 0
    for j in range(MLP_SLICES):
        cols = slice(j * MLP_TF, (j + 1) * MLP_TF)
        acts = []
        for rows in row_blocks:
            if side_work is not None:
                side_work(k)
            k += 1
            acts.append(jnp.square(jnp.maximum(_dot(h2[rows], wu_ref[:, cols]), 0.0)).astype(BF16))
        for rows, a in zip(row_blocks, acts):
            if side_work is not None:
                side_work(k)
            k += 1
            part = _dot(a, wd_ref[cols, :])
            if j == 0:
                acc_ref[rows, :] = part
            else:
                acc_ref[rows, :] += part
    o_ref[...] += g2_ref[...] * acc_ref[...]


def _tail_kernel(*refs, ob_channel_major):
    _tail_compute(*refs, ob_channel_major=ob_channel_major)


def _tail_specs(mods, tm, ob_channel_major):
    row_spec = pl.BlockSpec((None, tm, D_MODEL), lambda b, i, *_: (b, i, 0))
    half_spec = pl.BlockSpec((None, tm, GROUP_W), lambda b, i, *_: (b, i, 0))
    ob_spec = pl.BlockSpec((None, GROUP_W, tm), lambda b, i, *_: (b, 0, i)) if ob_channel_major else half_spec
    in_specs = [row_spec, half_spec, ob_spec,
                _mod_spec(mods, ADA_G1, tm), _mod_spec(mods, ADA_SC2, tm), _mod_spec(mods, ADA_SH2, tm),
                _mod_spec(mods, ADA_G2, tm), _resident((1, D_MODEL)),
                _resident((2 * GROUP_W, D_MODEL)), _resident((D_MODEL, D_FF)), _resident((D_FF, D_MODEL))]
    return in_specs, row_spec


def _tail(x3, oa, ob, mods, n2, w_out_bf, w_up_bf, w_down_bf, tm, ob_channel_major):
    bx, r, _ = x3.shape
    in_specs, row_spec = _tail_specs(mods, tm, ob_channel_major)
    return pl.pallas_call(
        functools.partial(_tail_kernel, ob_channel_major=ob_channel_major),
        grid=(bx, r // tm),
        in_specs=in_specs,
        out_specs=row_spec,
        out_shape=jax.ShapeDtypeStruct(x3.shape, F32),
        scratch_shapes=[pltpu.VMEM((tm, D_MODEL), F32)],
        compiler_params=pltpu.CompilerParams(dimension_semantics=("parallel", "parallel"),
                                             vmem_limit_bytes=VMEM_LIMIT),
        name="tail",
    )(x3, oa, ob, mods, mods, mods, mods, n2, w_out_bf, w_up_bf, w_down_bf)


def _tail_decode_kernel(pt_ref, x_ref, oa_ref, ob_ref, g1_ref, sc_ref, sh_ref, g2_ref, n2_ref, wo_ref, wu_ref,
                        wd_ref, q_ref, kn_ref, vn_ref, kc_hbm, vc_hbm, bias_ref, tri_ref,
                        o_ref, oa8_ref, acc_ref, kbuf, vbuf, sems, *, n_pages, seqs_per_step, n_seqs):
    step = pl.program_id(0) * pl.num_programs(1) + pl.program_id(1)

    def page_copies(seq, slot):
        copies = []
        for j in range(n_pages):
            page = pt_ref[seq * n_pages + (n_pages - 1 - j)]
            copies.append(pltpu.make_async_copy(kc_hbm.at[page], kbuf.at[slot, j], sems.at[slot]))
            copies.append(pltpu.make_async_copy(vc_hbm.at[page], vbuf.at[slot, j], sems.at[slot]))
        return copies

    @pl.when(step == 0)
    def _():
        for cp in page_copies(0, 0):
            cp.start()

    pieces = {}

    def begin_sequence(local):
        slot = local % 2
        seq = step * seqs_per_step + local
        for cp in page_copies(seq, slot):
            cp.wait()

        @pl.when(seq + 1 < n_seqs)
        def _():
            for cp in page_copies(seq + 1, 1 - slot):
                cp.start()

        pieces[local] = _decode_attend(
            q_ref[local], kn_ref[local], vn_ref[local],
            lambda p: kbuf[slot, p], lambda p: vbuf[slot, p], n_pages, bias_ref[...], tri_ref[...])

    def decode_pieces(k):
        n_pieces = 3 * (n_pages // PAGE_GROUP + 1) + 1
        total = seqs_per_step * n_pieces
        for p in range(k * total // MLP_DOTS, (k + 1) * total // MLP_DOTS):
            local, piece = divmod(p, n_pieces)
            if piece == 0:
                begin_sequence(local)
            result = pieces[local][piece]()
            if piece == n_pieces - 1:
                oa8_ref[local] = result

    _tail_compute(x_ref, oa_ref, ob_ref, g1_ref, sc_ref, sh_ref, g2_ref, n2_ref, wo_ref, wu_ref, wd_ref,
                  o_ref, acc_ref, ob_channel_major=False, side_work=decode_pieces)


def _tail_decode(x3, oa, ob, mods, n2, w_out_bf, w_up_bf, w_down_bf, tm,
                 q8, kn8, vn8, kc_cm, vc_cm, page_table, bias_bc, tri):
    bx, r, _ = x3.shape
    n_seqs, n_pages = page_table.shape
    n_steps = bx * (r // tm)
    seqs_per_step = n_seqs // n_steps
    assert seqs_per_step * n_steps == n_seqs and seqs_per_step % 2 == 0 and n_pages % PAGE_GROUP == 0
    in_specs, row_spec = _tail_specs(mods, tm, ob_channel_major=False)
    nt = r // tm
    seq_spec = pl.BlockSpec((seqs_per_step, SUBLANES, GROUP_W), lambda b, i, pt: (b * nt + i, 0, 0))
    page_buf = pltpu.VMEM((2, n_pages, GROUP_W, PAGE), F32)
    grid_spec = pltpu.PrefetchScalarGridSpec(
        num_scalar_prefetch=1,
        grid=(bx, nt),
        in_specs=in_specs + [seq_spec, seq_spec, seq_spec,
                             pl.BlockSpec(memory_space=pl.ANY), pl.BlockSpec(memory_space=pl.ANY),
                             _resident((N_QROWS, LANES)), _resident((LANES, 2 * LANES))],
        out_specs=[row_spec, seq_spec],
        scratch_shapes=[pltpu.VMEM((tm, D_MODEL), F32), page_buf, page_buf, pltpu.SemaphoreType.DMA((2,))])
    return pl.pallas_call(
        functools.partial(_tail_decode_kernel, n_pages=n_pages, seqs_per_step=seqs_per_step, n_seqs=n_seqs),
        grid_spec=grid_spec,
        out_shape=[jax.ShapeDtypeStruct(x3.shape, F32),
                   jax.ShapeDtypeStruct((n_seqs, SUBLANES, GROUP_W), F32)],
        compiler_params=pltpu.CompilerParams(dimension_semantics=("arbitrary", "arbitrary"),
                                             vmem_limit_bytes=VMEM_LIMIT_TAIL_DECODE),
        name="tail_decode",
    )(page_table.reshape(-1), x3, oa, ob, mods, mods, mods, mods, n2, w_out_bf, w_up_bf, w_down_bf,
      q8, kn8, vn8, kc_cm, vc_cm, bias_bc, tri)


def _pad_rows(a, rows):
    return jnp.pad(a, ((0, 0), (0, rows - a.shape[1]), (0, 0)))


def kernel(x_prompt, x_sample, cache_k, cache_v, state_hgrn, page_table, c_prompt, c_sample,
           w_ada, b_ada, norm1_g, norm2_g, w_in, q_norm_g, k_norm_g, sb_bias, hg_lb_logits, hg_out_g,
           w_out, w_up, w_down):
    assert w_ada.shape[0] == 1 and hg_lb_logits.shape[0] == 2, "single-layer step"
    b, t, _ = x_prompt.shape
    db, dt, _ = x_sample.shape
    n_phys = cache_k.shape[1]
    tm_p = 512

    w_in_bf = w_in[0].astype(BF16)
    w_in_t = w_in[0].T.astype(BF16)
    w_out_bf = w_out[0].astype(BF16)
    w_up_bf = w_up[0].astype(BF16)
    w_down_bf = w_down[0].astype(BF16)
    n1 = norm1_g[0].reshape(1, D_MODEL)
    n2 = norm2_g[0].reshape(1, D_MODEL)
    qg_t = jnp.tile(q_norm_g[0], N_HEADS).reshape(1, GROUP_W)
    kg_t = jnp.tile(k_norm_g[0], N_HEADS).reshape(1, GROUP_W)
    kg_cm = jnp.broadcast_to(kg_t.reshape(GROUP_W, 1), (GROUP_W, tm_p))
    og_t = jnp.tile(hg_out_g[0], N_HEADS).reshape(1, GROUP_W)
    bias2 = sb_bias[0].astype(F32) * LOG2E
    bias_bc = jnp.broadcast_to(jnp.repeat(bias2, SUBLANES)[:, None], (N_QROWS, LANES))
    idx = jnp.arange(2 * LANES)
    bd = (idx[:, None] // HEAD_DIM == idx[None, :] // HEAD_DIM).astype(BF16)
    kk = jnp.arange(LANES)
    tri = jnp.concatenate([(kk[:, None] >= kk[None, :]).astype(BF16), jnp.ones((LANES, LANES), BF16)], axis=1)

    ada = _ada(jnp.concatenate([c_prompt, c_sample], axis=0), w_ada[0], b_ada[0])
    mods_p = ada[:b].reshape(b, 1, N_ADA * D_MODEL)
    mods_s = jnp.tile(ada[b:], (dt, 1)).reshape(1, dt * db, N_ADA * D_MODEL)

    q, kt_p, vt_p, hq, lf, iv, sg = _proj(x_prompt, mods_p, n1, w_in_bf, w_in_t, qg_t, kg_cm,
                                           hg_lb_logits, bd, tm=tm_p)
    tri_pair = jnp.kron(jnp.eye(2, dtype=BF16), tri[:, :LANES])
    oa = _sb_prompt(q, kt_p, vt_p, bias2, tri_pair, b, t)
    n_hg = N_HEADS // HG_GROUP
    gw = HG_GROUP * HEAD_DIM
    ob, s_bd = _hgrn_prompt(hq, lf, iv, sg, jnp.zeros((b, n_hg, gw, gw), F32), og_t, bd, b, t, tc=1024)
    s_bd = s_bd.reshape(b, n_hg, HG_GROUP, HEAD_DIM, HG_GROUP, HEAD_DIM)
    s_p = jnp.stack([s_bd[:, :, j, :, j, :] for j in range(HG_GROUP)], axis=2)
    s_p = jnp.swapaxes(s_p, -1, -2).reshape(b, N_HEADS, HEAD_DIM, HEAD_DIM)
    oa_p, ob_p = oa.reshape(b, t, GROUP_W), ob

    assert db == LANES, "decode batch fills the lane dimension"
    ns = dt * db
    xs3 = jnp.transpose(x_sample, (1, 0, 2)).reshape(1, ns, D_MODEL)
    q, k_s, v_s, kt_s, vt_s, hq_t, lf_t, iv_t, sg_t = _proj_decode(
        xs3, mods_s, n1, w_in_bf, w_in_t, qg_t, kg_t, kg_cm, hg_lb_logits.T, bd, n_tok=dt)
    seq = lambda a: _pad_rows(jnp.transpose(a.reshape(dt, db, GROUP_W), (1, 0, 2)), SUBLANES)
    cm = lambda c: jnp.transpose(c[0], (0, 2, 3, 1)).reshape(n_phys, GROUP_W, PAGE)
    y_p, oa8 = _tail_decode(x_prompt, oa_p, ob_p, mods_p, n2, w_out_bf, w_up_bf, w_down_bf, 512,
                            seq(q), seq(k_s), seq(v_s), cm(cache_k), cm(cache_v), page_table, bias_bc, tri)
    oa = jnp.transpose(oa8[:, :dt], (1, 0, 2)).reshape(1, ns, GROUP_W)
    og_col = jnp.broadcast_to(hg_out_g[0][:, None], (HEAD_DIM, LANES))
    state_cm = jnp.transpose(state_hgrn[0], (1, 2, 3, 0))
    ob_t, s_s = _hgrn_step(hq_t, lf_t, iv_t, sg_t, state_cm, og_col, n_tok=dt)
    y_s = _tail(xs3, oa, ob_t[None], mods_s, n2, w_out_bf, w_up_bf, w_down_bf, tm=ns, ob_channel_major=True)
    y_s = jnp.transpose(y_s.reshape(dt, db, D_MODEL), (1, 0, 2))

    heads_cm = lambda a: jnp.transpose(a.reshape(b, N_HEADS, HEAD_DIM, t), (0, 3, 1, 2))[None]
    heads_tok = lambda a: jnp.transpose(a.reshape(dt, N_HEADS, HEAD_DIM, db), (3, 0, 1, 2))[None]
    return (y_p, y_s, heads_cm(kt_p), heads_cm(vt_p), heads_tok(kt_s), heads_tok(vt_s),
            s_p[None], jnp.transpose(s_s, (3, 0, 1, 2))[None])
```

```python
import functools
import math

import jax
import jax.numpy as jnp
from jax import lax
from jax.experimental import pallas as pl
from jax.experimental.pallas import tpu as pltpu

F32 = jnp.float32
BF16 = jnp.bfloat16

D_MODEL = 1024
N_HEADS = 8
HEAD_DIM = 64
GROUP_W = N_HEADS * HEAD_DIM
N_GROUPS = 7
D_FF = 4 * D_MODEL
N_ADA = 6
ADA_SH1, ADA_SC1, ADA_G1, ADA_SH2, ADA_SC2, ADA_G2 = range(N_ADA)
EPS = 1e-6
SB_SCALE = HEAD_DIM ** -0.5
LOG2E = math.log2(math.e)
HG_CHUNK = 64
PAGE = 128
LANES = 128
SUBLANES = 8
VMEM_LIMIT = 48 * 1024 * 1024
PAGE_GROUP = 4
HG_GROUP = 4
MLP_TF = 512
MLP_SLICES = D_FF // MLP_TF
MLP_ROW_SPLIT = 1
MLP_DOTS = 2 * MLP_SLICES * MLP_ROW_SPLIT
VMEM_LIMIT_TAIL_DECODE = 56 * 1024 * 1024


def _dot(a, b):
    return jnp.dot(a, b, preferred_element_type=F32)


def _dot_nt(a, b):
    return lax.dot_general(a, b, (((1,), (1,)), ((), ())), preferred_element_type=F32)


def _dot_tn(a, b):
    return lax.dot_general(a, b, (((0,), (0,)), ((), ())), preferred_element_type=F32)


def _split_bf16(x, n):
    parts = []
    r = x
    for i in range(n):
        p = r.astype(BF16)
        parts.append(p)
        if i + 1 < n:
            r = r - p.astype(F32)
    return parts


def _silu(x):
    return x * jax.nn.sigmoid(x)


def _head_norm(y, gain, bd):
    sq = (y * y).astype(BF16)
    half = 2 * LANES
    ss = jnp.concatenate([_dot(sq[:, c:c + half], bd) for c in range(0, GROUP_W, half)], axis=1)
    return y * lax.rsqrt(ss * (1.0 / HEAD_DIM) + EPS) * gain


def _head_norm_cm(yt, gain, bd):
    sq = (yt * yt).astype(BF16)
    half = 2 * LANES
    ss = jnp.concatenate([_dot(bd, sq[c:c + half]) for c in range(0, GROUP_W, half)], axis=0)
    return yt * lax.rsqrt(ss * (1.0 / HEAD_DIM) + EPS) * gain


def _softplus2(z2):
    return jnp.maximum(z2, 0.0) + jnp.log(1.0 + jnp.exp2(-jnp.abs(z2))) * LOG2E


def _ada_kernel(c_ref, w_ref, b_ref, o_ref):
    s = _silu(c_ref[...]).astype(BF16)
    o_ref[...] = _dot(s, w_ref[...].astype(BF16)) + b_ref[...]


def _ada(c_all, w_ada, b_ada):
    m = c_all.shape[0]
    n = w_ada.shape[1]
    tn = 1024
    return pl.pallas_call(
        _ada_kernel,
        grid=(n // tn,),
        in_specs=[pl.BlockSpec((m, D_MODEL), lambda j: (0, 0)),
                  pl.BlockSpec((D_MODEL, tn), lambda j: (0, j)),
                  pl.BlockSpec((1, tn), lambda j: (0, j))],
        out_specs=pl.BlockSpec((m, tn), lambda j: (0, j)),
        out_shape=jax.ShapeDtypeStruct((m, n), F32),
        compiler_params=pltpu.CompilerParams(dimension_semantics=("parallel",),
                                             vmem_limit_bytes=VMEM_LIMIT),
        name="ada",
    )(c_all, w_ada, b_ada.reshape(1, n))


def _norm1(x_ref, sc_ref, sh_ref, n1_ref):
    x = x_ref[...]
    h = x * lax.rsqrt(jnp.mean(x * x, axis=-1, keepdims=True) + EPS) * n1_ref[...]
    return (h * (1.0 + sc_ref[...]) + sh_ref[...]).astype(BF16)


def _forget_gate_log(lb, y):
    return jnp.log(lb + (1.0 - lb) * jax.nn.sigmoid(y))


def _proj_kernel(x_ref, sc_ref, sh_ref, n1_ref, w_ref, wt_ref, qg_ref, kgc_ref, lbl_ref, bd_ref,
                 q_ref, kt_ref, vt_ref, hq_ref, lf_ref, i_ref, g_ref):
    hb = _norm1(x_ref, sc_ref, sh_ref, n1_ref)
    bd = bd_ref[...]

    def group(g):
        return _dot(hb, w_ref[:, g * GROUP_W:(g + 1) * GROUP_W])

    def group_cm(g):
        return _dot_nt(wt_ref[g * GROUP_W:(g + 1) * GROUP_W, :], hb)

    yq = group(0)
    ykt = group_cm(1)
    vt_ref[...] = group_cm(2)
    hq_ref[...] = _silu(group(3)).astype(BF16)
    q_ref[...] = (_head_norm(yq, qg_ref[...], bd) * (SB_SCALE * LOG2E)).astype(BF16)
    lbl = lbl_ref[...]
    e = jnp.exp(lbl - jnp.max(lbl, axis=0, keepdims=True))
    lb = e[0:1, :] / jnp.sum(e, axis=0, keepdims=True)
    lf_ref[...] = _forget_gate_log(lb, group(4))
    kt_ref[...] = _head_norm_cm(ykt, kgc_ref[...], bd)
    i_ref[...] = group(5).astype(BF16)
    g_ref[...] = _silu(group(6)).astype(BF16)


def _proj_decode_kernel(x_ref, sc_ref, sh_ref, n1_ref, w_ref, wt_ref, qg_ref, kg_ref, kgc_ref, lblt_ref, bd_ref,
                        q_ref, k_ref, v_ref, kt_ref, vt_ref, hq_ref, lf_ref, i_ref, g_ref):
    hb = _norm1(x_ref, sc_ref, sh_ref, n1_ref)
    bd = bd_ref[...]

    def group(g):
        return _dot(hb, w_ref[:, g * GROUP_W:(g + 1) * GROUP_W])

    def group_cm(g):
        return _dot_nt(wt_ref[g * GROUP_W:(g + 1) * GROUP_W, :], hb)

    q_ref[...] = _head_norm(group(0), qg_ref[...], bd) * (SB_SCALE * LOG2E)
    k_ref[...] = _head_norm(group(1), kg_ref[...], bd)
    v_ref[...] = group(2)
    kt = _head_norm_cm(group_cm(1), kgc_ref[...], bd)
    vt = group_cm(2)
    for tok in range(kt_ref.shape[0]):
        kt_ref[tok] = kt[:, tok * LANES:(tok + 1) * LANES]
        vt_ref[tok] = vt[:, tok * LANES:(tok + 1) * LANES]
    hq_ref[...] = _silu(group_cm(3))
    lblt = lblt_ref[...]
    l0, l1 = lblt[:, 0:1], lblt[:, 1:2]
    m = jnp.maximum(l0, l1)
    e0, e1 = jnp.exp(l0 - m), jnp.exp(l1 - m)
    lf_ref[...] = _forget_gate_log(e0 / (e0 + e1), group_cm(4))
    i_ref[...] = group_cm(5)
    g_ref[...] = _silu(group_cm(6))


def _mod_spec(mods, m, tm):
    if mods.shape[1] == 1:
        return pl.BlockSpec((None, 1, D_MODEL), lambda b, i, *_: (b, 0, m))
    return pl.BlockSpec((None, tm, D_MODEL), lambda b, i, *_: (b, i, m))


def _resident(shape):
    return pl.BlockSpec(shape, lambda *_: (0,) * len(shape), pipeline_mode=pl.Buffered(1))


def _proj(x3, mods, n1, w_in_bf, w_in_t, qg_t, kg_cm, lb_logits, bd, tm):
    bx, r, _ = x3.shape
    nt = r // tm
    row_spec = pl.BlockSpec((tm, GROUP_W), lambda b, i: (b * nt + i, 0))
    row_shape = lambda dt: jax.ShapeDtypeStruct((bx * r, GROUP_W), dt)
    cm_spec = pl.BlockSpec((None, GROUP_W, tm), lambda b, i: (b, 0, i))
    cm_shape = jax.ShapeDtypeStruct((bx, GROUP_W, r), F32)
    return pl.pallas_call(
        _proj_kernel,
        grid=(bx, nt),
        in_specs=[pl.BlockSpec((None, tm, D_MODEL), lambda b, i: (b, i, 0)),
                  _mod_spec(mods, ADA_SC1, tm), _mod_spec(mods, ADA_SH1, tm),
                  _resident((1, D_MODEL)), _resident(w_in_bf.shape), _resident(w_in_t.shape),
                  _resident((1, GROUP_W)), _resident(kg_cm.shape), _resident(lb_logits.shape),
                  _resident((2 * LANES, 2 * LANES))],
        out_specs=[row_spec, cm_spec, cm_spec, row_spec, row_spec, row_spec, row_spec],
        out_shape=[row_shape(BF16), cm_shape, cm_shape, row_shape(BF16), row_shape(F32),
                   row_shape(BF16), row_shape(BF16)],
        compiler_params=pltpu.CompilerParams(dimension_semantics=("parallel", "parallel"),
                                             vmem_limit_bytes=VMEM_LIMIT),
        name="proj",
    )(x3, mods, mods, n1, w_in_bf, w_in_t, qg_t, kg_cm, lb_logits, bd)


def _proj_decode(x3, mods, n1, w_in_bf, w_in_t, qg_t, kg_t, kg_cm, lb_logits_t, bd, n_tok):
    _, r, _ = x3.shape
    nseq = r // n_tok
    row_spec = pl.BlockSpec((r, GROUP_W), lambda b, i: (0, 0))
    row_shape = jax.ShapeDtypeStruct((r, GROUP_W), F32)
    tok_spec = pl.BlockSpec((n_tok, GROUP_W, nseq), lambda b, i: (0, 0, 0))
    tok_shape = jax.ShapeDtypeStruct((n_tok, GROUP_W, nseq), F32)
    cm_spec = pl.BlockSpec((GROUP_W, r), lambda b, i: (0, 0))
    cm_shape = jax.ShapeDtypeStruct((GROUP_W, r), F32)
    return pl.pallas_call(
        _proj_decode_kernel,
        grid=(1, 1),
        in_specs=[pl.BlockSpec((None, r, D_MODEL), lambda b, i: (b, i, 0)),
                  _mod_spec(mods, ADA_SC1, r), _mod_spec(mods, ADA_SH1, r),
                  _resident((1, D_MODEL)), _resident(w_in_bf.shape), _resident(w_in_t.shape),
                  _resident((1, GROUP_W)), _resident((1, GROUP_W)), _resident(kg_cm.shape),
                  _resident(lb_logits_t.shape), _resident((2 * LANES, 2 * LANES))],
        out_specs=[row_spec, row_spec, row_spec, tok_spec, tok_spec, cm_spec, cm_spec, cm_spec, cm_spec],
        out_shape=[row_shape, row_shape, row_shape, tok_shape, tok_shape,
                   cm_shape, cm_shape, cm_shape, cm_shape],
        compiler_params=pltpu.CompilerParams(dimension_semantics=("parallel", "parallel"),
                                             vmem_limit_bytes=VMEM_LIMIT),
        name="proj_decode",
    )(x3, mods, mods, n1, w_in_bf, w_in_t, qg_t, kg_t, kg_cm, lb_logits_t, bd)


def _sb_prompt_kernel(bias_ref, q_ref, qe_ref, kt_ref, vt_ref, tri_ref, o_ref,
                      qx_ref, r_ref, acc_ref, z0_ref, hl0_ref, w0_ref, z1_ref, hl1_ref, w1_ref):
    p = pl.program_id(1)
    t = q_ref.shape[0]
    n_blocks = t // LANES
    lane2 = lax.broadcasted_iota(jnp.int32, (1, 2 * LANES), 1)
    bias_row = jnp.where(lane2 < LANES, bias_ref[2 * p], bias_ref[2 * p + 1])
    brow = lax.broadcasted_iota(jnp.int32, (LANES, 2 * LANES), 0)
    bias_blk = jnp.zeros((LANES, 2 * LANES), F32)
    for n, part in enumerate(_split_bf16(bias_row, 3)):
        bias_blk = jnp.where(brow == n, part.astype(F32), bias_blk)
    bias_blk = bias_blk.astype(BF16)
    zeros_half = jnp.zeros((HEAD_DIM, LANES), F32)
    qx_ref[:, :LANES] = q_ref[...]
    qx_ref[:, LANES:] = qe_ref[...]
    row = lax.broadcasted_iota(jnp.int32, (LANES, LANES), 0)
    col = lax.broadcasted_iota(jnp.int32, (LANES, LANES), 1)
    strictly_before = col < row

    def stacked(ref, c):
        blk = ref[:, c * LANES:(c + 1) * LANES]
        top = jnp.concatenate([blk[:HEAD_DIM], zeros_half], axis=0)
        bot = jnp.concatenate([zeros_half, blk[HEAD_DIM:]], axis=0)
        return jnp.concatenate([top, bot], axis=1).astype(BF16)

    def diag_masked(x):
        top = jnp.where(strictly_before, x[:LANES], 0.0)
        return top if x.shape[0] == LANES else jnp.concatenate([top, x[LANES:]], axis=0)

    def scores(c, z_ref, hl_ref):
        rows = slice(c * LANES, t)
        z = _dot(qx_ref[rows, :], jnp.concatenate([stacked(kt_ref, c), bias_blk], axis=0))
        l = _softplus2(z)
        z_ref[rows, :] = z
        for hh in range(2):
            lh = diag_masked(l[:, hh * LANES:(hh + 1) * LANES])
            hl_ref[rows, hh * LANES:(hh + 1) * LANES] = lh.astype(BF16)

    def tails(c, z_ref, hl_ref, w_ref):
        rows = slice(c * LANES, t)
        cs = _dot(hl_ref[rows, :], tri_ref[...])
        for hh in range(2):
            cs_h = cs[:, hh * LANES:(hh + 1) * LANES]
            w = jnp.exp2(z_ref[rows, hh * LANES:(hh + 1) * LANES] - cs_h - r_ref[hh, rows, :])
            r_ref[hh, rows, :] += jnp.broadcast_to(cs_h[:, 0:1], cs_h.shape)
            w_ref[rows, hh * LANES:(hh + 1) * LANES] = diag_masked(w).astype(BF16)

    def values(c, w_ref):
        rows = slice(c * LANES, t)
        acc_ref[rows, :] += _dot_nt(w_ref[rows, :], stacked(vt_ref, c))

    r_ref[...] = jnp.zeros_like(r_ref)
    acc_ref[...] = jnp.zeros_like(acc_ref)
    slots = ((z0_ref, hl0_ref, w0_ref), (z1_ref, hl1_ref, w1_ref))
    order = list(reversed(range(n_blocks)))
    for step in range(n_blocks + 2):
        if step < n_blocks:
            z_ref, hl_ref, _ = slots[step % 2]
            scores(order[step], z_ref, hl_ref)
        if 1 <= step <= n_blocks:
            tails(order[step - 1], *slots[(step - 1) % 2])
        if step >= 2:
            values(order[step - 2], slots[step % 2][2])
    o_ref[...] = acc_ref[...].astype(BF16)


def _sb_prompt(q, kt, vt, bias2, tri_pair, b, t):
    q3 = q.reshape(b, t, GROUP_W)
    pair_cm = pl.BlockSpec((None, LANES, t), lambda bb, p: (bb, p, 0))
    tile = pl.BlockSpec((None, t, LANES), lambda bb, p: (bb, 0, p))
    n_bias_cols = 3
    qe = jnp.broadcast_to((jnp.arange(LANES) < n_bias_cols).astype(BF16)[None, :], (t, LANES))
    stage_bufs = [pltpu.VMEM((t, 2 * LANES), F32), pltpu.VMEM((t, 2 * LANES), BF16),
                  pltpu.VMEM((t, 2 * LANES), BF16)]
    out = pl.pallas_call(
        _sb_prompt_kernel,
        grid=(b, N_HEADS // 2),
        in_specs=[pl.BlockSpec(memory_space=pltpu.SMEM), tile, _resident((t, LANES)), pair_cm, pair_cm,
                  _resident((2 * LANES, 2 * LANES))],
        out_specs=tile,
        out_shape=jax.ShapeDtypeStruct((b, t, GROUP_W), BF16),
        scratch_shapes=[pltpu.VMEM((t, 2 * LANES), BF16),
                        pltpu.VMEM((2, t, LANES), F32), pltpu.VMEM((t, LANES), F32)] + stage_bufs + stage_bufs,
        compiler_params=pltpu.CompilerParams(dimension_semantics=("parallel", "parallel"),
                                             vmem_limit_bytes=VMEM_LIMIT),
        name="sb_prompt",
    )(bias2, q3, qe, kt, vt, tri_pair)
    return out.reshape(b * t, GROUP_W)


N_QROWS = N_HEADS * SUBLANES


def _decode_attend(q8, kn8, vn8, k_page, v_page, n_pages, bias, tri):
    gsz = PAGE_GROUP
    row = lax.broadcasted_iota(jnp.int32, (N_QROWS, GROUP_W), 0)
    lane = lax.broadcasted_iota(jnp.int32, (N_QROWS, GROUP_W), 1)
    own_head = (row // SUBLANES) == (lane // HEAD_DIM)
    qm = jnp.where(own_head, jnp.concatenate([q8] * N_HEADS, axis=0), 0.0).astype(BF16)

    def log_terms(z, nblk, valid):
        l = _softplus2(z)
        if valid is not None:
            l = jnp.where(valid, l, 0.0)
        l = l.astype(BF16)
        return jnp.concatenate([l[:, j * LANES:(j + 1) * LANES] for j in range(nblk)], axis=0)

    def weights(z, cs2, r, nblk, valid):
        ws = []
        for j in range(nblk):
            blk = cs2[j * N_QROWS:(j + 1) * N_QROWS]
            w = jnp.exp2(z[:, j * LANES:(j + 1) * LANES] - blk[:, :LANES] - r)
            if valid is not None:
                w = jnp.where(valid, w, 0.0)
            r = r + blk[:, LANES:]
            ws.append(w.astype(BF16))
        return jnp.concatenate(ws, axis=1), r

    def group_scores(g):
        kt = jnp.concatenate([k_page(g * gsz + j).astype(BF16) for j in range(gsz)], axis=1)
        z = _dot(qm, kt) + jnp.concatenate([bias] * gsz, axis=1)
        return z, log_terms(z, gsz, None)

    def group_pv(g, w):
        vt = jnp.concatenate([v_page(g * gsz + j).astype(BF16) for j in range(gsz)], axis=1)
        return _dot_nt(w, vt)

    pad = jnp.zeros((PAGE - SUBLANES, GROUP_W), F32)
    kb = jnp.concatenate([kn8, pad], axis=0).astype(BF16)
    vb = jnp.concatenate([vn8, pad], axis=0).astype(BF16)
    r2 = lax.broadcasted_iota(jnp.int32, (N_QROWS, PAGE), 0)
    c2 = lax.broadcasted_iota(jnp.int32, (N_QROWS, PAGE), 1)
    own_valid = c2 < (r2 % SUBLANES)
    n_groups = n_pages // gsz
    n_blk = n_groups + 1
    z, hl, cs2, w = ([None] * n_blk for _ in range(4))
    acc = [None]

    def scores(i):
        if i == 0:
            z[0] = _dot_nt(qm, kb) + bias
            hl[0] = log_terms(z[0], 1, own_valid)
        else:
            z[i], hl[i] = group_scores(i - 1)

    def cumsum(i):
        cs2[i] = _dot(hl[i], tri)

    def all_weights():
        w[0], r = weights(z[0], cs2[0], jnp.zeros((N_QROWS, LANES), F32), 1, own_valid)
        for i in range(1, n_blk):
            w[i], r = weights(z[i], cs2[i], r, gsz, None)

    def values(i):
        pv = _dot(w[0], vb) if i == 0 else group_pv(i - 1, w[i])
        acc[0] = pv if i == 0 else acc[0] + pv
        if i < n_blk - 1:
            return None
        own = jnp.where(own_head, acc[0], 0.0)
        o = own[0:SUBLANES]
        for h in range(1, N_HEADS):
            o = o + own[h * SUBLANES:(h + 1) * SUBLANES]
        return o

    bind = lambda f, i: (lambda: f(i))
    return ([bind(scores, i) for i in range(n_blk)] + [bind(cumsum, i) for i in range(n_blk)]
            + [all_weights] + [bind(values, i) for i in range(n_blk)])


def _hgrn_step_kernel(q_ref, lf_ref, v_ref, sg_ref, s_ref, og_ref, o_ref, sout_ref, f_scr, k_scr, *, n_tok):
    f = jnp.exp(lf_ref[...])
    f_scr[...] = f
    k_scr[...] = 1.0 - f
    tile = (HEAD_DIM, LANES)

    def key_rows(g, outs):
        base = pl.multiple_of(g * SUBLANES, SUBLANES)
        outs = list(outs)
        gates = [[ref[pl.ds(base, SUBLANES), tok * LANES:(tok + 1) * LANES] for ref in (f_scr, k_scr, q_ref)]
                 for tok in range(n_tok)]
        for j in range(SUBLANES):
            s = s_ref[base + j]
            for tok in range(n_tok):
                fb, kb, qb = (jnp.broadcast_to(a[j:j + 1, :], tile) for a in gates[tok])
                s = fb * s + kb * v_ref[:, tok * LANES:(tok + 1) * LANES]
                outs[tok] = outs[tok] + qb * s
            sout_ref[base + j] = s
        return tuple(outs)

    outs = lax.fori_loop(0, HEAD_DIM // SUBLANES, key_rows,
                         tuple(jnp.zeros(tile, F32) for _ in range(n_tok)))
    for tok in range(n_tok):
        cols = slice(tok * LANES, (tok + 1) * LANES)
        o = outs[tok]
        ms = jnp.mean(o * o, axis=0, keepdims=True)
        o_ref[:, cols] = o * lax.rsqrt(ms + EPS) * og_ref[...] * sg_ref[:, cols]


def _hgrn_step(hq_t, lf_t, iv_t, sg_t, state, og_col, n_tok):
    width = n_tok * LANES
    head_rows = pl.BlockSpec((HEAD_DIM, width), lambda h: (h, 0))
    state_spec = pl.BlockSpec((None, HEAD_DIM, HEAD_DIM, LANES), lambda h: (h, 0, 0, 0))
    return pl.pallas_call(
        functools.partial(_hgrn_step_kernel, n_tok=n_tok),
        grid=(N_HEADS,),
        in_specs=[head_rows, head_rows, head_rows, head_rows, state_spec,
                  pl.BlockSpec((HEAD_DIM, LANES), lambda h: (0, 0))],
        out_specs=[head_rows, state_spec],
        out_shape=[jax.ShapeDtypeStruct((GROUP_W, width), F32), jax.ShapeDtypeStruct(state.shape, F32)],
        scratch_shapes=[pltpu.VMEM((HEAD_DIM, width), F32), pltpu.VMEM((HEAD_DIM, width), F32)],
        compiler_params=pltpu.CompilerParams(dimension_semantics=("parallel",),
                                             vmem_limit_bytes=VMEM_LIMIT),
        name="hgrn_step",
    )(hq_t, lf_t, iv_t, sg_t, state, og_col)


def _hgrn_prompt_kernel(hq_ref, lf_ref, iv_ref, sg_ref, s0_ref, og_ref, bd_ref, ltri_ref,
                        o_ref, sout_ref, s_scr, *, n_chunks):
    i = pl.program_id(1)

    @pl.when(i == 0)
    def _():
        s_scr[...] = s0_ref[...]

    c_len = HG_CHUNK
    gw = HG_GROUP * HEAD_DIM
    n_groups = N_HEADS // HG_GROUP
    ltri = ltri_ref[...]
    r4 = lax.broadcasted_iota(jnp.int32, (gw, gw), 0)
    c4 = lax.broadcasted_iota(jnp.int32, (gw, gw), 1)
    same_head = (r4 // HEAD_DIM) == (c4 // HEAD_DIM)
    rt = lax.broadcasted_iota(jnp.int32, (c_len, gw), 0)
    ct = lax.broadcasted_iota(jnp.int32, (c_len, gw), 1)
    causal = (ct % c_len) <= rt

    def block_diag(a):
        return jnp.where(same_head, jnp.concatenate([a] * HG_GROUP, axis=0), 0.0).astype(BF16)

    chunk_rows = [slice(c * c_len, (c + 1) * c_len) for c in range(n_chunks)]
    groups = [slice(g * gw, (g + 1) * gw) for g in range(n_groups)]
    gates = []
    for rows in chunk_rows:
        lf = lf_ref[rows, :]
        b = sum(_dot(ltri, part) for part in _split_bf16(lf, 3))
        b_last = b[c_len - 1:c_len, :]
        hk = 1.0 - jnp.exp(lf)
        qd = (hq_ref[rows, :].astype(F32) * jnp.exp(b)).astype(BF16)
        gates.append((qd, hk * jnp.exp(-b), (hk * jnp.exp(b_last - b)).astype(BF16), iv_ref[rows, :],
                      jnp.exp(b_last)))
    atts = [[jnp.where(causal, _dot_nt(qd[:, sl], block_diag(kd[:, sl])), 0.0).astype(BF16) for sl in groups]
            for qd, kd, _, _, _ in gates]
    deltas = [[jnp.where(same_head, _dot_tn(v[:, sl], kk[:, sl]), 0.0) for sl in groups]
              for _, _, kk, v, _ in gates]
    states = [s_scr[g] for g in range(n_groups)]
    outs = []
    for c in range(n_chunks):
        qd, _, _, v, decay = gates[c]
        outs.append(jnp.concatenate(
            [_dot(atts[c][g], block_diag(v[:, sl].astype(F32))) + _dot_nt(qd[:, sl], states[g].astype(BF16))
             for g, sl in enumerate(groups)], axis=1))
        states = [states[g] * decay[:, sl] + deltas[c][g] for g, sl in enumerate(groups)]
    o = jnp.concatenate(outs, axis=0)
    gated = _head_norm(o, og_ref[...], bd_ref[...]) * sg_ref[...].astype(F32)
    o_ref[...] = gated.astype(o_ref.dtype)
    for g in range(n_groups):
        s_scr[g] = states[g]

    @pl.when(i == pl.num_programs(1) - 1)
    def _():
        sout_ref[...] = s_scr[...]


def _hgrn_prompt(hq, lf, iv, sg, s0_bd, og_t, bd, bx, r, tc):
    shp = (bx, r, GROUP_W)
    gw = HG_GROUP * HEAD_DIM
    tok_spec = pl.BlockSpec((None, tc, GROUP_W), lambda b, i: (b, i, 0))
    state_spec = pl.BlockSpec((None, N_HEADS // HG_GROUP, gw, gw), lambda b, i: (b, 0, 0, 0))
    const = lambda shape: pl.BlockSpec(shape, lambda b, i: (0,) * len(shape))
    ltri = (jnp.arange(HG_CHUNK)[None, :] <= jnp.arange(HG_CHUNK)[:, None]).astype(BF16)
    return pl.pallas_call(
        functools.partial(_hgrn_prompt_kernel, n_chunks=tc // HG_CHUNK),
        grid=(bx, r // tc),
        in_specs=[tok_spec, tok_spec, tok_spec, tok_spec, state_spec,
                  const((1, GROUP_W)), const((2 * LANES, 2 * LANES)), const((HG_CHUNK, HG_CHUNK))],
        out_specs=[tok_spec, state_spec],
        out_shape=[jax.ShapeDtypeStruct(shp, BF16), jax.ShapeDtypeStruct(s0_bd.shape, F32)],
        scratch_shapes=[pltpu.VMEM(s0_bd.shape[1:], F32)],
        compiler_params=pltpu.CompilerParams(dimension_semantics=("parallel", "arbitrary"),
                                             vmem_limit_bytes=VMEM_LIMIT),
        name="hgrn_prompt",
    )(hq.reshape(shp), lf.reshape(shp), iv.reshape(shp), sg.reshape(shp), s0_bd, og_t, bd, ltri)


def _tail_compute(x_ref, oa_ref, ob_ref, g1_ref, sc_ref, sh_ref, g2_ref, n2_ref, wo_ref, wu_ref, wd_ref,
                  o_ref, acc_ref, ob_channel_major, side_work=None):
    wo_b = wo_ref[GROUP_W:2 * GROUP_W, :]
    ob = ob_ref[...].astype(BF16)
    mix = _dot(oa_ref[...].astype(BF16), wo_ref[0:GROUP_W, :])
    mix = mix + (_dot_tn(ob, wo_b) if ob_channel_major else _dot(ob, wo_b))
    x1 = x_ref[...] + g1_ref[...] * mix
    o_ref[...] = x1
    h2 = x1 * lax.rsqrt(jnp.mean(x1 * x1, axis=-1, keepdims=True) + EPS) * n2_ref[...]
    h2 = (h2 * (1.0 + sc_ref[...]) + sh_ref[...]).astype(BF16)
    tm = h2.shape[0]
    row_blocks = [slice(i * tm // MLP_ROW_SPLIT, (i + 1) * tm // MLP_ROW_SPLIT) for i in range(MLP_ROW_SPLIT)]
    k = 0
    for j in range(MLP_SLICES):
        cols = slice(j * MLP_TF, (j + 1) * MLP_TF)
        acts = []
        for rows in row_blocks:
            if side_work is not None:
                side_work(k)
            k += 1
            acts.append(jnp.square(jnp.maximum(_dot(h2[rows], wu_ref[:, cols]), 0.0)).astype(BF16))
        for rows, a in zip(row_blocks, acts):
            if side_work is not None:
                side_work(k)
            k += 1
            part = _dot(a, wd_ref[cols, :])
            if j == 0:
                acc_ref[rows, :] = part
            else:
                acc_ref[rows, :] += part
    o_ref[...] += g2_ref[...] * acc_ref[...]


def _tail_kernel(*refs, ob_channel_major):
    _tail_compute(*refs, ob_channel_major=ob_channel_major)


def _tail_specs(mods, tm, ob_channel_major):
    row_spec = pl.BlockSpec((None, tm, D_MODEL), lambda b, i, *_: (b, i, 0))
    half_spec = pl.BlockSpec((None, tm, GROUP_W), lambda b, i, *_: (b, i, 0))
    ob_spec = pl.BlockSpec((None, GROUP_W, tm), lambda b, i, *_: (b, 0, i)) if ob_channel_major else half_spec
    in_specs = [row_spec, half_spec, ob_spec,
                _mod_spec(mods, ADA_G1, tm), _mod_spec(mods, ADA_SC2, tm), _mod_spec(mods, ADA_SH2, tm),
                _mod_spec(mods, ADA_G2, tm), _resident((1, D_MODEL)),
                _resident((2 * GROUP_W, D_MODEL)), _resident((D_MODEL, D_FF)), _resident((D_FF, D_MODEL))]
    return in_specs, row_spec


def _tail(x3, oa, ob, mods, n2, w_out_bf, w_up_bf, w_down_bf, tm, ob_channel_major):
    bx, r, _ = x3.shape
    in_specs, row_spec = _tail_specs(mods, tm, ob_channel_major)
    return pl.pallas_call(
        functools.partial(_tail_kernel, ob_channel_major=ob_channel_major),
        grid=(bx, r // tm),
        in_specs=in_specs,
        out_specs=row_spec,
        out_shape=jax.ShapeDtypeStruct(x3.shape, F32),
        scratch_shapes=[pltpu.VMEM((tm, D_MODEL), F32)],
        compiler_params=pltpu.CompilerParams(dimension_semantics=("parallel", "parallel"),
                                             vmem_limit_bytes=VMEM_LIMIT),
        name="tail",
    )(x3, oa, ob, mods, mods, mods, mods, n2, w_out_bf, w_up_bf, w_down_bf)


def _tail_decode_kernel(pt_ref, x_ref, oa_ref, ob_ref, g1_ref, sc_ref, sh_ref, g2_ref, n2_ref, wo_ref, wu_ref,
                        wd_ref, q_ref, kn_ref, vn_ref, kc_hbm, vc_hbm, bias_ref, tri_ref,
                        o_ref, oa8_ref, acc_ref, kbuf, vbuf, sems, *, n_pages, seqs_per_step, n_seqs):
    step = pl.program_id(0) * pl.num_programs(1) + pl.program_id(1)

    def page_copies(seq, slot):
        copies = []
        for j in range(n_pages):
            page = pt_ref[seq * n_pages + (n_pages - 1 - j)]
            copies.append(pltpu.make_async_copy(kc_hbm.at[page], kbuf.at[slot, j], sems.at[slot]))
            copies.append(pltpu.make_async_copy(vc_hbm.at[page], vbuf.at[slot, j], sems.at[slot]))
        return copies

    @pl.when(step == 0)
    def _():
        for cp in page_copies(0, 0):
            cp.start()

    pieces = {}

    def begin_sequence(local):
        slot = local % 2
        seq = step * seqs_per_step + local
        for cp in page_copies(seq, slot):
            cp.wait()
        for cp in page_copies(jnp.minimum(seq + 1, n_seqs - 1), 1 - slot):
            cp.start()

        pieces[local] = _decode_attend(
            q_ref[local], kn_ref[local], vn_ref[local],
            lambda p: kbuf[slot, p], lambda p: vbuf[slot, p], n_pages, bias_ref[...], tri_ref[...])

    def decode_pieces(k):
        n_pieces = 3 * (n_pages // PAGE_GROUP + 1) + 1
        total = seqs_per_step * n_pieces
        for p in range(k * total // MLP_DOTS, (k + 1) * total // MLP_DOTS):
            local, piece = divmod(p, n_pieces)
            if piece == 0:
                begin_sequence(local)
            result = pieces[local][piece]()
            if piece == n_pieces - 1:
                oa8_ref[local] = result

    _tail_compute(x_ref, oa_ref, ob_ref, g1_ref, sc_ref, sh_ref, g2_ref, n2_ref, wo_ref, wu_ref, wd_ref,
                  o_ref, acc_ref, ob_channel_major=False, side_work=decode_pieces)

    @pl.when(step == pl.num_programs(0) * pl.num_programs(1) - 1)
    def _():
        for cp in page_copies(n_seqs - 1, seqs_per_step % 2):
            cp.wait()


def _tail_decode(x3, oa, ob, mods, n2, w_out_bf, w_up_bf, w_down_bf, tm,
                 q8, kn8, vn8, kc_cm, vc_cm, page_table, bias_bc, tri):
    bx, r, _ = x3.shape
    n_seqs, n_pages = page_table.shape
    n_steps = bx * (r // tm)
    seqs_per_step = n_seqs // n_steps
    assert seqs_per_step * n_steps == n_seqs and seqs_per_step % 2 == 0 and n_pages % PAGE_GROUP == 0
    in_specs, row_spec = _tail_specs(mods, tm, ob_channel_major=False)
    nt = r // tm
    seq_spec = pl.BlockSpec((seqs_per_step, SUBLANES, GROUP_W), lambda b, i, pt: (b * nt + i, 0, 0))
    page_buf = pltpu.VMEM((2, n_pages, GROUP_W, PAGE), F32)
    grid_spec = pltpu.PrefetchScalarGridSpec(
        num_scalar_prefetch=1,
        grid=(bx, nt),
        in_specs=in_specs + [seq_spec, seq_spec, seq_spec,
                             pl.BlockSpec(memory_space=pl.ANY), pl.BlockSpec(memory_space=pl.ANY),
                             _resident((N_QROWS, LANES)), _resident((LANES, 2 * LANES))],
        out_specs=[row_spec, seq_spec],
        scratch_shapes=[pltpu.VMEM((tm, D_MODEL), F32), page_buf, page_buf, pltpu.SemaphoreType.DMA((2,))])
    return pl.pallas_call(
        functools.partial(_tail_decode_kernel, n_pages=n_pages, seqs_per_step=seqs_per_step, n_seqs=n_seqs),
        grid_spec=grid_spec,
        out_shape=[jax.ShapeDtypeStruct(x3.shape, F32),
                   jax.ShapeDtypeStruct((n_seqs, SUBLANES, GROUP_W), F32)],
        compiler_params=pltpu.CompilerParams(dimension_semantics=("arbitrary", "arbitrary"),
                                             vmem_limit_bytes=VMEM_LIMIT_TAIL_DECODE),
        name="tail_decode",
    )(page_table.reshape(-1), x3, oa, ob, mods, mods, mods, mods, n2, w_out_bf, w_up_bf, w_down_bf,
      q8, kn8, vn8, kc_cm, vc_cm, bias_bc, tri)


def _pad_rows(a, rows):
    return jnp.pad(a, ((0, 0), (0, rows - a.shape[1]), (0, 0)))


def kernel(x_prompt, x_sample, cache_k, cache_v, state_hgrn, page_table, c_prompt, c_sample,
           w_ada, b_ada, norm1_g, norm2_g, w_in, q_norm_g, k_norm_g, sb_bias, hg_lb_logits, hg_out_g,
           w_out, w_up, w_down):
    assert w_ada.shape[0] == 1 and hg_lb_logits.shape[0] == 2, "single-layer step"
    b, t, _ = x_prompt.shape
    db, dt, _ = x_sample.shape
    n_phys = cache_k.shape[1]
    tm_p = 512

    w_in_bf = w_in[0].astype(BF16)
    w_in_t = w_in[0].T.astype(BF16)
    w_out_bf = w_out[0].astype(BF16)
    w_up_bf = w_up[0].astype(BF16)
    w_down_bf = w_down[0].astype(BF16)
    n1 = norm1_g[0].reshape(1, D_MODEL)
    n2 = norm2_g[0].reshape(1, D_MODEL)
    qg_t = jnp.tile(q_norm_g[0], N_HEADS).reshape(1, GROUP_W)
    kg_t = jnp.tile(k_norm_g[0], N_HEADS).reshape(1, GROUP_W)
    kg_cm = jnp.broadcast_to(kg_t.reshape(GROUP_W, 1), (GROUP_W, tm_p))
    og_t = jnp.tile(hg_out_g[0], N_HEADS).reshape(1, GROUP_W)
    bias2 = sb_bias[0].astype(F32) * LOG2E
    bias_bc = jnp.broadcast_to(jnp.repeat(bias2, SUBLANES)[:, None], (N_QROWS, LANES))
    idx = jnp.arange(2 * LANES)
    bd = (idx[:, None] // HEAD_DIM == idx[None, :] // HEAD_DIM).astype(BF16)
    kk = jnp.arange(LANES)
    tri = jnp.concatenate([(kk[:, None] >= kk[None, :]).astype(BF16), jnp.ones((LANES, LANES), BF16)], axis=1)

    ada = _ada(jnp.concatenate([c_prompt, c_sample], axis=0), w_ada[0], b_ada[0])
    mods_p = ada[:b].reshape(b, 1, N_ADA * D_MODEL)
    mods_s = jnp.tile(ada[b:], (dt, 1)).reshape(1, dt * db, N_ADA * D_MODEL)

    q, kt_p, vt_p, hq, lf, iv, sg = _proj(x_prompt, mods_p, n1, w_in_bf, w_in_t, qg_t, kg_cm,
                                           hg_lb_logits, bd, tm=tm_p)
    tri_pair = jnp.kron(jnp.eye(2, dtype=BF16), tri[:, :LANES])
    oa = _sb_prompt(q, kt_p, vt_p, bias2, tri_pair, b, t)
    n_hg = N_HEADS // HG_GROUP
    gw = HG_GROUP * HEAD_DIM
    ob, s_bd = _hgrn_prompt(hq, lf, iv, sg, jnp.zeros((b, n_hg, gw, gw), F32), og_t, bd, b, t, tc=1024)
    s_bd = s_bd.reshape(b, n_hg, HG_GROUP, HEAD_DIM, HG_GROUP, HEAD_DIM)
    s_p = jnp.stack([s_bd[:, :, j, :, j, :] for j in range(HG_GROUP)], axis=2)
    s_p = jnp.swapaxes(s_p, -1, -2).reshape(b, N_HEADS, HEAD_DIM, HEAD_DIM)
    oa_p, ob_p = oa.reshape(b, t, GROUP_W), ob

    assert db == LANES, "decode batch fills the lane dimension"
    ns = dt * db
    xs3 = jnp.transpose(x_sample, (1, 0, 2)).reshape(1, ns, D_MODEL)
    q, k_s, v_s, kt_s, vt_s, hq_t, lf_t, iv_t, sg_t = _proj_decode(
        xs3, mods_s, n1, w_in_bf, w_in_t, qg_t, kg_t, kg_cm, hg_lb_logits.T, bd, n_tok=dt)
    seq = lambda a: _pad_rows(jnp.transpose(a.reshape(dt, db, GROUP_W), (1, 0, 2)), SUBLANES)
    cm = lambda c: jnp.transpose(c[0], (0, 2, 3, 1)).reshape(n_phys, GROUP_W, PAGE)
    y_p, oa8 = _tail_decode(x_prompt, oa_p, ob_p, mods_p, n2, w_out_bf, w_up_bf, w_down_bf, 512,
                            seq(q), seq(k_s), seq(v_s), cm(cache_k), cm(cache_v), page_table, bias_bc, tri)
    oa = jnp.transpose(oa8[:, :dt], (1, 0, 2)).reshape(1, ns, GROUP_W)
    og_col = jnp.broadcast_to(hg_out_g[0][:, None], (HEAD_DIM, LANES))
    state_cm = jnp.transpose(state_hgrn[0], (1, 2, 3, 0))
    ob_t, s_s = _hgrn_step(hq_t, lf_t, iv_t, sg_t, state_cm, og_col, n_tok=dt)
    y_s = _tail(xs3, oa, ob_t[None], mods_s, n2, w_out_bf, w_up_bf, w_down_bf, tm=ns, ob_channel_major=True)
    y_s = jnp.transpose(y_s.reshape(dt, db, D_MODEL), (1, 0, 2))

    heads_cm = lambda a: jnp.transpose(a.reshape(b, N_HEADS, HEAD_DIM, t), (0, 3, 1, 2))[None]
    heads_tok = lambda a: jnp.transpose(a.reshape(dt, N_HEADS, HEAD_DIM, db), (3, 0, 1, 2))[None]
    return (y_p, y_s, heads_cm(kt_p), heads_cm(vt_p), heads_tok(kt_s), heads_tok(vt_s),
            s_p[None], jnp.transpose(s_s, (3, 0, 1, 2))[None])
```
